```python
import math
import jax
import jax.numpy as jnp
from jax import lax
import numpy as np

D_MODEL = 1024
BATCH = 32
SEQ = 256
DEPTH = 4
DEC_BATCH = 4
DEC_SEQ = 1024
PAST_LEN = 256

GRID_W = 64
FFN_DIM = 2816
NORM_EPS = 1e-6
ROPE_THETA = 10000.0
Q_BLOCK = 128
N_ADA = 9
NEG_INF = -1e30

MLA_HEADS = 4
MLA_Q_LORA = 256
MLA_KV_LORA = 128
MLA_NOPE = 64
MLA_ROPE = 32
MLA_V = 64
DIFF_HEADS = 4
DIFF_DK = 32
DIFF_DV = 2 * DIFF_DK
NAT_HEADS = 8
NAT_HD = 64
NAT_WIN_ROWS = 8
NAT_WIN_COLS = 16
NAT_QCOLS = 16
NAT_KCOLS = NAT_QCOLS + NAT_WIN_COLS

MLA_SCALE = (MLA_NOPE + MLA_ROPE) ** -0.5
DIFF_SCALE = DIFF_DK ** -0.5
NAT_SCALE = NAT_HD ** -0.5

SPLIT_1 = MLA_Q_LORA
SPLIT_2 = SPLIT_1 + MLA_KV_LORA
SPLIT_3 = SPLIT_2 + MLA_ROPE
SPLIT_4 = SPLIT_3 + DIFF_HEADS * 2 * DIFF_DK
SPLIT_5 = SPLIT_4 + DIFF_HEADS * 2 * DIFF_DK
SPLIT_6 = SPLIT_5 + DIFF_HEADS * DIFF_DV
SPLIT_7 = SPLIT_6 + NAT_HEADS * NAT_HD
SPLIT_8 = SPLIT_7 + NAT_HEADS * NAT_HD
IN_COLS = SPLIT_8 + NAT_HEADS * NAT_HD
IN_SPLITS = (SPLIT_1, SPLIT_2, SPLIT_3, SPLIT_4, SPLIT_5, SPLIT_6, SPLIT_7, SPLIT_8)
MIX_OUT = MLA_HEADS * MLA_V + DIFF_HEADS * DIFF_DV + NAT_HEADS * NAT_HD

kernel_name = 'hybrid_mla_diff_natten_prefix_flow_step'


def rmsnorm(x, g):
    xf = x.astype(jnp.float32)
    y = xf * lax.rsqrt(jnp.mean(xf * xf, axis=-1, keepdims=True) + NORM_EPS)
    return y.astype(x.dtype) * g


def modulate(x, shift, scale):
    return x * (1.0 + scale) + shift


def half_ffn(x, g, shift, scale, gate, w_gate, w_up, w_down):
    h = modulate(rmsnorm(x, g), shift, scale)
    return x + 0.5 * gate * ((jax.nn.silu(h @ w_gate) * (h @ w_up)) @ w_down)


def rope_1d(x, pos):
    half = x.shape[-1] // 2
    freqs = ROPE_THETA ** (-jnp.arange(half, dtype=jnp.float32) / half)
    ang = pos.astype(jnp.float32)[:, None] * freqs[None, :]
    ang = ang.reshape((ang.shape[0],) + (1,) * (x.ndim - 3) + (half,))
    cos = jnp.cos(ang).astype(x.dtype)
    sin = jnp.sin(ang).astype(x.dtype)
    x1, x2 = x[..., :half], x[..., half:]
    return jnp.concatenate([x1 * cos - x2 * sin, x1 * sin + x2 * cos], axis=-1)


def axial_rope(x):
    t = jnp.arange(x.shape[1])
    h = x.shape[-1] // 2
    return jnp.concatenate([rope_1d(x[..., :h], t // GRID_W), rope_1d(x[..., h:], t % GRID_W)], axis=-1)


def rope_pair(x):
    B, S, H, _ = x.shape
    return axial_rope(x.reshape(B, S, H, 2, DIFF_DK)).reshape(B, S, H, 2 * DIFF_DK)


def to_heads(x):
    return jnp.transpose(x, (0, 2, 1, 3))


def from_heads(x):
    B, H, S, d = x.shape
    return jnp.transpose(x, (0, 2, 1, 3)).reshape(B, S, H * d)


def over_query_blocks(fn, q):
    B, H, S, dq = q.shape
    nb = S // Q_BLOCK
    qb = jnp.moveaxis(q.reshape(B, H, nb, Q_BLOCK, dq), 2, 0)
    out = lax.map(fn, qb)
    return jnp.moveaxis(out, 0, 2).reshape(B, H, S, out.shape[-1])


def dense_attention(q, k, v, scale):
    def blk(qb):
        s = jnp.einsum('bhqd,bhkd->bhqk', qb, k).astype(jnp.float32) * scale
        p = jax.nn.softmax(s, axis=-1)
        return jnp.einsum('bhqk,bhkd->bhqd', p.astype(v.dtype), v)
    return over_query_blocks(blk, q)


def diff_attention(q, k, v, lam):
    k1, k2 = k[..., :DIFF_DK], k[..., DIFF_DK:]
    def blk(qb):
        s1 = jnp.einsum('bhqd,bhkd->bhqk', qb[..., :DIFF_DK], k1).astype(jnp.float32) * DIFF_SCALE
        s2 = jnp.einsum('bhqd,bhkd->bhqk', qb[..., DIFF_DK:], k2).astype(jnp.float32) * DIFF_SCALE
        p = jax.nn.softmax(s1, axis=-1) - lam * jax.nn.softmax(s2, axis=-1)
        return jnp.einsum('bhqk,bhkd->bhqd', p.astype(v.dtype), v)
    return over_query_blocks(blk, q)


def mixer_inputs(h, w_in, q_norm, w_uq, kv_norm):
    B, S, _ = h.shape
    c_q, c_kv, k_rope, dq, dk, dv, nq, nk, nv = jnp.split(h @ w_in, IN_SPLITS, axis=-1)
    q_mla = (rmsnorm(c_q, q_norm) @ w_uq).reshape(B, S, MLA_HEADS, MLA_NOPE + MLA_ROPE)
    ckv = rmsnorm(c_kv, kv_norm)
    dq = dq.reshape(B, S, DIFF_HEADS, 2 * DIFF_DK)
    dk = dk.reshape(B, S, DIFF_HEADS, 2 * DIFF_DK)
    dv = dv.reshape(B, S, DIFF_HEADS, DIFF_DV)
    nq = nq.reshape(B, S, NAT_HEADS, NAT_HD)
    nk = nk.reshape(B, S, NAT_HEADS, NAT_HD)
    nv = nv.reshape(B, S, NAT_HEADS, NAT_HD)
    return q_mla, ckv, k_rope, dq, dk, dv, nq, nk, nv


def mla_attend(q_mla, ckv_all, krope_all, w_ukv):
    B, K, _ = ckv_all.shape
    kv = (ckv_all @ w_ukv).reshape(B, K, MLA_HEADS, MLA_NOPE + MLA_V)
    k_nope, v = kv[..., :MLA_NOPE], kv[..., MLA_NOPE:]
    k = jnp.concatenate([k_nope, jnp.broadcast_to(krope_all[:, :, None, :], (B, K, MLA_HEADS, MLA_ROPE))], axis=-1)
    return from_heads(dense_attention(to_heads(q_mla), to_heads(k), to_heads(v), MLA_SCALE))


def diff_attend(dq, k_all, v_all, lq1, lk1, lq2, lk2, subln, lam_init):
    f32 = jnp.float32
    lam = (jnp.exp(jnp.sum(lq1.astype(f32) * lk1.astype(f32)))
           - jnp.exp(jnp.sum(lq2.astype(f32) * lk2.astype(f32))) + lam_init)
    o = diff_attention(to_heads(dq), k_all, v_all, lam)
    return from_heads(rmsnorm(o, subln) * (1.0 - lam_init))


def natten_latent(q, k, v, k_ctx, v_ctx, rpb):
    f32 = jnp.float32
    B, H, S, d = q.shape
    rows = S // GRID_W
    wr = min(NAT_WIN_ROWS, rows)
    n_cb = GRID_W // NAT_QCOLS
    r = jnp.arange(rows)
    row_idx = jnp.clip(r - wr // 2, 0, rows - wr)[:, None] + jnp.arange(wr)[None, :]
    cb = jnp.arange(n_cb)
    col_idx = (jnp.clip(cb * NAT_QCOLS - NAT_WIN_COLS // 2, 0, GRID_W - NAT_KCOLS)[:, None]
               + jnp.arange(NAT_KCOLS)[None, :])
    key_idx = (row_idx[:, None, :, None] * GRID_W + col_idx[None, :, None, :]).reshape(rows, n_cb, wr * NAT_KCOLS)
    k_win = jnp.take(k, key_idx, axis=2)
    v_win = jnp.take(v, key_idx, axis=2)
    q_blk = q.reshape(B, H, rows, n_cb, NAT_QCOLS, d)
    q_col = cb[:, None] * NAT_QCOLS + jnp.arange(NAT_QCOLS)[None, :]
    c_start = jnp.clip(q_col - NAT_WIN_COLS // 2, 0, GRID_W - NAT_WIN_COLS)
    kc = col_idx[:, None, :]
    in_win = (kc >= c_start[..., None]) & (kc < c_start[..., None] + NAT_WIN_COLS)
    dr = row_idx - r[:, None] + (NAT_WIN_ROWS - 1)
    dc = jnp.clip(kc - q_col[..., None], -(NAT_WIN_COLS - 1), NAT_WIN_COLS - 1) + (NAT_WIN_COLS - 1)
    bias = rpb.astype(f32)[:, dr[:, None, None, :, None], dc[None, :, :, None, :]]
    bias = jnp.where(in_win[None, None, :, :, None, :], bias, NEG_INF).reshape(H, rows, n_cb, NAT_QCOLS, wr * NAT_KCOLS)
    s_win = jnp.einsum('bhrcqd,bhrckd->bhrcqk', q_blk, k_win).astype(f32) * NAT_SCALE + bias[None]
    s_ctx = jnp.einsum('bhrcqd,bhld->bhrcql', q_blk, k_ctx).astype(f32) * NAT_SCALE
    p = jax.nn.softmax(jnp.concatenate([s_win, s_ctx], axis=-1), axis=-1).astype(v.dtype)
    n_win = wr * NAT_KCOLS
    o = (jnp.einsum('bhrcqk,bhrckd->bhrcqd', p[..., :n_win], v_win)
         + jnp.einsum('bhrcql,bhld->bhrcqd', p[..., n_win:], v_ctx))
    return o.reshape(B, H, S, d)


def setup_inputs(seed: int = 0) -> dict:
    key = jax.random.key(seed)
    ks = jax.random.split(key, 40)
    counter = iter(range(40))

    def nrm(shape, scale):
        return jax.random.normal(ks[next(counter)], shape, jnp.float32) * scale

    def gain(shape):
        return 1.0 + nrm(shape, 0.05)

    L = DEPTH
    return {
        'x_prompt': nrm((BATCH, SEQ, D_MODEL), 1.0),
        'x_sample': nrm((DEC_BATCH, DEC_SEQ, D_MODEL), 1.0),
        'cache_mla_ckv': nrm((DEC_BATCH, L, PAST_LEN, MLA_KV_LORA), 1.0),
        'cache_mla_krope': nrm((DEC_BATCH, L, PAST_LEN, MLA_ROPE), 1.0),
        'cache_diff_k': nrm((DEC_BATCH, L, DIFF_HEADS, PAST_LEN, 2 * DIFF_DK), 1.0),
        'cache_diff_v': nrm((DEC_BATCH, L, DIFF_HEADS, PAST_LEN, DIFF_DV), 1.0),
        'cache_nat_k': nrm((DEC_BATCH, L, NAT_HEADS, PAST_LEN, NAT_HD), 1.0),
        'cache_nat_v': nrm((DEC_BATCH, L, NAT_HEADS, PAST_LEN, NAT_HD), 1.0),
        'c': nrm((DEC_BATCH, D_MODEL), 1.0),
        'c_ctx': nrm((D_MODEL,), 1.0),
        'w_ada': nrm((L, D_MODEL, N_ADA * D_MODEL), 0.3 * D_MODEL ** -0.5),
        'b_ada': nrm((L, N_ADA * D_MODEL), 0.02),
        'ffn1_norm': gain((L, D_MODEL)),
        'ffn1_w_gate': nrm((L, D_MODEL, FFN_DIM), D_MODEL ** -0.5),
        'ffn1_w_up': nrm((L, D_MODEL, FFN_DIM), D_MODEL ** -0.5),
        'ffn1_w_down': nrm((L, FFN_DIM, D_MODEL), FFN_DIM ** -0.5),
        'mix_norm': gain((L, D_MODEL)),
        'w_in': nrm((L, D_MODEL, IN_COLS), D_MODEL ** -0.5),
        'mla_q_norm': gain((L, MLA_Q_LORA)),
        'mla_w_uq': nrm((L, MLA_Q_LORA, MLA_HEADS * (MLA_NOPE + MLA_ROPE)), MLA_Q_LORA ** -0.5),
        'mla_kv_norm': gain((L, MLA_KV_LORA)),
        'mla_w_ukv': nrm((L, MLA_KV_LORA, MLA_HEADS * (MLA_NOPE + MLA_V)), MLA_KV_LORA ** -0.5),
        'diff_lambda_q1': nrm((L, DIFF_DK), 0.1),
        'diff_lambda_k1': nrm((L, DIFF_DK), 0.1),
        'diff_lambda_q2': nrm((L, DIFF_DK), 0.1),
        'diff_lambda_k2': nrm((L, DIFF_DK), 0.1),
        'diff_subln': gain((L, DIFF_DV)),
        'nat_rpb': nrm((L, NAT_HEADS, 2 * NAT_WIN_ROWS - 1, 2 * NAT_WIN_COLS - 1), 0.1),
        'w_out': nrm((L, MIX_OUT, D_MODEL), MIX_OUT ** -0.5),
        'ffn2_norm': gain((L, D_MODEL)),
        'ffn2_w_gate': nrm((L, D_MODEL, FFN_DIM), D_MODEL ** -0.5),
        'ffn2_w_up': nrm((L, D_MODEL, FFN_DIM), D_MODEL ** -0.5),
        'ffn2_w_down': nrm((L, FFN_DIM, D_MODEL), FFN_DIM ** -0.5),
        'final_norm': gain((D_MODEL,)),
    }


def reference(x_prompt, x_sample, cache_mla_ckv, cache_mla_krope, cache_diff_k, cache_diff_v,
              cache_nat_k, cache_nat_v, c, c_ctx, w_ada, b_ada, ffn1_norm, ffn1_w_gate, ffn1_w_up,
              ffn1_w_down, mix_norm, w_in, mla_q_norm, mla_w_uq, mla_kv_norm, mla_w_ukv,
              diff_lambda_q1, diff_lambda_k1, diff_lambda_q2, diff_lambda_k2, diff_subln, nat_rpb,
              w_out, ffn2_norm, ffn2_w_gate, ffn2_w_up, ffn2_w_down, final_norm):
    x = x_prompt
    st_ckv, st_krope, st_dk, st_dv, st_nk, st_nv = [], [], [], [], [], []
    for l in range(DEPTH):
        lam_init = 0.8 - 0.6 * math.exp(-0.3 * l)
        ada = (jax.nn.silu(c_ctx) @ w_ada[l] + b_ada[l])[None, None, :]
        sh1, sc1, g1, sh2, sc2, g2, sh3, sc3, g3 = jnp.split(ada, N_ADA, axis=-1)
        x = half_ffn(x, ffn1_norm[l], sh1, sc1, g1, ffn1_w_gate[l], ffn1_w_up[l], ffn1_w_down[l])
        h = modulate(rmsnorm(x, mix_norm[l]), sh2, sc2)
        q_mla, ckv, k_rope, dq, dk, dv, nq, nk, nv = mixer_inputs(h, w_in[l], mla_q_norm[l], mla_w_uq[l], mla_kv_norm[l])
        dk_h, dv_h, nk_h, nv_h = to_heads(dk), to_heads(dv), to_heads(nk), to_heads(nv)
        o_mla = mla_attend(q_mla, ckv, k_rope, mla_w_ukv[l])
        o_diff = diff_attend(dq, dk_h, dv_h, diff_lambda_q1[l], diff_lambda_k1[l], diff_lambda_q2[l],
                             diff_lambda_k2[l], diff_subln[l], lam_init)
        o_nat = from_heads(dense_attention(to_heads(nq), nk_h, nv_h, NAT_SCALE))
        x = x + g2 * (jnp.concatenate([o_mla, o_diff, o_nat], axis=-1) @ w_out[l])
        x = half_ffn(x, ffn2_norm[l], sh3, sc3, g3, ffn2_w_gate[l], ffn2_w_up[l], ffn2_w_down[l])
        st_ckv.append(ckv)
        st_krope.append(k_rope)
        st_dk.append(dk_h)
        st_dv.append(dv_h)
        st_nk.append(nk_h)
        st_nv.append(nv_h)
    y_prompt = rmsnorm(x, final_norm)

    x = x_sample
    for l in range(DEPTH):
        lam_init = 0.8 - 0.6 * math.exp(-0.3 * l)
        ada = (jax.nn.silu(c) @ w_ada[l] + b_ada[l])[:, None, :]
        sh1, sc1, g1, sh2, sc2, g2, sh3, sc3, g3 = jnp.split(ada, N_ADA, axis=-1)
        x = half_ffn(x, ffn1_norm[l], sh1, sc1, g1, ffn1_w_gate[l], ffn1_w_up[l], ffn1_w_down[l])
        h = modulate(rmsnorm(x, mix_norm[l]), sh2, sc2)
        q_mla, ckv, k_rope, dq, dk, dv, nq, nk, nv = mixer_inputs(h, w_in[l], mla_q_norm[l], mla_w_uq[l], mla_kv_norm[l])
        q_mla = jnp.concatenate([q_mla[..., :MLA_NOPE], axial_rope(q_mla[..., MLA_NOPE:])], axis=-1)
        ckv_all = jnp.concatenate([cache_mla_ckv[:, l], ckv], axis=1)
        krope_all = jnp.concatenate([cache_mla_krope[:, l], axial_rope(k_rope[:, :, None, :])[:, :, 0, :]], axis=1)
        o_mla = mla_attend(q_mla, ckv_all, krope_all, mla_w_ukv[l])
        dk_all = jnp.concatenate([cache_diff_k[:, l], to_heads(rope_pair(dk))], axis=2)
        dv_all = jnp.concatenate([cache_diff_v[:, l], to_heads(dv)], axis=2)
        o_diff = diff_attend(rope_pair(dq), dk_all, dv_all, diff_lambda_q1[l], diff_lambda_k1[l],
                             diff_lambda_q2[l], diff_lambda_k2[l], diff_subln[l], lam_init)
        o_nat = from_heads(natten_latent(to_heads(nq), to_heads(nk), to_heads(nv),
                                         cache_nat_k[:, l], cache_nat_v[:, l], nat_rpb[l]))
        x = x + g2 * (jnp.concatenate([o_mla, o_diff, o_nat], axis=-1) @ w_out[l])
        x = half_ffn(x, ffn2_norm[l], sh3, sc3, g3, ffn2_w_gate[l], ffn2_w_up[l], ffn2_w_down[l])
    y_sample = rmsnorm(x, final_norm)

    return (y_prompt, y_sample, jnp.stack(st_ckv, axis=1), jnp.stack(st_krope, axis=1),
            jnp.stack(st_dk, axis=1), jnp.stack(st_dv, axis=1), jnp.stack(st_nk, axis=1),
            jnp.stack(st_nv, axis=1))
```

```python
import functools
import math

import numpy as np
import jax
import jax.numpy as jnp
from jax import lax
from jax.experimental import pallas as pl
from jax.experimental.pallas import tpu as pltpu

F32 = jnp.float32
BF16 = jnp.bfloat16

D_MODEL = 1024
FFN_DIM = 2816
NORM_EPS = 1e-6
ROPE_THETA = 10000.0
GRID_W = 64
N_ADA = 9
NEG_INF = -1e30

MLA_HEADS = 4
MLA_Q_LORA = 256
MLA_KV_LORA = 128
MLA_NOPE = 64
MLA_ROPE = 32
MLA_V = 64
DIFF_HEADS = 4
DIFF_DK = 32
DIFF_DV = 64
NAT_HEADS = 8
NAT_HD = 64
NAT_WIN_ROWS = 8
NAT_WIN_COLS = 16

MLA_SCALE = (MLA_NOPE + MLA_ROPE) ** -0.5
DIFF_SCALE = DIFF_DK ** -0.5
NAT_SCALE = NAT_HD ** -0.5

CTX_SEQ = 256
LAT_SEQ = 1024
LAT_ROWS = LAT_SEQ // GRID_W

LANES = 128
MXU_DIM = 256
VMEM_LIMIT_BYTES = 58 * 1024 * 1024

P_CQ = 0
P_CKV = 256
P_KR = 384
P_DQ = 512
P_DK = 768
P_DV = 1024
P_NQ = 1280
P_NK = 1792
P_NV = 2304
P_COLS = 2816
HEAD_BLOCK = 256
ADA_ROWS = 8

FFN_CHUNKS = ((0, 768), (768, 1536), (1536, 2304), (2304, 2816))


def _rope_tables():
    t = np.arange(LAT_SEQ)
    pos = np.stack([t // GRID_W, t % GRID_W], axis=0).astype(np.float64)
    lane = np.arange(LANES)
    p = lane % 32
    axis = (p >= 16).astype(np.int64)
    freqs = ROPE_THETA ** (-(p % 8).astype(np.float64) / 8.0)
    ang = pos[axis, :].T * freqs[None, :]
    first = (p % 16) < 8
    cos = np.cos(ang)
    sin = np.sin(ang)
    s_next = np.where(first[None, :], -sin, 0.0)
    s_prev = np.where(first[None, :], 0.0, sin)
    return cos.astype(np.float32), s_next.astype(np.float32), s_prev.astype(np.float32)


_ROPE_COS, _ROPE_SNEXT, _ROPE_SPREV = _rope_tables()


def _rmsnorm(x, g):
    ms = jnp.mean(x * x, axis=-1, keepdims=True)
    return x * lax.rsqrt(ms + NORM_EPS) * g


def _silu(x):
    return x / (1.0 + jnp.exp(-x))


def _dot(a, b):
    return jnp.dot(a, b, preferred_element_type=F32)


def _dot_t(a, b):
    return lax.dot_general(a, b, (((1,), (1,)), ((), ())), preferred_element_type=F32)


def _lane_mask(width, lo, hi):
    lane = lax.broadcasted_iota(jnp.int32, (1, width), 1)
    return (lane >= lo) & (lane < hi)


def _head_masks(width=HEAD_BLOCK, group=64, n=4):
    return [_lane_mask(width, h * group, (h + 1) * group) for h in range(n)]


def _mla_qmasks():
    lane = lax.broadcasted_iota(jnp.int32, (1, HEAD_BLOCK + LANES), 1)
    out = []
    for h in range(MLA_HEADS):
        nope = (lane >= h * MLA_NOPE) & (lane < (h + 1) * MLA_NOPE)
        rope = (lane >= HEAD_BLOCK + h * MLA_ROPE) & (lane < HEAD_BLOCK + (h + 1) * MLA_ROPE)
        out.append(nope | rope)
    return out


def _tile32(blk):
    return blk + pltpu.roll(blk, 32, 1) + pltpu.roll(blk, 64, 1) + pltpu.roll(blk, 96, 1)


def _rope(x, cos, s_next, s_prev):
    return x * cos + pltpu.roll(x, LANES - 8, 1) * s_next + pltpu.roll(x, 8, 1) * s_prev


def _softmax_heads(q, k_bf, v_bf, qmasks, omasks):
    out = None
    for qm, om in zip(qmasks, omasks):
        s = _dot_t(jnp.where(qm, q, jnp.zeros_like(q)), k_bf)
        e = jnp.exp(s - jnp.max(s, axis=-1, keepdims=True))
        l = jnp.sum(e, axis=-1, keepdims=True)
        o = jnp.where(om, _dot(e.astype(BF16), v_bf) / l, 0.0)
        out = o if out is None else out + o
    return out


def _diff_heads(q, k_bf, v_bf, lam, subln, lam_init):
    out = None
    for h in range(DIFF_HEADS):
        lo = h * 2 * DIFF_DK
        m1 = _lane_mask(HEAD_BLOCK, lo, lo + DIFF_DK)
        m2 = _lane_mask(HEAD_BLOCK, lo + DIFF_DK, lo + 2 * DIFF_DK)
        zero = jnp.zeros_like(q)
        s1 = _dot_t(jnp.where(m1, q, zero), k_bf)
        s2 = _dot_t(jnp.where(m2, q, zero), k_bf)
        e1 = jnp.exp(s1 - jnp.max(s1, axis=-1, keepdims=True))
        e2 = jnp.exp(s2 - jnp.max(s2, axis=-1, keepdims=True))
        r1 = 1.0 / jnp.sum(e1, axis=-1, keepdims=True)
        r2 = lam / jnp.sum(e2, axis=-1, keepdims=True)
        p = e1 * r1 - e2 * r2
        o = jnp.where(_lane_mask(HEAD_BLOCK, lo, lo + DIFF_DV), _dot(p.astype(BF16), v_bf), 0.0)
        ms = jnp.sum(o * o, axis=-1, keepdims=True) * (1.0 / DIFF_DV)
        o = o * lax.rsqrt(ms + NORM_EPS)
        out = o if out is None else out + o
    return out * subln * (1.0 - lam_init)


def _lambda(lamv, lam_init):
    a = jnp.sum(lamv[0:1] * lamv[1:2], axis=-1, keepdims=True)
    b = jnp.sum(lamv[2:3] * lamv[3:4], axis=-1, keepdims=True)
    return jnp.exp(a) - jnp.exp(b) + lam_init


def _ada_kernel(c_ref, w_ref, b_ref, o_ref):
    s = _silu(c_ref[...]).astype(BF16)
    o_ref[...] = _dot(s, w_ref[...].astype(BF16)) + b_ref[...]


def _ada_call(cvec, w_ada, b_ada):
    L, _, n = w_ada.shape
    tn = 1024
    return pl.pallas_call(
        _ada_kernel,
        grid=(L, n // tn),
        in_specs=[
            pl.BlockSpec((ADA_ROWS, D_MODEL), lambda l, j: (0, 0)),
            pl.BlockSpec((None, D_MODEL, tn), lambda l, j: (l, 0, j)),
            pl.BlockSpec((None, 1, tn), lambda l, j: (l, 0, j)),
        ],
        out_specs=pl.BlockSpec((None, ADA_ROWS, tn), lambda l, j: (l, 0, j)),
        out_shape=jax.ShapeDtypeStruct((L, ADA_ROWS, n), F32),
        compiler_params=pltpu.CompilerParams(
            dimension_semantics=("arbitrary", "arbitrary"), vmem_limit_bytes=VMEM_LIMIT_BYTES),
        name="ada",
    )(cvec, w_ada, b_ada.reshape(L, 1, n))


def _ffn_kernel(x_ref, mod_ref, g_ref, wg_ref, wu_ref, wd_ref, *rest, final):
    if final:
        fn_ref, o_ref = rest
    else:
        (o_ref,) = rest
    x = x_ref[...]
    mod = mod_ref[...]
    h = (_rmsnorm(x, g_ref[...]) * (1.0 + mod[1:2]) + mod[0:1]).astype(BF16)
    acc = None
    for lo, hi in FFN_CHUNKS:
        g = _dot(h, wg_ref[:, lo:hi])
        u = _dot(h, wu_ref[:, lo:hi])
        part = _dot((_silu(g) * u).astype(BF16), wd_ref[lo:hi, :])
        acc = part if acc is None else acc + part
    y = x + (0.5 * mod[2:3]) * acc
    if final:
        y = _rmsnorm(y, fn_ref[...])
    o_ref[...] = y


def _resident(shape, index_map):
    return pl.BlockSpec(shape, index_map, pipeline_mode=pl.Buffered(1))


def _ffn_call(x, ada5, layer, group, row_of_tile, norm, wg, wu, wd, final_norm, tm, name):
    T = x.shape[0]
    in_specs = [
        pl.BlockSpec((tm, D_MODEL), lambda t: (t, 0)),
        pl.BlockSpec((None, None, None, 3, D_MODEL), lambda t: (layer, row_of_tile(t), group, 0, 0)),
        _resident((None, 1, D_MODEL), lambda t: (layer, 0, 0)),
        _resident((None, D_MODEL, FFN_DIM), lambda t: (layer, 0, 0)),
        _resident((None, D_MODEL, FFN_DIM), lambda t: (layer, 0, 0)),
        _resident((None, FFN_DIM, D_MODEL), lambda t: (layer, 0, 0)),
    ]
    args = [x, ada5, norm, wg, wu, wd]
    if final_norm is not None:
        in_specs.append(_resident((1, D_MODEL), lambda t: (0, 0)))
        args.append(final_norm)
    return pl.pallas_call(
        functools.partial(_ffn_kernel, final=final_norm is not None),
        grid=(T // tm,),
        in_specs=in_specs,
        out_specs=pl.BlockSpec((tm, D_MODEL), lambda t: (t, 0)),
        out_shape=jax.ShapeDtypeStruct((T, D_MODEL), F32),
        compiler_params=pltpu.CompilerParams(
            dimension_semantics=("arbitrary",), vmem_limit_bytes=VMEM_LIMIT_BYTES),
        name=name,
    )(*args)


def _mix_i_kernel(x_ref, mod_ref, g_ref, win_ref, qn_ref, wuq_ref, kvn_ref, wukv_ref, lamv_ref, subln_ref,
                  wout_ref, ckv_any, kr_any, dk_any, dv_any, nk_any, nv_any,
                  y_ref, ckv_ref, kr_ref, dk_ref, dv_ref, nk_ref, nv_ref, *, nb, lam_init):
    del ckv_any, kr_any, dk_any, dv_any, nk_any, nv_any
    S = CTX_SEQ
    x = x_ref[...]
    mod = mod_ref[...]
    h = (_rmsnorm(x, g_ref[...]) * (1.0 + mod[1:2]) + mod[0:1]).astype(BF16)
    proj = _dot(h, win_ref[...])
    q_cat = _dot(_rmsnorm(proj[:, P_CQ:P_CQ + MLA_Q_LORA], qn_ref[...]).astype(BF16), wuq_ref[...])
    ckv = _rmsnorm(proj[:, P_CKV:P_CKV + MLA_KV_LORA], kvn_ref[...])
    kv = _dot(ckv.astype(BF16), wukv_ref[...])
    kr_blk = proj[:, P_KR:P_KR + LANES]
    k_cat = jnp.concatenate([kv[:, 0:HEAD_BLOCK], _tile32(kr_blk)], axis=1).astype(BF16)
    v_mla = kv[:, HEAD_BLOCK:2 * HEAD_BLOCK].astype(BF16)
    q_cat = (q_cat * MLA_SCALE).astype(BF16)
    dq = (proj[:, P_DQ:P_DQ + HEAD_BLOCK] * DIFF_SCALE).astype(BF16)
    dk = proj[:, P_DK:P_DK + HEAD_BLOCK]
    dv = proj[:, P_DV:P_DV + HEAD_BLOCK]
    nq = (proj[:, P_NQ:P_NQ + 2 * HEAD_BLOCK] * NAT_SCALE).astype(BF16)
    nk = proj[:, P_NK:P_NK + 2 * HEAD_BLOCK]
    nv = proj[:, P_NV:P_NV + 2 * HEAD_BLOCK]
    dk_bf, dv_bf, nk_bf, nv_bf = dk.astype(BF16), dv.astype(BF16), nk.astype(BF16), nv.astype(BF16)
    lam = _lambda(lamv_ref[...], lam_init)
    hmasks = _head_masks()
    mla_qm = _mla_qmasks()

    outs = []
    for j in range(nb):
        r0, r1 = j * S, (j + 1) * S
        ckv_ref[j] = ckv[r0:r1]
        kr_ref[j] = kr_blk[r0:r1, 0:MLA_ROPE]
        for hh in range(DIFF_HEADS):
            dk_ref[j, hh] = dk[r0:r1, hh * 64:(hh + 1) * 64]
            dv_ref[j, hh] = dv[r0:r1, hh * 64:(hh + 1) * 64]
        for hh in range(NAT_HEADS):
            nk_ref[j, hh] = nk[r0:r1, hh * 64:(hh + 1) * 64]
            nv_ref[j, hh] = nv[r0:r1, hh * 64:(hh + 1) * 64]
        o_mla = _softmax_heads(q_cat[r0:r1], k_cat[r0:r1], v_mla[r0:r1], mla_qm, hmasks)
        o_diff = _diff_heads(dq[r0:r1], dk_bf[r0:r1], dv_bf[r0:r1], lam, subln_ref[...], lam_init)
        o_nat = [
            _softmax_heads(nq[r0:r1, b * HEAD_BLOCK:(b + 1) * HEAD_BLOCK], nk_bf[r0:r1, b * HEAD_BLOCK:(b + 1) * HEAD_BLOCK],
                           nv_bf[r0:r1, b * HEAD_BLOCK:(b + 1) * HEAD_BLOCK], hmasks, hmasks)
            for b in range(2)
        ]
        outs.append(jnp.concatenate([o_mla, o_diff] + o_nat, axis=1).astype(BF16))
    o = jnp.concatenate(outs, axis=0) if nb > 1 else outs[0]
    y_ref[...] = x + mod[2:3] * _dot(o, wout_ref[...])


def _mix_i_call(x, ada5, layer, mix_norm, win, qn, wuq, kvn, wukv, lamv, subln, wout, states, nb, lam_init):
    T = x.shape[0]
    B = T // CTX_SEQ
    tm = nb * CTX_SEQ
    st_ckv, st_kr, st_dk, st_dv, st_nk, st_nv = states
    L = st_ckv.shape[1]
    lyr = lambda shape: _resident((None,) + shape, lambda i: (layer,) + (0,) * len(shape))
    any_spec = pl.BlockSpec(memory_space=pl.ANY)
    in_specs = [
        pl.BlockSpec((tm, D_MODEL), lambda i: (i, 0)),
        _resident((None, None, None, 3, D_MODEL), lambda i: (layer, 0, 1, 0, 0)),
        lyr((1, D_MODEL)),
        lyr((D_MODEL, P_COLS)),
        lyr((1, MLA_Q_LORA)),
        lyr((MLA_Q_LORA, HEAD_BLOCK + LANES)),
        lyr((1, MLA_KV_LORA)),
        lyr((MLA_KV_LORA, 2 * HEAD_BLOCK)),
        lyr((4, DIFF_DK)),
        lyr((1, HEAD_BLOCK)),
        lyr((D_MODEL, D_MODEL)),
    ] + [any_spec] * 6
    out_specs = [
        pl.BlockSpec((tm, D_MODEL), lambda i: (i, 0)),
        pl.BlockSpec((nb, None, CTX_SEQ, MLA_KV_LORA), lambda i: (i, layer, 0, 0)),
        pl.BlockSpec((nb, None, CTX_SEQ, MLA_ROPE), lambda i: (i, layer, 0, 0)),
        pl.BlockSpec((nb, None, DIFF_HEADS, CTX_SEQ, 2 * DIFF_DK), lambda i: (i, layer, 0, 0, 0)),
        pl.BlockSpec((nb, None, DIFF_HEADS, CTX_SEQ, DIFF_DV), lambda i: (i, layer, 0, 0, 0)),
        pl.BlockSpec((nb, None, NAT_HEADS, CTX_SEQ, NAT_HD), lambda i: (i, layer, 0, 0, 0)),
        pl.BlockSpec((nb, None, NAT_HEADS, CTX_SEQ, NAT_HD), lambda i: (i, layer, 0, 0, 0)),
    ]
    out_shape = [jax.ShapeDtypeStruct((T, D_MODEL), F32)] + [jax.ShapeDtypeStruct(s.shape, s.dtype) for s in states]
    res = pl.pallas_call(
        functools.partial(_mix_i_kernel, nb=nb, lam_init=lam_init),
        grid=(B // nb,),
        in_specs=in_specs,
        out_specs=out_specs,
        out_shape=out_shape,
        input_output_aliases={11 + k: 1 + k for k in range(6)},
        compiler_params=pltpu.CompilerParams(
            dimension_semantics=("arbitrary",), vmem_limit_bytes=VMEM_LIMIT_BYTES),
        name="mix_ctx",
    )(x, ada5, mix_norm, win, qn, wuq, kvn, wukv, lamv, subln, wout, *states)
    return res[0], tuple(res[1:])


PROJ_ROWS = 256
N_KEYS = CTX_SEQ + LAT_SEQ


def _mix_ii_kernel(x_ref, mod_ref, g_ref, win_ref, qn_ref, wuq_ref, kvn_ref, wukv_ref, lamv_ref, subln_ref,
                   wout_ref, cos_ref, snext_ref, sprev_ref, bias_ref,
                   cckv_ref, ckr_ref, cdk_ref, cdv_ref, cnk_ref, cnv_ref,
                   y_ref,
                   qcat_s, ckv_s, kcat_s, vmla_s, dq_s, dk_s, dv_s, nq_s, nk_s, nv_s, o_s, *, lam_init):
    S = LAT_SEQ
    C = CTX_SEQ
    mod = mod_ref[...]
    lam = _lambda(lamv_ref[...], lam_init)
    hmasks = _head_masks()
    mla_qm = _mla_qmasks()

    ckv_s[0:C, :] = cckv_ref[...]
    kcat_s[0:C, HEAD_BLOCK:] = ckr_ref[...]
    dk_s[0:C, :] = cdk_ref[...]
    dv_s[0:C, :] = cdv_ref[...]

    def proj_body(i, carry):
        r = pl.multiple_of(i * PROJ_ROWS, PROJ_ROWS)
        rows = pl.ds(r, PROJ_ROWS)
        krows = pl.ds(C + r, PROJ_ROWS)
        cos, s_next, s_prev = cos_ref[rows, :], snext_ref[rows, :], sprev_ref[rows, :]
        rope = lambda v: _rope(v, cos, s_next, s_prev)
        x = x_ref[rows, :]
        h = (_rmsnorm(x, g_ref[...]) * (1.0 + mod[1:2]) + mod[0:1]).astype(BF16)
        proj = _dot(h, win_ref[...])
        q_cat = _dot(_rmsnorm(proj[:, P_CQ:P_CQ + MLA_Q_LORA], qn_ref[...]).astype(BF16), wuq_ref[...])
        qcat_s[rows, 0:HEAD_BLOCK] = (q_cat[:, 0:HEAD_BLOCK] * MLA_SCALE).astype(BF16)
        qcat_s[rows, HEAD_BLOCK:] = (rope(q_cat[:, HEAD_BLOCK:]) * MLA_SCALE).astype(BF16)
        ckv_s[krows, :] = _rmsnorm(proj[:, P_CKV:P_CKV + MLA_KV_LORA], kvn_ref[...]).astype(BF16)
        kcat_s[krows, HEAD_BLOCK:] = _tile32(rope(proj[:, P_KR:P_KR + LANES])).astype(BF16)
        for b in range(2):
            c0 = b * LANES
            dq_s[rows, c0:c0 + LANES] = (rope(proj[:, P_DQ + c0:P_DQ + c0 + LANES]) * DIFF_SCALE).astype(BF16)
            dk_s[krows, c0:c0 + LANES] = rope(proj[:, P_DK + c0:P_DK + c0 + LANES]).astype(BF16)
        dv_s[krows, :] = proj[:, P_DV:P_DV + HEAD_BLOCK].astype(BF16)
        nq_s[rows, :] = (proj[:, P_NQ:P_NQ + 2 * HEAD_BLOCK] * NAT_SCALE).astype(BF16)
        nk_s[rows, :] = proj[:, P_NK:P_NK + 2 * HEAD_BLOCK].astype(BF16)
        nv_s[rows, :] = proj[:, P_NV:P_NV + 2 * HEAD_BLOCK].astype(BF16)
        return carry

    lax.fori_loop(0, S // PROJ_ROWS, proj_body, 0)

    def kv_body(i, carry):
        rows = pl.ds(pl.multiple_of(i * PROJ_ROWS, PROJ_ROWS), PROJ_ROWS)
        kv = _dot(ckv_s[rows, :], wukv_ref[...])
        kcat_s[rows, 0:HEAD_BLOCK] = kv[:, 0:HEAD_BLOCK].astype(BF16)
        vmla_s[rows, :] = kv[:, HEAD_BLOCK:].astype(BF16)
        return carry

    lax.fori_loop(0, N_KEYS // PROJ_ROWS, kv_body, 0)

    def dense_body(i, carry):
        rows = pl.ds(pl.multiple_of(i * PROJ_ROWS, PROJ_ROWS), PROJ_ROWS)
        o_mla = _softmax_heads(qcat_s[rows, :], kcat_s[...], vmla_s[...], mla_qm, hmasks)
        o_s[rows, 0:HEAD_BLOCK] = o_mla.astype(BF16)
        o_diff = _diff_heads(dq_s[rows, :], dk_s[...], dv_s[...], lam, subln_ref[...], lam_init)
        o_s[rows, HEAD_BLOCK:2 * HEAD_BLOCK] = o_diff.astype(BF16)
        return carry

    lax.fori_loop(0, S // PROJ_ROWS, dense_body, 0)

    for r in range(LAT_ROWS):
        rs = min(max(r - NAT_WIN_ROWS // 2, 0), LAT_ROWS - NAT_WIN_ROWS)
        dr0 = rs - r + (NAT_WIN_ROWS - 1)
        par = dr0 % 2
        off = GRID_W * (dr0 - par)
        q0, q1 = r * GRID_W, (r + 1) * GRID_W
        k0, k1 = rs * GRID_W, (rs + NAT_WIN_ROWS) * GRID_W
        for b in range(2):
            c0, c1 = b * HEAD_BLOCK, (b + 1) * HEAD_BLOCK
            q = nq_s[q0:q1, c0:c1]
            zero = jnp.zeros_like(q)
            qs = jnp.concatenate([jnp.where(m, q, zero) for m in hmasks], axis=0)
            sw = _dot_t(qs, nk_s[k0:k1, c0:c1]) + bias_ref[par, b, :, off:off + NAT_WIN_ROWS * GRID_W]
            sc = _dot_t(qs, cnk_ref[:, c0:c1])
            m = jnp.maximum(jnp.max(sw, axis=-1, keepdims=True), jnp.max(sc, axis=-1, keepdims=True))
            ew = jnp.exp(sw - m)
            ec = jnp.exp(sc - m)
            l = jnp.sum(ew, axis=-1, keepdims=True) + jnp.sum(ec, axis=-1, keepdims=True)
            o = (_dot(ew.astype(BF16), nv_s[k0:k1, c0:c1]) + _dot(ec.astype(BF16), cnv_ref[:, c0:c1])) / l
            of = None
            for hh, hm in enumerate(hmasks):
                part = jnp.where(hm, o[hh * GRID_W:(hh + 1) * GRID_W], 0.0)
                of = part if of is None else of + part
            o_s[q0:q1, 2 * HEAD_BLOCK + c0:2 * HEAD_BLOCK + c1] = of.astype(BF16)

    def out_body(i, carry):
        rows = pl.ds(pl.multiple_of(i * PROJ_ROWS, PROJ_ROWS), PROJ_ROWS)
        y_ref[rows, :] = x_ref[rows, :] + mod[2:3] * _dot(o_s[rows, :], wout_ref[...])
        return carry

    lax.fori_loop(0, S // PROJ_ROWS, out_body, 0)


def _mix_ii_call(x, ada5, layer, mix_norm, win, qn, wuq, kvn, wukv, lamv, subln, wout, rope_tabs, bias_tab,
                 caches, lam_init):
    T = x.shape[0]
    Bd = T // LAT_SEQ
    cckv, ckr, cdk, cdv, cnk, cnv = caches
    lyr = lambda shape: _resident((None,) + shape, lambda i: (layer,) + (0,) * len(shape))
    cache = lambda width: pl.BlockSpec((None, None, CTX_SEQ, width), lambda i: (i, layer, 0, 0))
    tab = _resident((LAT_SEQ, LANES), lambda i: (0, 0))
    in_specs = [
        pl.BlockSpec((LAT_SEQ, D_MODEL), lambda i: (i, 0), pipeline_mode=pl.Buffered(1)),
        pl.BlockSpec((None, None, None, 3, D_MODEL), lambda i: (layer, i + 1, 1, 0, 0)),
        lyr((1, D_MODEL)),
        lyr((D_MODEL, P_COLS)),
        lyr((1, MLA_Q_LORA)),
        lyr((MLA_Q_LORA, HEAD_BLOCK + LANES)),
        lyr((1, MLA_KV_LORA)),
        lyr((MLA_KV_LORA, 2 * HEAD_BLOCK)),
        lyr((4, DIFF_DK)),
        lyr((1, HEAD_BLOCK)),
        lyr((D_MODEL, D_MODEL)),
        tab, tab, tab,
        _resident((2, 2, HEAD_BLOCK, 1024), lambda i: (0, 0, 0, 0)),
        cache(MLA_KV_LORA), cache(LANES), cache(HEAD_BLOCK), cache(HEAD_BLOCK),
        cache(2 * HEAD_BLOCK), cache(2 * HEAD_BLOCK),
    ]
    scratch = [
        pltpu.VMEM((LAT_SEQ, HEAD_BLOCK + LANES), BF16),
        pltpu.VMEM((N_KEYS, MLA_KV_LORA), BF16),
        pltpu.VMEM((N_KEYS, HEAD_BLOCK + LANES), BF16),
        pltpu.VMEM((N_KEYS, HEAD_BLOCK), BF16),
        pltpu.VMEM((LAT_SEQ, HEAD_BLOCK), BF16),
        pltpu.VMEM((N_KEYS, HEAD_BLOCK), BF16),
        pltpu.VMEM((N_KEYS, HEAD_BLOCK), BF16),
        pltpu.VMEM((LAT_SEQ, 2 * HEAD_BLOCK), BF16),
        pltpu.VMEM((LAT_SEQ, 2 * HEAD_BLOCK), BF16),
        pltpu.VMEM((LAT_SEQ, 2 * HEAD_BLOCK), BF16),
        pltpu.VMEM((LAT_SEQ, D_MODEL), BF16),
    ]
    return pl.pallas_call(
        functools.partial(_mix_ii_kernel, lam_init=lam_init),
        grid=(Bd,),
        in_specs=in_specs,
        out_specs=pl.BlockSpec((LAT_SEQ, D_MODEL), lambda i: (i, 0)),
        out_shape=jax.ShapeDtypeStruct((T, D_MODEL), F32),
        scratch_shapes=scratch,
        compiler_params=pltpu.CompilerParams(
            dimension_semantics=("arbitrary",), vmem_limit_bytes=VMEM_LIMIT_BYTES),
        name="mix_lat",
    )(x, ada5, mix_norm, win, qn, wuq, kvn, wukv, lamv, subln, wout, *rope_tabs, bias_tab, *caches)


def _nat_bias_table(rpb):
    qc = np.arange(GRID_W)[:, None]
    kc = np.arange(GRID_W)[None, :]
    c_start = np.clip(qc - NAT_WIN_COLS // 2, 0, GRID_W - NAT_WIN_COLS)
    in_win = (kc >= c_start) & (kc < c_start + NAT_WIN_COLS)
    dc = np.clip(kc - qc, -(NAT_WIN_COLS - 1), NAT_WIN_COLS - 1) + (NAT_WIN_COLS - 1)
    t = jnp.where(in_win[None, None], rpb.astype(F32)[:, :, dc], NEG_INF)
    n_dr = 2 * NAT_WIN_ROWS - 1
    t = jnp.transpose(t, (0, 2, 1, 3)).reshape(NAT_HEADS, GRID_W, n_dr * GRID_W)
    even = jnp.pad(t, ((0, 0), (0, 0), (0, 1024 - n_dr * GRID_W)))
    odd = jnp.pad(t[:, :, GRID_W:], ((0, 0), (0, 0), (0, 1024 - (n_dr - 1) * GRID_W)))
    return jnp.stack([even, odd]).reshape(2, 2, HEAD_BLOCK, 1024)


def _heads_to_lanes(cache):
    B, L, H, S, d = cache.shape
    return jnp.transpose(cache, (0, 1, 3, 2, 4)).reshape(B, L, S, H * d).astype(BF16)


def kernel(x_prompt, x_sample, cache_mla_ckv, cache_mla_krope, cache_diff_k, cache_diff_v, cache_nat_k, cache_nat_v, c, c_ctx, w_ada, b_ada, ffn1_norm, ffn1_w_gate, ffn1_w_up, ffn1_w_down, mix_norm, w_in, mla_q_norm, mla_w_uq, mla_kv_norm, mla_w_ukv, diff_lambda_q1, diff_lambda_k1, diff_lambda_q2, diff_lambda_k2, diff_subln, nat_rpb, w_out, ffn2_norm, ffn2_w_gate, ffn2_w_up, ffn2_w_down, final_norm):
    L = w_ada.shape[0]
    B, S_ctx, _ = x_prompt.shape
    Bd, S_lat, _ = x_sample.shape
    assert S_ctx == CTX_SEQ and S_lat == LAT_SEQ and 1 + Bd <= ADA_ROWS
    assert cache_mla_ckv.shape[2] == CTX_SEQ

    cvec = jnp.concatenate([c_ctx[None, :], c, jnp.zeros((ADA_ROWS - 1 - Bd, D_MODEL), F32)], axis=0)
    ada5 = _ada_call(cvec, w_ada, b_ada).reshape(L, ADA_ROWS, 3, 3, D_MODEL)

    bf = lambda w: w.astype(BF16)
    w1g, w1u, w1d = bf(ffn1_w_gate), bf(ffn1_w_up), bf(ffn1_w_down)
    w2g, w2u, w2d = bf(ffn2_w_gate), bf(ffn2_w_up), bf(ffn2_w_down)
    split = MLA_Q_LORA + MLA_KV_LORA + MLA_ROPE
    win = bf(jnp.concatenate(
        [w_in[:, :, :split], jnp.zeros((L, D_MODEL, LANES - MLA_ROPE), F32), w_in[:, :, split:]], axis=2))
    wuq4 = mla_w_uq.reshape(L, MLA_Q_LORA, MLA_HEADS, MLA_NOPE + MLA_ROPE)
    wuq = bf(jnp.concatenate([wuq4[..., :MLA_NOPE].reshape(L, MLA_Q_LORA, -1),
                              wuq4[..., MLA_NOPE:].reshape(L, MLA_Q_LORA, -1)], axis=2))
    wukv4 = mla_w_ukv.reshape(L, MLA_KV_LORA, MLA_HEADS, MLA_NOPE + MLA_V)
    wukv = bf(jnp.concatenate([wukv4[..., :MLA_NOPE].reshape(L, MLA_KV_LORA, -1),
                               wukv4[..., MLA_NOPE:].reshape(L, MLA_KV_LORA, -1)], axis=2))
    wout = bf(w_out)
    lamv = jnp.stack([diff_lambda_q1, diff_lambda_k1, diff_lambda_q2, diff_lambda_k2], axis=1)
    subln = jnp.tile(diff_subln, (1, DIFF_HEADS)).reshape(L, 1, HEAD_BLOCK)
    r3 = lambda a: a.reshape(L, 1, a.shape[-1])
    n1, n2, nm, qn, kvn = r3(ffn1_norm), r3(ffn2_norm), r3(mix_norm), r3(mla_q_norm), r3(mla_kv_norm)
    fnorm = final_norm.reshape(1, D_MODEL)

    caches = (bf(cache_mla_ckv), bf(jnp.tile(cache_mla_krope, (1, 1, 1, LANES // MLA_ROPE))),
              _heads_to_lanes(cache_diff_k), _heads_to_lanes(cache_diff_v),
              _heads_to_lanes(cache_nat_k), _heads_to_lanes(cache_nat_v))
    rope_tabs = (jnp.asarray(_ROPE_COS), jnp.asarray(_ROPE_SNEXT), jnp.asarray(_ROPE_SPREV))

    states = (jnp.zeros((B, L, CTX_SEQ, MLA_KV_LORA), F32), jnp.zeros((B, L, CTX_SEQ, MLA_ROPE), F32),
              jnp.zeros((B, L, DIFF_HEADS, CTX_SEQ, 2 * DIFF_DK), F32), jnp.zeros((B, L, DIFF_HEADS, CTX_SEQ, DIFF_DV), F32),
              jnp.zeros((B, L, NAT_HEADS, CTX_SEQ, NAT_HD), F32), jnp.zeros((B, L, NAT_HEADS, CTX_SEQ, NAT_HD), F32))

    xi = x_prompt.reshape(B * CTX_SEQ, D_MODEL)
    xs = x_sample.reshape(Bd * LAT_SEQ, D_MODEL)
    tm = 512
    nb = 2 if B % 2 == 0 else 1
    ctx_row = lambda t: 0
    lat_row = lambda t: 1 + t // (LAT_SEQ // tm)
    for l in range(L):
        lam_init = 0.8 - 0.6 * math.exp(-0.3 * l)
        last = l == L - 1
        xi = _ffn_call(xi, ada5, l, 0, ctx_row, n1, w1g, w1u, w1d, None, tm, "ffn1_ctx")
        xs = _ffn_call(xs, ada5, l, 0, lat_row, n1, w1g, w1u, w1d, None, tm, "ffn1_lat")
        xi, states = _mix_i_call(xi, ada5, l, nm, win, qn, wuq, kvn, wukv, lamv, subln, wout, states, nb, lam_init)
        xs = _mix_ii_call(xs, ada5, l, nm, win, qn, wuq, kvn, wukv, lamv, subln, wout, rope_tabs,
                          _nat_bias_table(nat_rpb[l]), caches, lam_init)
        xi = _ffn_call(xi, ada5, l, 2, ctx_row, n2, w2g, w2u, w2d, fnorm if last else None, tm, "ffn2_ctx")
        xs = _ffn_call(xs, ada5, l, 2, lat_row, n2, w2g, w2u, w2d, fnorm if last else None, tm, "ffn2_lat")

    return (xi.reshape(B, CTX_SEQ, D_MODEL), xs.reshape(Bd, LAT_SEQ, D_MODEL)) + tuple(states)
```

```python
import functools
import math

import numpy as np
import jax
import jax.numpy as jnp
from jax import lax
from jax.experimental import pallas as pl
from jax.experimental.pallas import tpu as pltpu

F32 = jnp.float32
BF16 = jnp.bfloat16

D_MODEL = 1024
FFN_DIM = 2816
NORM_EPS = 1e-6
ROPE_THETA = 10000.0
GRID_W = 64
N_ADA = 9
NEG_INF = -1e30

MLA_HEADS = 4
MLA_Q_LORA = 256
MLA_KV_LORA = 128
MLA_NOPE = 64
MLA_ROPE = 32
MLA_V = 64
DIFF_HEADS = 4
DIFF_DK = 32
DIFF_DV = 64
NAT_HEADS = 8
NAT_HD = 64
NAT_WIN_ROWS = 8
NAT_WIN_COLS = 16

MLA_SCALE = (MLA_NOPE + MLA_ROPE) ** -0.5
DIFF_SCALE = DIFF_DK ** -0.5
NAT_SCALE = NAT_HD ** -0.5

CTX_SEQ = 256
LAT_SEQ = 1024
LAT_ROWS = LAT_SEQ // GRID_W

LANES = 128
MXU_DIM = 256
VMEM_LIMIT_BYTES = 58 * 1024 * 1024

P_CQ = 0
P_CKV = 256
P_KR = 384
P_DQ = 512
P_DK = 768
P_DV = 1024
P_NQ = 1280
P_NK = 1792
P_NV = 2304
P_COLS = 2816
HEAD_BLOCK = 256
ADA_ROWS = 8

FFN_CHUNKS = ((0, 768), (768, 1536), (1536, 2304), (2304, 2816))


def _rope_tables():
    t = np.arange(LAT_SEQ)
    pos = np.stack([t // GRID_W, t % GRID_W], axis=0).astype(np.float64)
    lane = np.arange(LANES)
    p = lane % 32
    axis = (p >= 16).astype(np.int64)
    freqs = ROPE_THETA ** (-(p % 8).astype(np.float64) / 8.0)
    ang = pos[axis, :].T * freqs[None, :]
    first = (p % 16) < 8
    cos = np.cos(ang)
    sin = np.sin(ang)
    s_next = np.where(first[None, :], -sin, 0.0)
    s_prev = np.where(first[None, :], 0.0, sin)
    return cos.astype(np.float32), s_next.astype(np.float32), s_prev.astype(np.float32)


_ROPE_COS, _ROPE_SNEXT, _ROPE_SPREV = _rope_tables()


def _rmsnorm(x, g):
    ms = jnp.mean(x * x, axis=-1, keepdims=True)
    return x * lax.rsqrt(ms + NORM_EPS) * g


def _silu(x):
    return x / (1.0 + jnp.exp(-x))


def _dot(a, b):
    return jnp.dot(a, b, preferred_element_type=F32)


def _dot_t(a, b):
    return lax.dot_general(a, b, (((1,), (1,)), ((), ())), preferred_element_type=F32)


def _lane_mask(width, lo, hi):
    lane = lax.broadcasted_iota(jnp.int32, (1, width), 1)
    return (lane >= lo) & (lane < hi)


def _head_masks(width=HEAD_BLOCK, group=64, n=4):
    return [_lane_mask(width, h * group, (h + 1) * group) for h in range(n)]


def _mla_qmasks():
    lane = lax.broadcasted_iota(jnp.int32, (1, HEAD_BLOCK + LANES), 1)
    out = []
    for h in range(MLA_HEADS):
        nope = (lane >= h * MLA_NOPE) & (lane < (h + 1) * MLA_NOPE)
        rope = (lane >= HEAD_BLOCK + h * MLA_ROPE) & (lane < HEAD_BLOCK + (h + 1) * MLA_ROPE)
        out.append(nope | rope)
    return out


def _tile32(blk):
    return blk + pltpu.roll(blk, 32, 1) + pltpu.roll(blk, 64, 1) + pltpu.roll(blk, 96, 1)


def _rope(x, cos, s_next, s_prev):
    return x * cos + pltpu.roll(x, LANES - 8, 1) * s_next + pltpu.roll(x, 8, 1) * s_prev


def _stack_masked(q, masks):
    zero = jnp.zeros_like(q)
    return jnp.concatenate([jnp.where(m, q, zero) for m in masks], axis=0)


def _softmax_heads(q, k_bf, v_bf, qmasks, omasks):
    sq = q.shape[0]
    s = _dot_t(_stack_masked(q, qmasks), k_bf)
    e = jnp.exp(s - jnp.max(s, axis=-1, keepdims=True))
    l = jnp.sum(e, axis=-1, keepdims=True)
    o = _dot(e.astype(BF16), v_bf) / l
    out = None
    for h, om in enumerate(omasks):
        part = jnp.where(om, o[h * sq:(h + 1) * sq], 0.0)
        out = part if out is None else out + part
    return out


def _diff_heads(q, k_bf, v_bf, lam, subln, lam_init):
    sq = q.shape[0]
    n = DIFF_HEADS
    m1 = [_lane_mask(HEAD_BLOCK, h * 2 * DIFF_DK, h * 2 * DIFF_DK + DIFF_DK) for h in range(n)]
    m2 = [_lane_mask(HEAD_BLOCK, h * 2 * DIFF_DK + DIFF_DK, (h + 1) * 2 * DIFF_DK) for h in range(n)]
    s = _dot_t(_stack_masked(q, m1 + m2), k_bf)
    e = jnp.exp(s - jnp.max(s, axis=-1, keepdims=True))
    r = 1.0 / jnp.sum(e, axis=-1, keepdims=True)
    p = e[:n * sq] * r[:n * sq] - e[n * sq:] * (lam * r[n * sq:])
    o = _dot(p.astype(BF16), v_bf)
    out = None
    for h in range(n):
        oh = jnp.where(_lane_mask(HEAD_BLOCK, h * DIFF_DV, (h + 1) * DIFF_DV), o[h * sq:(h + 1) * sq], 0.0)
        ms = jnp.sum(oh * oh, axis=-1, keepdims=True) * (1.0 / DIFF_DV)
        oh = oh * lax.rsqrt(ms + NORM_EPS)
        out = oh if out is None else out + oh
    return out * subln * (1.0 - lam_init)


def _lambda(lamv, lam_init):
    a = jnp.sum(lamv[0:1] * lamv[1:2], axis=-1, keepdims=True)
    b = jnp.sum(lamv[2:3] * lamv[3:4], axis=-1, keepdims=True)
    return jnp.exp(a) - jnp.exp(b) + lam_init


def _ada_kernel(c_ref, w_ref, b_ref, o_ref):
    s = _silu(c_ref[...]).astype(BF16)
    o_ref[...] = _dot(s, w_ref[...].astype(BF16)) + b_ref[...]


def _ada_call(cvec, w_ada, b_ada):
    L, _, n = w_ada.shape
    tn = 1024
    return pl.pallas_call(
        _ada_kernel,
        grid=(L, n // tn),
        in_specs=[
            pl.BlockSpec((ADA_ROWS, D_MODEL), lambda l, j: (0, 0)),
            pl.BlockSpec((None, D_MODEL, tn), lambda l, j: (l, 0, j)),
            pl.BlockSpec((None, 1, tn), lambda l, j: (l, 0, j)),
        ],
        out_specs=pl.BlockSpec((None, ADA_ROWS, tn), lambda l, j: (l, 0, j)),
        out_shape=jax.ShapeDtypeStruct((L, ADA_ROWS, n), F32),
        compiler_params=pltpu.CompilerParams(
            dimension_semantics=("arbitrary", "arbitrary"), vmem_limit_bytes=VMEM_LIMIT_BYTES),
        name="ada",
    )(cvec, w_ada, b_ada.reshape(L, 1, n))


def _ffn_kernel(x_ref, mod_ref, g_ref, wg_ref, wu_ref, wd_ref, *rest, final):
    if final:
        fn_ref, o_ref = rest
    else:
        (o_ref,) = rest
    x = x_ref[...]
    mod = mod_ref[...]
    h = (_rmsnorm(x, g_ref[...]) * (1.0 + mod[1:2]) + mod[0:1]).astype(BF16)
    acc = None
    for lo, hi in FFN_CHUNKS:
        g = _dot(h, wg_ref[:, lo:hi])
        u = _dot(h, wu_ref[:, lo:hi])
        part = _dot((_silu(g) * u).astype(BF16), wd_ref[lo:hi, :])
        acc = part if acc is None else acc + part
    y = x + (0.5 * mod[2:3]) * acc
    if final:
        y = _rmsnorm(y, fn_ref[...])
    o_ref[...] = y


def _resident(shape, index_map):
    return pl.BlockSpec(shape, index_map, pipeline_mode=pl.Buffered(1))


def _ffn_call(x, ada5, layer, group, row_of_tile, norm, wg, wu, wd, final_norm, tm, name):
    T = x.shape[0]
    in_specs = [
        pl.BlockSpec((tm, D_MODEL), lambda t: (t, 0)),
        pl.BlockSpec((None, None, None, 3, D_MODEL), lambda t: (layer, row_of_tile(t), group, 0, 0)),
        _resident((None, 1, D_MODEL), lambda t: (layer, 0, 0)),
        _resident((None, D_MODEL, FFN_DIM), lambda t: (layer, 0, 0)),
        _resident((None, D_MODEL, FFN_DIM), lambda t: (layer, 0, 0)),
        _resident((None, FFN_DIM, D_MODEL), lambda t: (layer, 0, 0)),
    ]
    args = [x, ada5, norm, wg, wu, wd]
    if final_norm is not None:
        in_specs.append(_resident((1, D_MODEL), lambda t: (0, 0)))
        args.append(final_norm)
    return pl.pallas_call(
        functools.partial(_ffn_kernel, final=final_norm is not None),
        grid=(T // tm,),
        in_specs=in_specs,
        out_specs=pl.BlockSpec((tm, D_MODEL), lambda t: (t, 0)),
        out_shape=jax.ShapeDtypeStruct((T, D_MODEL), F32),
        compiler_params=pltpu.CompilerParams(
            dimension_semantics=("arbitrary",), vmem_limit_bytes=VMEM_LIMIT_BYTES),
        name=name,
    )(*args)


def _mix_i_kernel(x_ref, mod_ref, g_ref, win_ref, qn_ref, wuq_ref, kvn_ref, wukv_ref, lamv_ref, subln_ref,
                  wout_ref, ckv_any, kr_any, dk_any, dv_any, nk_any, nv_any,
                  y_ref, ckv_ref, kr_ref, dk_ref, dv_ref, nk_ref, nv_ref, *, nb, lam_init):
    del ckv_any, kr_any, dk_any, dv_any, nk_any, nv_any
    S = CTX_SEQ
    x = x_ref[...]
    mod = mod_ref[...]
    h = (_rmsnorm(x, g_ref[...]) * (1.0 + mod[1:2]) + mod[0:1]).astype(BF16)
    proj = _dot(h, win_ref[...])
    q_cat = _dot(_rmsnorm(proj[:, P_CQ:P_CQ + MLA_Q_LORA], qn_ref[...]).astype(BF16), wuq_ref[...])
    ckv = _rmsnorm(proj[:, P_CKV:P_CKV + MLA_KV_LORA], kvn_ref[...])
    kv = _dot(ckv.astype(BF16), wukv_ref[...])
    kr_blk = proj[:, P_KR:P_KR + LANES]
    k_cat = jnp.concatenate([kv[:, 0:HEAD_BLOCK], _tile32(kr_blk)], axis=1).astype(BF16)
    v_mla = kv[:, HEAD_BLOCK:2 * HEAD_BLOCK].astype(BF16)
    q_cat = (q_cat * MLA_SCALE).astype(BF16)
    dq = (proj[:, P_DQ:P_DQ + HEAD_BLOCK] * DIFF_SCALE).astype(BF16)
    dk = proj[:, P_DK:P_DK + HEAD_BLOCK]
    dv = proj[:, P_DV:P_DV + HEAD_BLOCK]
    nq = (proj[:, P_NQ:P_NQ + 2 * HEAD_BLOCK] * NAT_SCALE).astype(BF16)
    nk = proj[:, P_NK:P_NK + 2 * HEAD_BLOCK]
    nv = proj[:, P_NV:P_NV + 2 * HEAD_BLOCK]
    dk_bf, dv_bf, nk_bf, nv_bf = dk.astype(BF16), dv.astype(BF16), nk.astype(BF16), nv.astype(BF16)
    lam = _lambda(lamv_ref[...], lam_init)
    hmasks = _head_masks()
    mla_qm = _mla_qmasks()

    outs = []
    for j in range(nb):
        r0, r1 = j * S, (j + 1) * S
        ckv_ref[j] = ckv[r0:r1]
        kr_ref[j] = kr_blk[r0:r1, 0:MLA_ROPE]
        for hh in range(DIFF_HEADS):
            dk_ref[j, hh] = dk[r0:r1, hh * 64:(hh + 1) * 64]
            dv_ref[j, hh] = dv[r0:r1, hh * 64:(hh + 1) * 64]
        for hh in range(NAT_HEADS):
            nk_ref[j, hh] = nk[r0:r1, hh * 64:(hh + 1) * 64]
            nv_ref[j, hh] = nv[r0:r1, hh * 64:(hh + 1) * 64]
        o_mla = _softmax_heads(q_cat[r0:r1], k_cat[r0:r1], v_mla[r0:r1], mla_qm, hmasks)
        o_diff = _diff_heads(dq[r0:r1], dk_bf[r0:r1], dv_bf[r0:r1], lam, subln_ref[...], lam_init)
        o_nat = [
            _softmax_heads(nq[r0:r1, b * HEAD_BLOCK:(b + 1) * HEAD_BLOCK], nk_bf[r0:r1, b * HEAD_BLOCK:(b + 1) * HEAD_BLOCK],
                           nv_bf[r0:r1, b * HEAD_BLOCK:(b + 1) * HEAD_BLOCK], hmasks, hmasks)
            for b in range(2)
        ]
        outs.append(jnp.concatenate([o_mla, o_diff] + o_nat, axis=1).astype(BF16))
    o = jnp.concatenate(outs, axis=0) if nb > 1 else outs[0]
    y_ref[...] = x + mod[2:3] * _dot(o, wout_ref[...])


def _mix_i_call(x, ada5, layer, mix_norm, win, qn, wuq, kvn, wukv, lamv, subln, wout, states, nb, lam_init):
    T = x.shape[0]
    B = T // CTX_SEQ
    tm = nb * CTX_SEQ
    st_ckv, st_kr, st_dk, st_dv, st_nk, st_nv = states
    L = st_ckv.shape[1]
    lyr = lambda shape: _resident((None,) + shape, lambda i: (layer,) + (0,) * len(shape))
    any_spec = pl.BlockSpec(memory_space=pl.ANY)
    in_specs = [
        pl.BlockSpec((tm, D_MODEL), lambda i: (i, 0)),
        _resident((None, None, None, 3, D_MODEL), lambda i: (layer, 0, 1, 0, 0)),
        lyr((1, D_MODEL)),
        lyr((D_MODEL, P_COLS)),
        lyr((1, MLA_Q_LORA)),
        lyr((MLA_Q_LORA, HEAD_BLOCK + LANES)),
        lyr((1, MLA_KV_LORA)),
        lyr((MLA_KV_LORA, 2 * HEAD_BLOCK)),
        lyr((4, DIFF_DK)),
        lyr((1, HEAD_BLOCK)),
        lyr((D_MODEL, D_MODEL)),
    ] + [any_spec] * 6
    out_specs = [
        pl.BlockSpec((tm, D_MODEL), lambda i: (i, 0)),
        pl.BlockSpec((nb, None, CTX_SEQ, MLA_KV_LORA), lambda i: (i, layer, 0, 0)),
        pl.BlockSpec((nb, None, CTX_SEQ, MLA_ROPE), lambda i: (i, layer, 0, 0)),
        pl.BlockSpec((nb, None, DIFF_HEADS, CTX_SEQ, 2 * DIFF_DK), lambda i: (i, layer, 0, 0, 0)),
        pl.BlockSpec((nb, None, DIFF_HEADS, CTX_SEQ, DIFF_DV), lambda i: (i, layer, 0, 0, 0)),
        pl.BlockSpec((nb, None, NAT_HEADS, CTX_SEQ, NAT_HD), lambda i: (i, layer, 0, 0, 0)),
        pl.BlockSpec((nb, None, NAT_HEADS, CTX_SEQ, NAT_HD), lambda i: (i, layer, 0, 0, 0)),
    ]
    out_shape = [jax.ShapeDtypeStruct((T, D_MODEL), F32)] + [jax.ShapeDtypeStruct(s.shape, s.dtype) for s in states]
    res = pl.pallas_call(
        functools.partial(_mix_i_kernel, nb=nb, lam_init=lam_init),
        grid=(B // nb,),
        in_specs=in_specs,
        out_specs=out_specs,
        out_shape=out_shape,
        input_output_aliases={11 + k: 1 + k for k in range(6)},
        compiler_params=pltpu.CompilerParams(
            dimension_semantics=("arbitrary",), vmem_limit_bytes=VMEM_LIMIT_BYTES),
        name="mix_ctx",
    )(x, ada5, mix_norm, win, qn, wuq, kvn, wukv, lamv, subln, wout, *states)
    return res[0], tuple(res[1:])


PROJ_ROWS = 256
MLA_QROWS = 128
DIFF_QROWS = 64
N_KEYS = CTX_SEQ + LAT_SEQ
BIAS_LANES = NAT_WIN_ROWS * LANES


def _mix_ii_kernel(x_ref, mod_ref, g_ref, win_ref, qn_ref, wuq_ref, kvn_ref, wukv_ref, lamv_ref, subln_ref,
                   wout_ref, cos_ref, snext_ref, sprev_ref, rpb_ref,
                   cckv_ref, ckr_ref, cdk_ref, cdv_ref, cnk_ref, cnv_ref,
                   y_ref,
                   qcat_s, ckv_s, kcat_s, vmla_s, dq_s, dk_s, dv_s, nq_s, nk_s, nv_s, o_s, bias_ref, *, lam_init):
    S = LAT_SEQ
    C = CTX_SEQ
    mod = mod_ref[...]
    lam = _lambda(lamv_ref[...], lam_init)
    hmasks = _head_masks()
    mla_qm = _mla_qmasks()

    @pl.when(pl.program_id(0) == 0)
    def _():
        qc = lax.broadcasted_iota(jnp.int32, (GRID_W, LANES), 0)
        kc = lax.broadcasted_iota(jnp.int32, (GRID_W, LANES), 1) % GRID_W
        c_start = jnp.clip(qc - NAT_WIN_COLS // 2, 0, GRID_W - NAT_WIN_COLS)
        in_win = (kc >= c_start) & (kc < c_start + NAT_WIN_COLS)
        for par in range(2):
            for h in range(NAT_HEADS):
                for p in range(NAT_WIN_ROWS):
                    row = jnp.broadcast_to(rpb_ref[par, h, p:p + 1, :], (GRID_W, LANES))
                    tile = pltpu.roll(row, LANES - (NAT_WIN_COLS - 1), 1, stride=1, stride_axis=0)
                    bias_ref[par, h // 4, (h % 4) * GRID_W:(h % 4 + 1) * GRID_W, p * LANES:(p + 1) * LANES] = (
                        jnp.where(in_win, tile, NEG_INF))

    ckv_s[0:C, :] = cckv_ref[...]
    kcat_s[0:C, HEAD_BLOCK:] = ckr_ref[...]
    dk_s[0:C, :] = cdk_ref[...]
    dv_s[0:C, :] = cdv_ref[...]

    def proj_body(i, carry):
        r = pl.multiple_of(i * PROJ_ROWS, PROJ_ROWS)
        rows = pl.ds(r, PROJ_ROWS)
        krows = pl.ds(C + r, PROJ_ROWS)
        cos, s_next, s_prev = cos_ref[rows, :], snext_ref[rows, :], sprev_ref[rows, :]
        rope = lambda v: _rope(v, cos, s_next, s_prev)
        x = x_ref[rows, :]
        h = (_rmsnorm(x, g_ref[...]) * (1.0 + mod[1:2]) + mod[0:1]).astype(BF16)
        proj = _dot(h, win_ref[...])
        q_cat = _dot(_rmsnorm(proj[:, P_CQ:P_CQ + MLA_Q_LORA], qn_ref[...]).astype(BF16), wuq_ref[...])
        qcat_s[rows, 0:HEAD_BLOCK] = (q_cat[:, 0:HEAD_BLOCK] * MLA_SCALE).astype(BF16)
        qcat_s[rows, HEAD_BLOCK:] = (rope(q_cat[:, HEAD_BLOCK:]) * MLA_SCALE).astype(BF16)
        ckv_s[krows, :] = _rmsnorm(proj[:, P_CKV:P_CKV + MLA_KV_LORA], kvn_ref[...]).astype(BF16)
        kcat_s[krows, HEAD_BLOCK:] = _tile32(rope(proj[:, P_KR:P_KR + LANES])).astype(BF16)
        for b in range(2):
            c0 = b * LANES
            dq_s[rows, c0:c0 + LANES] = (rope(proj[:, P_DQ + c0:P_DQ + c0 + LANES]) * DIFF_SCALE).astype(BF16)
            dk_s[krows, c0:c0 + LANES] = rope(proj[:, P_DK + c0:P_DK + c0 + LANES]).astype(BF16)
        dv_s[krows, :] = proj[:, P_DV:P_DV + HEAD_BLOCK].astype(BF16)
        nq_s[rows, :] = (proj[:, P_NQ:P_NQ + 2 * HEAD_BLOCK] * NAT_SCALE).astype(BF16)
        nk_s[rows, :] = proj[:, P_NK:P_NK + 2 * HEAD_BLOCK].astype(BF16)
        nv_s[rows, :] = proj[:, P_NV:P_NV + 2 * HEAD_BLOCK].astype(BF16)
        return carry

    lax.fori_loop(0, S // PROJ_ROWS, proj_body, 0)

    def kv_body(i, carry):
        rows = pl.ds(pl.multiple_of(i * PROJ_ROWS, PROJ_ROWS), PROJ_ROWS)
        kv = _dot(ckv_s[rows, :], wukv_ref[...])
        kcat_s[rows, 0:HEAD_BLOCK] = kv[:, 0:HEAD_BLOCK].astype(BF16)
        vmla_s[rows, :] = kv[:, HEAD_BLOCK:].astype(BF16)
        return carry

    lax.fori_loop(0, N_KEYS // PROJ_ROWS, kv_body, 0)

    def mla_body(i, carry):
        rows = pl.ds(pl.multiple_of(i * MLA_QROWS, MLA_QROWS), MLA_QROWS)
        o_mla = _softmax_heads(qcat_s[rows, :], kcat_s[...], vmla_s[...], mla_qm, hmasks)
        o_s[rows, 0:HEAD_BLOCK] = o_mla.astype(BF16)
        return carry

    lax.fori_loop(0, S // MLA_QROWS, mla_body, 0)

    def diff_body(i, carry):
        rows = pl.ds(pl.multiple_of(i * DIFF_QROWS, DIFF_QROWS), DIFF_QROWS)
        o_diff = _diff_heads(dq_s[rows, :], dk_s[...], dv_s[...], lam, subln_ref[...], lam_init)
        o_s[rows, HEAD_BLOCK:2 * HEAD_BLOCK] = o_diff.astype(BF16)
        return carry

    lax.fori_loop(0, S // DIFF_QROWS, diff_body, 0)

    for r in range(LAT_ROWS):
        rs = min(max(r - NAT_WIN_ROWS // 2, 0), LAT_ROWS - NAT_WIN_ROWS)
        dr0 = rs - r + (NAT_WIN_ROWS - 1)
        par = dr0 % 2
        off = GRID_W * (dr0 - par)
        q0, q1 = r * GRID_W, (r + 1) * GRID_W
        k0, k1 = rs * GRID_W, (rs + NAT_WIN_ROWS) * GRID_W
        for b in range(2):
            c0, c1 = b * HEAD_BLOCK, (b + 1) * HEAD_BLOCK
            q = nq_s[q0:q1, c0:c1]
            zero = jnp.zeros_like(q)
            qs = jnp.concatenate([jnp.where(m, q, zero) for m in hmasks], axis=0)
            sw = _dot_t(qs, nk_s[k0:k1, c0:c1]) + bias_ref[par, b, :, off:off + NAT_WIN_ROWS * GRID_W]
            sc = _dot_t(qs, cnk_ref[:, c0:c1])
            m = jnp.maximum(jnp.max(sw, axis=-1, keepdims=True), jnp.max(sc, axis=-1, keepdims=True))
            ew = jnp.exp(sw - m)
            ec = jnp.exp(sc - m)
            l = jnp.sum(ew, axis=-1, keepdims=True) + jnp.sum(ec, axis=-1, keepdims=True)
            o = (_dot(ew.astype(BF16), nv_s[k0:k1, c0:c1]) + _dot(ec.astype(BF16), cnv_ref[:, c0:c1])) / l
            of = None
            for hh, hm in enumerate(hmasks):
                part = jnp.where(hm, o[hh * GRID_W:(hh + 1) * GRID_W], 0.0)
                of = part if of is None else of + part
            o_s[q0:q1, 2 * HEAD_BLOCK + c0:2 * HEAD_BLOCK + c1] = of.astype(BF16)

    def out_body(i, carry):
        rows = pl.ds(pl.multiple_of(i * PROJ_ROWS, PROJ_ROWS), PROJ_ROWS)
        y_ref[rows, :] = x_ref[rows, :] + mod[2:3] * _dot(o_s[rows, :], wout_ref[...])
        return carry

    lax.fori_loop(0, S // PROJ_ROWS, out_body, 0)


def _mix_ii_call(x, ada5, layer, mix_norm, win, qn, wuq, kvn, wukv, lamv, subln, wout, rope_tabs, rpb_rows,
                 caches, lam_init):
    T = x.shape[0]
    Bd = T // LAT_SEQ
    cckv, ckr, cdk, cdv, cnk, cnv = caches
    lyr = lambda shape: _resident((None,) + shape, lambda i: (layer,) + (0,) * len(shape))
    cache = lambda width: pl.BlockSpec((None, None, CTX_SEQ, width), lambda i: (i, layer, 0, 0))
    tab = _resident((LAT_SEQ, LANES), lambda i: (0, 0))
    in_specs = [
        pl.BlockSpec((LAT_SEQ, D_MODEL), lambda i: (i, 0), pipeline_mode=pl.Buffered(1)),
        pl.BlockSpec((None, None, None, 3, D_MODEL), lambda i: (layer, i + 1, 1, 0, 0)),
        lyr((1, D_MODEL)),
        lyr((D_MODEL, P_COLS)),
        lyr((1, MLA_Q_LORA)),
        lyr((MLA_Q_LORA, HEAD_BLOCK + LANES)),
        lyr((1, MLA_KV_LORA)),
        lyr((MLA_KV_LORA, 2 * HEAD_BLOCK)),
        lyr((4, DIFF_DK)),
        lyr((1, HEAD_BLOCK)),
        lyr((D_MODEL, D_MODEL)),
        tab, tab, tab,
        lyr((2, NAT_HEADS, NAT_WIN_ROWS, LANES)),
        cache(MLA_KV_LORA), cache(LANES), cache(HEAD_BLOCK), cache(HEAD_BLOCK),
        cache(2 * HEAD_BLOCK), cache(2 * HEAD_BLOCK),
    ]
    scratch = [
        pltpu.VMEM((LAT_SEQ, HEAD_BLOCK + LANES), BF16),
        pltpu.VMEM((N_KEYS, MLA_KV_LORA), BF16),
        pltpu.VMEM((N_KEYS, HEAD_BLOCK + LANES), BF16),
        pltpu.VMEM((N_KEYS, HEAD_BLOCK), BF16),
        pltpu.VMEM((LAT_SEQ, HEAD_BLOCK), BF16),
        pltpu.VMEM((N_KEYS, HEAD_BLOCK), BF16),
        pltpu.VMEM((N_KEYS, HEAD_BLOCK), BF16),
        pltpu.VMEM((LAT_SEQ, 2 * HEAD_BLOCK), BF16),
        pltpu.VMEM((LAT_SEQ, 2 * HEAD_BLOCK), BF16),
        pltpu.VMEM((LAT_SEQ, 2 * HEAD_BLOCK), BF16),
        pltpu.VMEM((LAT_SEQ, D_MODEL), BF16),
        pltpu.VMEM((2, 2, HEAD_BLOCK, BIAS_LANES), F32),
    ]
    return pl.pallas_call(
        functools.partial(_mix_ii_kernel, lam_init=lam_init),
        grid=(Bd,),
        in_specs=in_specs,
        out_specs=pl.BlockSpec((LAT_SEQ, D_MODEL), lambda i: (i, 0)),
        out_shape=jax.ShapeDtypeStruct((T, D_MODEL), F32),
        scratch_shapes=scratch,
        compiler_params=pltpu.CompilerParams(
            dimension_semantics=("arbitrary",), vmem_limit_bytes=VMEM_LIMIT_BYTES),
        name="mix_lat",
    )(x, ada5, mix_norm, win, qn, wuq, kvn, wukv, lamv, subln, wout, *rope_tabs, rpb_rows, *caches)


def _rpb_rows(rpb):
    L = rpb.shape[0]
    n_dr, n_dc = 2 * NAT_WIN_ROWS - 1, 2 * NAT_WIN_COLS - 1
    padded = jnp.pad(rpb.astype(F32), ((0, 0), (0, 0), (0, 2 * NAT_WIN_ROWS + 1 - n_dr), (0, GRID_W - n_dc)))
    even = padded[:, :, 0:2 * NAT_WIN_ROWS].reshape(L, NAT_HEADS, NAT_WIN_ROWS, LANES)
    odd = padded[:, :, 1:2 * NAT_WIN_ROWS + 1].reshape(L, NAT_HEADS, NAT_WIN_ROWS, LANES)
    return jnp.stack([even, odd], axis=1)


def _heads_to_lanes(cache):
    B, L, H, S, d = cache.shape
    return jnp.transpose(cache, (0, 1, 3, 2, 4)).reshape(B, L, S, H * d).astype(BF16)


def kernel(x_prompt, x_sample, cache_mla_ckv, cache_mla_krope, cache_diff_k, cache_diff_v, cache_nat_k, cache_nat_v, c, c_ctx, w_ada, b_ada, ffn1_norm, ffn1_w_gate, ffn1_w_up, ffn1_w_down, mix_norm, w_in, mla_q_norm, mla_w_uq, mla_kv_norm, mla_w_ukv, diff_lambda_q1, diff_lambda_k1, diff_lambda_q2, diff_lambda_k2, diff_subln, nat_rpb, w_out, ffn2_norm, ffn2_w_gate, ffn2_w_up, ffn2_w_down, final_norm):
    L = w_ada.shape[0]
    B, S_ctx, _ = x_prompt.shape
    Bd, S_lat, _ = x_sample.shape
    assert S_ctx == CTX_SEQ and S_lat == LAT_SEQ and 1 + Bd <= ADA_ROWS
    assert cache_mla_ckv.shape[2] == CTX_SEQ

    cvec = jnp.concatenate([c_ctx[None, :], c, jnp.zeros((ADA_ROWS - 1 - Bd, D_MODEL), F32)], axis=0)
    ada5 = _ada_call(cvec, w_ada, b_ada).reshape(L, ADA_ROWS, 3, 3, D_MODEL)

    bf = lambda w: w.astype(BF16)
    w1g, w1u, w1d = bf(ffn1_w_gate), bf(ffn1_w_up), bf(ffn1_w_down)
    w2g, w2u, w2d = bf(ffn2_w_gate), bf(ffn2_w_up), bf(ffn2_w_down)
    split = MLA_Q_LORA + MLA_KV_LORA + MLA_ROPE
    win = bf(jnp.concatenate(
        [w_in[:, :, :split], jnp.zeros((L, D_MODEL, LANES - MLA_ROPE), F32), w_in[:, :, split:]], axis=2))
    wuq4 = mla_w_uq.reshape(L, MLA_Q_LORA, MLA_HEADS, MLA_NOPE + MLA_ROPE)
    wuq = bf(jnp.concatenate([wuq4[..., :MLA_NOPE].reshape(L, MLA_Q_LORA, -1),
                              wuq4[..., MLA_NOPE:].reshape(L, MLA_Q_LORA, -1)], axis=2))
    wukv4 = mla_w_ukv.reshape(L, MLA_KV_LORA, MLA_HEADS, MLA_NOPE + MLA_V)
    wukv = bf(jnp.concatenate([wukv4[..., :MLA_NOPE].reshape(L, MLA_KV_LORA, -1),
                               wukv4[..., MLA_NOPE:].reshape(L, MLA_KV_LORA, -1)], axis=2))
    wout = bf(w_out)
    lamv = jnp.stack([diff_lambda_q1, diff_lambda_k1, diff_lambda_q2, diff_lambda_k2], axis=1)
    subln = jnp.tile(diff_subln, (1, DIFF_HEADS)).reshape(L, 1, HEAD_BLOCK)
    r3 = lambda a: a.reshape(L, 1, a.shape[-1])
    n1, n2, nm, qn, kvn = r3(ffn1_norm), r3(ffn2_norm), r3(mix_norm), r3(mla_q_norm), r3(mla_kv_norm)
    fnorm = final_norm.reshape(1, D_MODEL)

    caches = (bf(cache_mla_ckv), bf(jnp.tile(cache_mla_krope, (1, 1, 1, LANES // MLA_ROPE))),
              _heads_to_lanes(cache_diff_k), _heads_to_lanes(cache_diff_v),
              _heads_to_lanes(cache_nat_k), _heads_to_lanes(cache_nat_v))
    rope_tabs = (jnp.asarray(_ROPE_COS), jnp.asarray(_ROPE_SNEXT), jnp.asarray(_ROPE_SPREV))
    rpb_rows = _rpb_rows(nat_rpb)

    states = (jnp.zeros((B, L, CTX_SEQ, MLA_KV_LORA), F32), jnp.zeros((B, L, CTX_SEQ, MLA_ROPE), F32),
              jnp.zeros((B, L, DIFF_HEADS, CTX_SEQ, 2 * DIFF_DK), F32), jnp.zeros((B, L, DIFF_HEADS, CTX_SEQ, DIFF_DV), F32),
              jnp.zeros((B, L, NAT_HEADS, CTX_SEQ, NAT_HD), F32), jnp.zeros((B, L, NAT_HEADS, CTX_SEQ, NAT_HD), F32))

    xi = x_prompt.reshape(B * CTX_SEQ, D_MODEL)
    xs = x_sample.reshape(Bd * LAT_SEQ, D_MODEL)
    tm = 512
    nb = 2 if B % 2 == 0 else 1
    ctx_row = lambda t: 0
    lat_row = lambda t: 1 + t // (LAT_SEQ // tm)
    for l in range(L):
        lam_init = 0.8 - 0.6 * math.exp(-0.3 * l)
        last = l == L - 1
        xi = _ffn_call(xi, ada5, l, 0, ctx_row, n1, w1g, w1u, w1d, None, tm, "ffn1_ctx")
        xs = _ffn_call(xs, ada5, l, 0, lat_row, n1, w1g, w1u, w1d, None, tm, "ffn1_lat")
        xi, states = _mix_i_call(xi, ada5, l, nm, win, qn, wuq, kvn, wukv, lamv, subln, wout, states, nb, lam_init)
        xs = _mix_ii_call(xs, ada5, l, nm, win, qn, wuq, kvn, wukv, lamv, subln, wout, rope_tabs,
                          rpb_rows, caches, lam_init)
        xi = _ffn_call(xi, ada5, l, 2, ctx_row, n2, w2g, w2u, w2d, fnorm if last else None, tm, "ffn2_ctx")
        xs = _ffn_call(xs, ada5, l, 2, lat_row, n2, w2g, w2u, w2d, fnorm if last else None, tm, "ffn2_lat")

    return (xi.reshape(B, CTX_SEQ, D_MODEL), xs.reshape(Bd, LAT_SEQ, D_MODEL)) + tuple(states)
```

```python
import functools
import math

import numpy as np
import jax
import jax.numpy as jnp
from jax import lax
from jax.experimental import pallas as pl
from jax.experimental.pallas import tpu as pltpu

F32 = jnp.float32
BF16 = jnp.bfloat16

D_MODEL = 1024
FFN_DIM = 2816
NORM_EPS = 1e-6
ROPE_THETA = 10000.0
GRID_W = 64
N_ADA = 9
NEG_INF = -1e30

MLA_HEADS = 4
MLA_Q_LORA = 256
MLA_KV_LORA = 128
MLA_NOPE = 64
MLA_ROPE = 32
MLA_V = 64
DIFF_HEADS = 4
DIFF_DK = 32
DIFF_DV = 64
NAT_HEADS = 8
NAT_HD = 64
NAT_WIN_ROWS = 8
NAT_WIN_COLS = 16

LOG2E = math.log2(math.e)
MLA_SCALE = (MLA_NOPE + MLA_ROPE) ** -0.5 * LOG2E
DIFF_SCALE = DIFF_DK ** -0.5 * LOG2E
NAT_SCALE = NAT_HD ** -0.5 * LOG2E

CTX_SEQ = 256
LAT_SEQ = 1024
LAT_ROWS = LAT_SEQ // GRID_W

LANES = 128
MXU_DIM = 256
VMEM_LIMIT_BYTES = 58 * 1024 * 1024

P_CQ = 0
P_CKV = 256
P_KR = 384
P_DQ = 512
P_DK = 768
P_DV = 1024
P_NQ = 1280
P_NK = 1792
P_NV = 2304
P_COLS = 2816
P_SPLIT = 512
HEAD_BLOCK = 256
ADA_ROWS = 8

FFN_CHUNKS = ((0, 768), (768, 1536), (1536, 2304), (2304, 2816))


def _rope_tables():
    t = np.arange(LAT_SEQ)
    pos = np.stack([t // GRID_W, t % GRID_W], axis=0).astype(np.float64)
    lane = np.arange(LANES)
    p = lane % 32
    axis = (p >= 16).astype(np.int64)
    freqs = ROPE_THETA ** (-(p % 8).astype(np.float64) / 8.0)
    ang = pos[axis, :].T * freqs[None, :]
    first = (p % 16) < 8
    cos = np.cos(ang)
    sin = np.sin(ang)
    s_next = np.where(first[None, :], -sin, 0.0)
    s_prev = np.where(first[None, :], 0.0, sin)
    return cos.astype(np.float32), s_next.astype(np.float32), s_prev.astype(np.float32)


_ROPE_COS, _ROPE_SNEXT, _ROPE_SPREV = _rope_tables()


def _rmsnorm(x, g):
    ms = jnp.mean(x * x, axis=-1, keepdims=True)
    return x * lax.rsqrt(ms + NORM_EPS) * g


def _silu(x):
    return x / (1.0 + jnp.exp(-x))


def _dot(a, b):
    return jnp.dot(a, b, preferred_element_type=F32)


def _dot_t(a, b):
    return lax.dot_general(a, b, (((1,), (1,)), ((), ())), preferred_element_type=F32)


class _Proj:
    def __init__(self, a, b):
        self.a, self.b = a, b

    def __getitem__(self, idx):
        lo, hi = idx[1].start, idx[1].stop
        return self.a[:, lo:hi] if hi <= P_SPLIT else self.b[:, lo - P_SPLIT:hi - P_SPLIT]


def _lane_mask(width, lo, hi):
    lane = lax.broadcasted_iota(jnp.int32, (1, width), 1)
    return (lane >= lo) & (lane < hi)


def _head_masks(width=HEAD_BLOCK, group=64, n=4):
    return [_lane_mask(width, h * group, (h + 1) * group) for h in range(n)]


def _mla_qmasks():
    lane = lax.broadcasted_iota(jnp.int32, (1, HEAD_BLOCK + LANES), 1)
    out = []
    for h in range(MLA_HEADS):
        nope = (lane >= h * MLA_NOPE) & (lane < (h + 1) * MLA_NOPE)
        rope = (lane >= HEAD_BLOCK + h * MLA_ROPE) & (lane < HEAD_BLOCK + (h + 1) * MLA_ROPE)
        out.append(nope | rope)
    return out


def _tile32(blk):
    return blk + pltpu.roll(blk, 32, 1) + pltpu.roll(blk, 64, 1) + pltpu.roll(blk, 96, 1)


def _rope(x, cos, s_next, s_prev):
    return x * cos + pltpu.roll(x, LANES - 8, 1) * s_next + pltpu.roll(x, 8, 1) * s_prev


def _stack_masked(q, masks):
    zero = jnp.zeros_like(q)
    return jnp.concatenate([jnp.where(m, q, zero) for m in masks], axis=0)


def _scores_t(q, k_bf, kmasks, stacked):
    sk = k_bf.shape[0]
    if stacked:
        s = _dot_t(_stack_masked(k_bf, kmasks), q)
        return [s[h * sk:(h + 1) * sk] for h in range(len(kmasks))]
    zero = jnp.zeros_like(k_bf)
    return [_dot_t(jnp.where(m, k_bf, zero), q) for m in kmasks]


def _pv_t(vt_bf, p_bf, h, short_keys):
    if short_keys:
        return _dot(vt_bf, p_bf)[h * 64:(h + 1) * 64]
    return _dot(vt_bf[h * 64:(h + 1) * 64], p_bf)


def _exp_sum_t(s):
    e = jnp.exp2(s - jnp.max(s, axis=0, keepdims=True))
    return e, jnp.sum(e, axis=0, keepdims=True)


def _softmax_heads_t(q, k_bf, vt_bf, kmasks, stacked):
    parts = []
    for h, s in enumerate(_scores_t(q, k_bf, kmasks, stacked)):
        e, l = _exp_sum_t(s)
        parts.append(_pv_t(vt_bf, e.astype(BF16), h, stacked) / l)
    return jnp.concatenate(parts, axis=0)


def _diff_heads_t(q, k_bf, vt_bf, lam, subln_col, lam_init, stacked):
    n = DIFF_HEADS
    m1 = [_lane_mask(HEAD_BLOCK, h * 2 * DIFF_DK, h * 2 * DIFF_DK + DIFF_DK) for h in range(n)]
    m2 = [_lane_mask(HEAD_BLOCK, h * 2 * DIFF_DK + DIFF_DK, (h + 1) * 2 * DIFF_DK) for h in range(n)]
    s = _scores_t(q, k_bf, m1 + m2, stacked)
    parts = []
    for h in range(n):
        e1, l1 = _exp_sum_t(s[h])
        e2, l2 = _exp_sum_t(s[n + h])
        p = e1 * (1.0 / l1) - e2 * (lam / l2)
        oh = _pv_t(vt_bf, p.astype(BF16), h, stacked)
        ms = jnp.mean(oh * oh, axis=0, keepdims=True)
        parts.append(oh * lax.rsqrt(ms + NORM_EPS))
    return jnp.concatenate(parts, axis=0) * subln_col * (1.0 - lam_init)


def _lambda(lamv, lam_init):
    a = jnp.sum(lamv[0:1] * lamv[1:2], axis=-1, keepdims=True)
    b = jnp.sum(lamv[2:3] * lamv[3:4], axis=-1, keepdims=True)
    return jnp.exp(a) - jnp.exp(b) + lam_init


def _ada_kernel(c_ref, w_ref, b_ref, o_ref):
    s = _silu(c_ref[...]).astype(BF16)
    o_ref[...] = _dot(s, w_ref[...].astype(BF16)) + b_ref[...]


def _ada_call(cvec, w_ada, b_ada):
    L, _, n = w_ada.shape
    tn = 1024
    return pl.pallas_call(
        _ada_kernel,
        grid=(L, n // tn),
        in_specs=[
            pl.BlockSpec((ADA_ROWS, D_MODEL), lambda l, j: (0, 0)),
            pl.BlockSpec((None, D_MODEL, tn), lambda l, j: (l, 0, j)),
            pl.BlockSpec((None, 1, tn), lambda l, j: (l, 0, j)),
        ],
        out_specs=pl.BlockSpec((None, ADA_ROWS, tn), lambda l, j: (l, 0, j)),
        out_shape=jax.ShapeDtypeStruct((L, ADA_ROWS, n), F32),
        compiler_params=pltpu.CompilerParams(
            dimension_semantics=("arbitrary", "arbitrary"), vmem_limit_bytes=VMEM_LIMIT_BYTES),
        name="ada",
    )(cvec, w_ada, b_ada.reshape(L, 1, n))


def _ffn_kernel(x_ref, mod_ref, g_ref, wg_ref, wu_ref, wd_ref, *rest, final):
    if final:
        fn_ref, o_ref = rest
    else:
        (o_ref,) = rest
    x = x_ref[...]
    mod = mod_ref[...]
    h = (_rmsnorm(x, g_ref[...]) * (1.0 + mod[1:2]) + mod[0:1]).astype(BF16)
    acc = None
    for lo, hi in FFN_CHUNKS:
        g = _dot(h, wg_ref[:, lo:hi])
        u = _dot(h, wu_ref[:, lo:hi])
        part = _dot((_silu(g) * u).astype(BF16), wd_ref[lo:hi, :])
        acc = part if acc is None else acc + part
    y = x + (0.5 * mod[2:3]) * acc
    if final:
        y = _rmsnorm(y, fn_ref[...])
    o_ref[...] = y


def _resident(shape, index_map):
    return pl.BlockSpec(shape, index_map, pipeline_mode=pl.Buffered(1))


def _ffn_call(x, ada5, layer, group, row_of_tile, norm, wg, wu, wd, final_norm, tm, name):
    T = x.shape[0]
    in_specs = [
        pl.BlockSpec((tm, D_MODEL), lambda t: (t, 0)),
        pl.BlockSpec((None, None, None, 3, D_MODEL), lambda t: (layer, row_of_tile(t), group, 0, 0)),
        _resident((None, 1, D_MODEL), lambda t: (layer, 0, 0)),
        _resident((None, D_MODEL, FFN_DIM), lambda t: (layer, 0, 0)),
        _resident((None, D_MODEL, FFN_DIM), lambda t: (layer, 0, 0)),
        _resident((None, FFN_DIM, D_MODEL), lambda t: (layer, 0, 0)),
    ]
    args = [x, ada5, norm, wg, wu, wd]
    if final_norm is not None:
        in_specs.append(_resident((1, D_MODEL), lambda t: (0, 0)))
        args.append(final_norm)
    return pl.pallas_call(
        functools.partial(_ffn_kernel, final=final_norm is not None),
        grid=(T // tm,),
        in_specs=in_specs,
        out_specs=pl.BlockSpec((tm, D_MODEL), lambda t: (t, 0)),
        out_shape=jax.ShapeDtypeStruct((T, D_MODEL), F32),
        compiler_params=pltpu.CompilerParams(
            dimension_semantics=("arbitrary",), vmem_limit_bytes=VMEM_LIMIT_BYTES),
        name=name,
    )(*args)


def _mix_i_kernel(x_ref, mod_ref, g_ref, wina_ref, winb_ref, qn_ref, wuq_ref, kvn_ref, wukv_ref, lamv_ref, subln_ref,
                  wout_ref, ckv_any, kr_any, dk_any, dv_any, nk_any, nv_any,
                  y_ref, ckv_ref, kr_ref, dk_ref, dv_ref, nk_ref, nv_ref, *, nb, lam_init):
    del ckv_any, kr_any, dk_any, dv_any, nk_any, nv_any
    S = CTX_SEQ
    x = x_ref[...]
    mod = mod_ref[...]
    h = (_rmsnorm(x, g_ref[...]) * (1.0 + mod[1:2]) + mod[0:1]).astype(BF16)
    proj = _Proj(_dot(h, wina_ref[...]), _dot(h, winb_ref[...]))
    q_cat = _dot(_rmsnorm(proj[:, P_CQ:P_CQ + MLA_Q_LORA], qn_ref[...]).astype(BF16), wuq_ref[...])
    ckv = _rmsnorm(proj[:, P_CKV:P_CKV + MLA_KV_LORA], kvn_ref[...])
    kv = _dot(ckv.astype(BF16), wukv_ref[...])
    kr_blk = proj[:, P_KR:P_KR + LANES]
    k_cat = jnp.concatenate([kv[:, 0:HEAD_BLOCK], _tile32(kr_blk)], axis=1).astype(BF16)
    v_mla = kv[:, HEAD_BLOCK:2 * HEAD_BLOCK]
    q_cat = (q_cat * MLA_SCALE).astype(BF16)
    dq = (proj[:, P_DQ:P_DQ + HEAD_BLOCK] * DIFF_SCALE).astype(BF16)
    dk = proj[:, P_DK:P_DK + HEAD_BLOCK]
    dv = proj[:, P_DV:P_DV + HEAD_BLOCK]
    nq = (proj[:, P_NQ:P_NQ + 2 * HEAD_BLOCK] * NAT_SCALE).astype(BF16)
    nk = proj[:, P_NK:P_NK + 2 * HEAD_BLOCK]
    nv = proj[:, P_NV:P_NV + 2 * HEAD_BLOCK]
    dk_bf, nk_bf = dk.astype(BF16), nk.astype(BF16)
    lam = _lambda(lamv_ref[...], lam_init)
    hmasks = _head_masks()
    mla_qm = _mla_qmasks()

    outs = []
    for j in range(nb):
        r0, r1 = j * S, (j + 1) * S
        ckv_ref[j] = ckv[r0:r1]
        kr_ref[j] = kr_blk[r0:r1, 0:MLA_ROPE]
        for hh in range(DIFF_HEADS):
            dk_ref[j, hh] = dk[r0:r1, hh * 64:(hh + 1) * 64]
            dv_ref[j, hh] = dv[r0:r1, hh * 64:(hh + 1) * 64]
        for hh in range(NAT_HEADS):
            nk_ref[j, hh] = nk[r0:r1, hh * 64:(hh + 1) * 64]
            nv_ref[j, hh] = nv[r0:r1, hh * 64:(hh + 1) * 64]
        tr = lambda v: v.T.astype(BF16)
        ot_mla = _softmax_heads_t(q_cat[r0:r1], k_cat[r0:r1], tr(v_mla[r0:r1]), mla_qm, True)
        ot_diff = _diff_heads_t(dq[r0:r1], dk_bf[r0:r1], tr(dv[r0:r1]), lam, subln_ref[...], lam_init, True)
        ot_nat = [
            _softmax_heads_t(nq[r0:r1, b * HEAD_BLOCK:(b + 1) * HEAD_BLOCK], nk_bf[r0:r1, b * HEAD_BLOCK:(b + 1) * HEAD_BLOCK],
                             tr(nv[r0:r1, b * HEAD_BLOCK:(b + 1) * HEAD_BLOCK]), hmasks, True)
            for b in range(2)
        ]
        outs.append(jnp.concatenate([ot_mla, ot_diff] + ot_nat, axis=0).T.astype(BF16))
    o = jnp.concatenate(outs, axis=0) if nb > 1 else outs[0]
    y_ref[...] = x + mod[2:3] * _dot(o, wout_ref[...])


def _mix_i_call(x, ada5, layer, mix_norm, win, qn, wuq, kvn, wukv, lamv, subln, wout, states, nb, lam_init):
    T = x.shape[0]
    B = T // CTX_SEQ
    tm = nb * CTX_SEQ
    st_ckv, st_kr, st_dk, st_dv, st_nk, st_nv = states
    L = st_ckv.shape[1]
    lyr = lambda shape: _resident((None,) + shape, lambda i: (layer,) + (0,) * len(shape))
    any_spec = pl.BlockSpec(memory_space=pl.ANY)
    in_specs = [
        pl.BlockSpec((tm, D_MODEL), lambda i: (i, 0)),
        _resident((None, None, None, 3, D_MODEL), lambda i: (layer, 0, 1, 0, 0)),
        lyr((1, D_MODEL)),
        lyr((D_MODEL, P_SPLIT)),
        lyr((D_MODEL, P_COLS - P_SPLIT)),
        lyr((1, MLA_Q_LORA)),
        lyr((MLA_Q_LORA, HEAD_BLOCK + LANES)),
        lyr((1, MLA_KV_LORA)),
        lyr((MLA_KV_LORA, 2 * HEAD_BLOCK)),
        lyr((4, DIFF_DK)),
        lyr((HEAD_BLOCK, 1)),
        lyr((D_MODEL, D_MODEL)),
    ] + [any_spec] * 6
    out_specs = [
        pl.BlockSpec((tm, D_MODEL), lambda i: (i, 0)),
        pl.BlockSpec((nb, None, CTX_SEQ, MLA_KV_LORA), lambda i: (i, layer, 0, 0)),
        pl.BlockSpec((nb, None, CTX_SEQ, MLA_ROPE), lambda i: (i, layer, 0, 0)),
        pl.BlockSpec((nb, None, DIFF_HEADS, CTX_SEQ, 2 * DIFF_DK), lambda i: (i, layer, 0, 0, 0)),
        pl.BlockSpec((nb, None, DIFF_HEADS, CTX_SEQ, DIFF_DV), lambda i: (i, layer, 0, 0, 0)),
        pl.BlockSpec((nb, None, NAT_HEADS, CTX_SEQ, NAT_HD), lambda i: (i, layer, 0, 0, 0)),
        pl.BlockSpec((nb, None, NAT_HEADS, CTX_SEQ, NAT_HD), lambda i: (i, layer, 0, 0, 0)),
    ]
    out_shape = [jax.ShapeDtypeStruct((T, D_MODEL), F32)] + [jax.ShapeDtypeStruct(s.shape, s.dtype) for s in states]
    res = pl.pallas_call(
        functools.partial(_mix_i_kernel, nb=nb, lam_init=lam_init),
        grid=(B // nb,),
        in_specs=in_specs,
        out_specs=out_specs,
        out_shape=out_shape,
        input_output_aliases={12 + k: 1 + k for k in range(6)},
        compiler_params=pltpu.CompilerParams(
            dimension_semantics=("arbitrary",), vmem_limit_bytes=VMEM_LIMIT_BYTES),
        name="mix_ctx",
    )(x, ada5, mix_norm, *win, qn, wuq, kvn, wukv, lamv, subln, wout, *states)
    return res[0], tuple(res[1:])


PROJ_ROWS = 256
DENSE_QROWS = MXU_DIM
N_KEYS = CTX_SEQ + LAT_SEQ
BIAS_LANES = NAT_WIN_ROWS * LANES


def _mix_ii_kernel(x_ref, mod_ref, g_ref, wina_ref, winb_ref, qn_ref, wuq_ref, kvn_ref, wukv_ref, lamv_ref, subln_ref,
                   wout_ref, cos_ref, snext_ref, sprev_ref, rpb_ref,
                   cckv_ref, ckr_ref, cdk_ref, cdv_ref, cnk_ref, cnv_ref,
                   y_ref,
                   qcat_s, ckv_s, kcat_s, vmla_s, dq_s, dk_s, dv_s, nq_s, nk_s, nv_s, o_s, bias_ref,
                   vmlat_s, dvt_s, *, lam_init):
    S = LAT_SEQ
    C = CTX_SEQ
    mod = mod_ref[...]
    lam = _lambda(lamv_ref[...], lam_init)
    hmasks = _head_masks()
    mla_qm = _mla_qmasks()

    @pl.when(pl.program_id(0) == 0)
    def _():
        qc = lax.broadcasted_iota(jnp.int32, (GRID_W, LANES), 0)
        kc = lax.broadcasted_iota(jnp.int32, (GRID_W, LANES), 1) % GRID_W
        c_start = jnp.clip(qc - NAT_WIN_COLS // 2, 0, GRID_W - NAT_WIN_COLS)
        in_win = (kc >= c_start) & (kc < c_start + NAT_WIN_COLS)
        for par in range(2):
            for h in range(NAT_HEADS):
                for p in range(NAT_WIN_ROWS):
                    row = jnp.broadcast_to(rpb_ref[par, h, p:p + 1, :], (GRID_W, LANES))
                    tile = pltpu.roll(row, LANES - (NAT_WIN_COLS - 1), 1, stride=1, stride_axis=0)
                    bias_ref[par, h // 4, (h % 4) * GRID_W:(h % 4 + 1) * GRID_W, p * LANES:(p + 1) * LANES] = (
                        jnp.where(in_win, tile * LOG2E, NEG_INF))

    ckv_s[0:C, :] = cckv_ref[...]
    kcat_s[0:C, HEAD_BLOCK:] = ckr_ref[...]
    dk_s[0:C, :] = cdk_ref[...]
    dv_s[0:C, :] = cdv_ref[...]

    def proj_body(i, carry):
        r = pl.multiple_of(i * PROJ_ROWS, PROJ_ROWS)
        rows = pl.ds(r, PROJ_ROWS)
        krows = pl.ds(C + r, PROJ_ROWS)
        cos, s_next, s_prev = cos_ref[rows, :], snext_ref[rows, :], sprev_ref[rows, :]
        rope = lambda v: _rope(v, cos, s_next, s_prev)
        x = x_ref[rows, :]
        h = (_rmsnorm(x, g_ref[...]) * (1.0 + mod[1:2]) + mod[0:1]).astype(BF16)
        proj = _Proj(_dot(h, wina_ref[...]), _dot(h, winb_ref[...]))
        q_cat = _dot(_rmsnorm(proj[:, P_CQ:P_CQ + MLA_Q_LORA], qn_ref[...]).astype(BF16), wuq_ref[...])
        qcat_s[rows, 0:HEAD_BLOCK] = (q_cat[:, 0:HEAD_BLOCK] * MLA_SCALE).astype(BF16)
        qcat_s[rows, HEAD_BLOCK:] = (rope(q_cat[:, HEAD_BLOCK:]) * MLA_SCALE).astype(BF16)
        ckv_s[krows, :] = _rmsnorm(proj[:, P_CKV:P_CKV + MLA_KV_LORA], kvn_ref[...]).astype(BF16)
        kcat_s[krows, HEAD_BLOCK:] = _tile32(rope(proj[:, P_KR:P_KR + LANES])).astype(BF16)
        for b in range(2):
            c0 = b * LANES
            dq_s[rows, c0:c0 + LANES] = (rope(proj[:, P_DQ + c0:P_DQ + c0 + LANES]) * DIFF_SCALE).astype(BF16)
            dk_s[krows, c0:c0 + LANES] = rope(proj[:, P_DK + c0:P_DK + c0 + LANES]).astype(BF16)
        dv_s[krows, :] = proj[:, P_DV:P_DV + HEAD_BLOCK].astype(BF16)
        nq_s[rows, :] = (proj[:, P_NQ:P_NQ + 2 * HEAD_BLOCK] * NAT_SCALE).astype(BF16)
        nk_s[rows, :] = proj[:, P_NK:P_NK + 2 * HEAD_BLOCK].astype(BF16)
        nv_s[rows, :] = proj[:, P_NV:P_NV + 2 * HEAD_BLOCK].astype(BF16)
        return carry

    lax.fori_loop(0, S // PROJ_ROWS, proj_body, 0)

    def kv_body(i, carry):
        rows = pl.ds(pl.multiple_of(i * PROJ_ROWS, PROJ_ROWS), PROJ_ROWS)
        kv = _dot(ckv_s[rows, :], wukv_ref[...])
        kcat_s[rows, 0:HEAD_BLOCK] = kv[:, 0:HEAD_BLOCK].astype(BF16)
        vmla_s[rows, :] = kv[:, HEAD_BLOCK:].astype(BF16)
        return carry

    lax.fori_loop(0, N_KEYS // PROJ_ROWS, kv_body, 0)

    for j in range(N_KEYS // PROJ_ROWS):
        c0, c1 = j * PROJ_ROWS, (j + 1) * PROJ_ROWS
        vmlat_s[:, c0:c1] = vmla_s[c0:c1, :].astype(F32).T.astype(BF16)
        dvt_s[:, c0:c1] = dv_s[c0:c1, :].astype(F32).T.astype(BF16)

    def dense_body(i, carry):
        rows = pl.ds(pl.multiple_of(i * DENSE_QROWS, DENSE_QROWS), DENSE_QROWS)
        ot_mla = _softmax_heads_t(qcat_s[rows, :], kcat_s[...], vmlat_s[...], mla_qm, False)
        o_s[rows, 0:HEAD_BLOCK] = ot_mla.T.astype(BF16)
        ot_diff = _diff_heads_t(dq_s[rows, :], dk_s[...], dvt_s[...], lam, subln_ref[...], lam_init, False)
        o_s[rows, HEAD_BLOCK:2 * HEAD_BLOCK] = ot_diff.T.astype(BF16)
        return carry

    lax.fori_loop(0, S // DENSE_QROWS, dense_body, 0)

    for r in range(LAT_ROWS):
        rs = min(max(r - NAT_WIN_ROWS // 2, 0), LAT_ROWS - NAT_WIN_ROWS)
        dr0 = rs - r + (NAT_WIN_ROWS - 1)
        par = dr0 % 2
        off = GRID_W * (dr0 - par)
        q0, q1 = r * GRID_W, (r + 1) * GRID_W
        k0, k1 = rs * GRID_W, (rs + NAT_WIN_ROWS) * GRID_W
        for b in range(2):
            c0, c1 = b * HEAD_BLOCK, (b + 1) * HEAD_BLOCK
            q = nq_s[q0:q1, c0:c1]
            zero = jnp.zeros_like(q)
            qs = jnp.concatenate([jnp.where(m, q, zero) for m in hmasks], axis=0)
            sw = _dot_t(qs, nk_s[k0:k1, c0:c1]) + bias_ref[par, b, :, off:off + NAT_WIN_ROWS * GRID_W]
            sc = _dot_t(qs, cnk_ref[:, c0:c1])
            m = jnp.maximum(jnp.max(sw, axis=-1, keepdims=True), jnp.max(sc, axis=-1, keepdims=True))
            ew = jnp.exp2(sw - m)
            ec = jnp.exp2(sc - m)
            l = jnp.sum(ew, axis=-1, keepdims=True) + jnp.sum(ec, axis=-1, keepdims=True)
            o = (_dot(ew.astype(BF16), nv_s[k0:k1, c0:c1]) + _dot(ec.astype(BF16), cnv_ref[:, c0:c1])) / l
            of = None
            for hh, hm in enumerate(hmasks):
                part = jnp.where(hm, o[hh * GRID_W:(hh + 1) * GRID_W], 0.0)
                of = part if of is None else of + part
            o_s[q0:q1, 2 * HEAD_BLOCK + c0:2 * HEAD_BLOCK + c1] = of.astype(BF16)

    def out_body(i, carry):
        rows = pl.ds(pl.multiple_of(i * PROJ_ROWS, PROJ_ROWS), PROJ_ROWS)
        y_ref[rows, :] = x_ref[rows, :] + mod[2:3] * _dot(o_s[rows, :], wout_ref[...])
        return carry

    lax.fori_loop(0, S // PROJ_ROWS, out_body, 0)


def _mix_ii_call(x, ada5, layer, mix_norm, win, qn, wuq, kvn, wukv, lamv, subln, wout, rope_tabs, rpb_rows,
                 caches, lam_init):
    T = x.shape[0]
    Bd = T // LAT_SEQ
    cckv, ckr, cdk, cdv, cnk, cnv = caches
    lyr = lambda shape: _resident((None,) + shape, lambda i: (layer,) + (0,) * len(shape))
    cache = lambda width: pl.BlockSpec((None, None, CTX_SEQ, width), lambda i: (i, layer, 0, 0))
    tab = _resident((LAT_SEQ, LANES), lambda i: (0, 0))
    in_specs = [
        pl.BlockSpec((LAT_SEQ, D_MODEL), lambda i: (i, 0), pipeline_mode=pl.Buffered(1)),
        pl.BlockSpec((None, None, None, 3, D_MODEL), lambda i: (layer, i + 1, 1, 0, 0)),
        lyr((1, D_MODEL)),
        lyr((D_MODEL, P_SPLIT)),
        lyr((D_MODEL, P_COLS - P_SPLIT)),
        lyr((1, MLA_Q_LORA)),
        lyr((MLA_Q_LORA, HEAD_BLOCK + LANES)),
        lyr((1, MLA_KV_LORA)),
        lyr((MLA_KV_LORA, 2 * HEAD_BLOCK)),
        lyr((4, DIFF_DK)),
        lyr((HEAD_BLOCK, 1)),
        lyr((D_MODEL, D_MODEL)),
        tab, tab, tab,
        lyr((2, NAT_HEADS, NAT_WIN_ROWS, LANES)),
        cache(MLA_KV_LORA), cache(LANES), cache(HEAD_BLOCK), cache(HEAD_BLOCK),
        cache(2 * HEAD_BLOCK), cache(2 * HEAD_BLOCK),
    ]
    scratch = [
        pltpu.VMEM((LAT_SEQ, HEAD_BLOCK + LANES), BF16),
        pltpu.VMEM((N_KEYS, MLA_KV_LORA), BF16),
        pltpu.VMEM((N_KEYS, HEAD_BLOCK + LANES), BF16),
        pltpu.VMEM((N_KEYS, HEAD_BLOCK), BF16),
        pltpu.VMEM((LAT_SEQ, HEAD_BLOCK), BF16),
        pltpu.VMEM((N_KEYS, HEAD_BLOCK), BF16),
        pltpu.VMEM((N_KEYS, HEAD_BLOCK), BF16),
        pltpu.VMEM((LAT_SEQ, 2 * HEAD_BLOCK), BF16),
        pltpu.VMEM((LAT_SEQ, 2 * HEAD_BLOCK), BF16),
        pltpu.VMEM((LAT_SEQ, 2 * HEAD_BLOCK), BF16),
        pltpu.VMEM((LAT_SEQ, D_MODEL), BF16),
        pltpu.VMEM((2, 2, HEAD_BLOCK, BIAS_LANES), F32),
        pltpu.VMEM((HEAD_BLOCK, N_KEYS), BF16),
        pltpu.VMEM((HEAD_BLOCK, N_KEYS), BF16),
    ]
    return pl.pallas_call(
        functools.partial(_mix_ii_kernel, lam_init=lam_init),
        grid=(Bd,),
        in_specs=in_specs,
        out_specs=pl.BlockSpec((LAT_SEQ, D_MODEL), lambda i: (i, 0)),
        out_shape=jax.ShapeDtypeStruct((T, D_MODEL), F32),
        scratch_shapes=scratch,
        compiler_params=pltpu.CompilerParams(
            dimension_semantics=("arbitrary",), vmem_limit_bytes=VMEM_LIMIT_BYTES),
        name="mix_lat",
    )(x, ada5, mix_norm, *win, qn, wuq, kvn, wukv, lamv, subln, wout, *rope_tabs, rpb_rows, *caches)


def _rpb_rows(rpb):
    L = rpb.shape[0]
    n_dr, n_dc = 2 * NAT_WIN_ROWS - 1, 2 * NAT_WIN_COLS - 1
    padded = jnp.pad(rpb.astype(F32), ((0, 0), (0, 0), (0, 2 * NAT_WIN_ROWS + 1 - n_dr), (0, GRID_W - n_dc)))
    even = padded[:, :, 0:2 * NAT_WIN_ROWS].reshape(L, NAT_HEADS, NAT_WIN_ROWS, LANES)
    odd = padded[:, :, 1:2 * NAT_WIN_ROWS + 1].reshape(L, NAT_HEADS, NAT_WIN_ROWS, LANES)
    return jnp.stack([even, odd], axis=1)


def _heads_to_lanes(cache):
    B, L, H, S, d = cache.shape
    return jnp.transpose(cache, (0, 1, 3, 2, 4)).reshape(B, L, S, H * d).astype(BF16)


def kernel(x_prompt, x_sample, cache_mla_ckv, cache_mla_krope, cache_diff_k, cache_diff_v, cache_nat_k, cache_nat_v, c, c_ctx, w_ada, b_ada, ffn1_norm, ffn1_w_gate, ffn1_w_up, ffn1_w_down, mix_norm, w_in, mla_q_norm, mla_w_uq, mla_kv_norm, mla_w_ukv, diff_lambda_q1, diff_lambda_k1, diff_lambda_q2, diff_lambda_k2, diff_subln, nat_rpb, w_out, ffn2_norm, ffn2_w_gate, ffn2_w_up, ffn2_w_down, final_norm):
    L = w_ada.shape[0]
    B, S_ctx, _ = x_prompt.shape
    Bd, S_lat, _ = x_sample.shape
    assert S_ctx == CTX_SEQ and S_lat == LAT_SEQ and 1 + Bd <= ADA_ROWS
    assert cache_mla_ckv.shape[2] == CTX_SEQ

    cvec = jnp.concatenate([c_ctx[None, :], c, jnp.zeros((ADA_ROWS - 1 - Bd, D_MODEL), F32)], axis=0)
    ada5 = _ada_call(cvec, w_ada, b_ada).reshape(L, ADA_ROWS, 3, 3, D_MODEL)

    bf = lambda w: w.astype(BF16)
    w1g, w1u, w1d = bf(ffn1_w_gate), bf(ffn1_w_up), bf(ffn1_w_down)
    w2g, w2u, w2d = bf(ffn2_w_gate), bf(ffn2_w_up), bf(ffn2_w_down)
    split = MLA_Q_LORA + MLA_KV_LORA + MLA_ROPE
    win = (bf(jnp.concatenate([w_in[:, :, :split], jnp.zeros((L, D_MODEL, LANES - MLA_ROPE), F32)], axis=2)),
           bf(w_in[:, :, split:]))
    wuq4 = mla_w_uq.reshape(L, MLA_Q_LORA, MLA_HEADS, MLA_NOPE + MLA_ROPE)
    wuq = bf(jnp.concatenate([wuq4[..., :MLA_NOPE].reshape(L, MLA_Q_LORA, -1),
                              wuq4[..., MLA_NOPE:].reshape(L, MLA_Q_LORA, -1)], axis=2))
    wukv4 = mla_w_ukv.reshape(L, MLA_KV_LORA, MLA_HEADS, MLA_NOPE + MLA_V)
    wukv = bf(jnp.concatenate([wukv4[..., :MLA_NOPE].reshape(L, MLA_KV_LORA, -1),
                               wukv4[..., MLA_NOPE:].reshape(L, MLA_KV_LORA, -1)], axis=2))
    wout = bf(w_out)
    lamv = jnp.stack([diff_lambda_q1, diff_lambda_k1, diff_lambda_q2, diff_lambda_k2], axis=1)
    subln = jnp.tile(diff_subln, (1, DIFF_HEADS)).reshape(L, HEAD_BLOCK, 1)
    r3 = lambda a: a.reshape(L, 1, a.shape[-1])
    n1, n2, nm, qn, kvn = r3(ffn1_norm), r3(ffn2_norm), r3(mix_norm), r3(mla_q_norm), r3(mla_kv_norm)
    fnorm = final_norm.reshape(1, D_MODEL)

    caches = (bf(cache_mla_ckv), bf(jnp.tile(cache_mla_krope, (1, 1, 1, LANES // MLA_ROPE))),
              _heads_to_lanes(cache_diff_k), _heads_to_lanes(cache_diff_v),
              _heads_to_lanes(cache_nat_k), _heads_to_lanes(cache_nat_v))
    rope_tabs = (jnp.asarray(_ROPE_COS), jnp.asarray(_ROPE_SNEXT), jnp.asarray(_ROPE_SPREV))
    rpb_rows = _rpb_rows(nat_rpb)

    states = (jnp.zeros((B, L, CTX_SEQ, MLA_KV_LORA), F32), jnp.zeros((B, L, CTX_SEQ, MLA_ROPE), F32),
              jnp.zeros((B, L, DIFF_HEADS, CTX_SEQ, 2 * DIFF_DK), F32), jnp.zeros((B, L, DIFF_HEADS, CTX_SEQ, DIFF_DV), F32),
              jnp.zeros((B, L, NAT_HEADS, CTX_SEQ, NAT_HD), F32), jnp.zeros((B, L, NAT_HEADS, CTX_SEQ, NAT_HD), F32))

    xi = x_prompt.reshape(B * CTX_SEQ, D_MODEL)
    xs = x_sample.reshape(Bd * LAT_SEQ, D_MODEL)
    tm = 512
    nb = 2 if B % 2 == 0 else 1
    ctx_row = lambda t: 0
    lat_row = lambda t: 1 + t // (LAT_SEQ // tm)
    for l in range(L):
        lam_init = 0.8 - 0.6 * math.exp(-0.3 * l)
        last = l == L - 1
        xi = _ffn_call(xi, ada5, l, 0, ctx_row, n1, w1g, w1u, w1d, None, tm, "ffn1_ctx")
        xs = _ffn_call(xs, ada5, l, 0, lat_row, n1, w1g, w1u, w1d, None, tm, "ffn1_lat")
        xi, states = _mix_i_call(xi, ada5, l, nm, win, qn, wuq, kvn, wukv, lamv, subln, wout, states, nb, lam_init)
        xs = _mix_ii_call(xs, ada5, l, nm, win, qn, wuq, kvn, wukv, lamv, subln, wout, rope_tabs,
                          rpb_rows, caches, lam_init)
        xi = _ffn_call(xi, ada5, l, 2, ctx_row, n2, w2g, w2u, w2d, fnorm if last else None, tm, "ffn2_ctx")
        xs = _ffn_call(xs, ada5, l, 2, lat_row, n2, w2g, w2u, w2d, fnorm if last else None, tm, "ffn2_lat")

    return (xi.reshape(B, CTX_SEQ, D_MODEL), xs.reshape(Bd, LAT_SEQ, D_MODEL)) + tuple(states)
```

```python
import functools
import math

import numpy as np
import jax
import jax.numpy as jnp
from jax import lax
from jax.experimental import pallas as pl
from jax.experimental.pallas import tpu as pltpu

F32 = jnp.float32
BF16 = jnp.bfloat16

D_MODEL = 1024
FFN_DIM = 2816
NORM_EPS = 1e-6
ROPE_THETA = 10000.0
GRID_W = 64
N_ADA = 9
NEG_INF = -1e30

MLA_HEADS = 4
MLA_Q_LORA = 256
MLA_KV_LORA = 128
MLA_NOPE = 64
MLA_ROPE = 32
MLA_V = 64
DIFF_HEADS = 4
DIFF_DK = 32
DIFF_DV = 64
NAT_HEADS = 8
NAT_HD = 64
NAT_WIN_ROWS = 8
NAT_WIN_COLS = 16

LOG2E = math.log2(math.e)
MLA_SCALE = (MLA_NOPE + MLA_ROPE) ** -0.5 * LOG2E
DIFF_SCALE = DIFF_DK ** -0.5 * LOG2E
NAT_SCALE = NAT_HD ** -0.5 * LOG2E

CTX_SEQ = 256
LAT_SEQ = 1024
LAT_ROWS = LAT_SEQ // GRID_W

LANES = 128
MXU_DIM = 256
VMEM_LIMIT_BYTES = 58 * 1024 * 1024

P_CQ = 0
P_CKV = 256
P_KR = 384
P_DQ = 512
P_DK = 768
P_DV = 1024
P_NQ = 1280
P_NK = 1792
P_NV = 2304
P_COLS = 2816
P_SPLIT = 512
HEAD_BLOCK = 256
ADA_ROWS = 8

FFN_CHUNKS = ((0, 768), (768, 1536), (1536, 2304), (2304, 2816))


def _rope_tables():
    t = np.arange(LAT_SEQ)
    pos = np.stack([t // GRID_W, t % GRID_W], axis=0).astype(np.float64)
    lane = np.arange(LANES)
    p = lane % 32
    axis = (p >= 16).astype(np.int64)
    freqs = ROPE_THETA ** (-(p % 8).astype(np.float64) / 8.0)
    ang = pos[axis, :].T * freqs[None, :]
    first = (p % 16) < 8
    cos = np.cos(ang)
    sin = np.sin(ang)
    s_next = np.where(first[None, :], -sin, 0.0)
    s_prev = np.where(first[None, :], 0.0, sin)
    return cos.astype(np.float32), s_next.astype(np.float32), s_prev.astype(np.float32)


_ROPE_COS, _ROPE_SNEXT, _ROPE_SPREV = _rope_tables()


def _rmsnorm(x, g):
    ms = jnp.mean(x * x, axis=-1, keepdims=True)
    return x * lax.rsqrt(ms + NORM_EPS) * g


def _silu(x):
    return x / (1.0 + jnp.exp(-x))


def _dot(a, b):
    return jnp.dot(a, b, preferred_element_type=F32)


def _dot_t(a, b):
    return lax.dot_general(a, b, (((1,), (1,)), ((), ())), preferred_element_type=F32)


class _Proj:
    def __init__(self, a, b):
        self.a, self.b = a, b

    def __getitem__(self, idx):
        lo, hi = idx[1].start, idx[1].stop
        return self.a[:, lo:hi] if hi <= P_SPLIT else self.b[:, lo - P_SPLIT:hi - P_SPLIT]


def _lane_mask(width, lo, hi):
    lane = lax.broadcasted_iota(jnp.int32, (1, width), 1)
    return (lane >= lo) & (lane < hi)


def _head_masks(width=HEAD_BLOCK, group=64, n=4):
    return [_lane_mask(width, h * group, (h + 1) * group) for h in range(n)]


def _mla_qmasks():
    lane = lax.broadcasted_iota(jnp.int32, (1, HEAD_BLOCK + LANES), 1)
    out = []
    for h in range(MLA_HEADS):
        nope = (lane >= h * MLA_NOPE) & (lane < (h + 1) * MLA_NOPE)
        rope = (lane >= HEAD_BLOCK + h * MLA_ROPE) & (lane < HEAD_BLOCK + (h + 1) * MLA_ROPE)
        out.append(nope | rope)
    return out


def _tile32(blk):
    return blk + pltpu.roll(blk, 32, 1) + pltpu.roll(blk, 64, 1) + pltpu.roll(blk, 96, 1)


def _rope(x, cos, s_next, s_prev):
    return x * cos + pltpu.roll(x, LANES - 8, 1) * s_next + pltpu.roll(x, 8, 1) * s_prev


def _stack_masked(q, masks):
    zero = jnp.zeros_like(q)
    return jnp.concatenate([jnp.where(m, q, zero) for m in masks], axis=0)


def _scores_t(q, k_bf, kmasks, stacked):
    sk = k_bf.shape[0]
    if stacked:
        s = _dot_t(_stack_masked(k_bf, kmasks), q)
        return [s[h * sk:(h + 1) * sk] for h in range(len(kmasks))]
    zero = jnp.zeros_like(q)
    return [_dot_t(k_bf, jnp.where(m, q, zero)) for m in kmasks]


def _pv_t(vt_bf, p_bf, h, short_keys):
    if short_keys:
        return _dot(vt_bf, p_bf)[h * 64:(h + 1) * 64]
    return _dot(vt_bf[h * 64:(h + 1) * 64], p_bf)


def _exp_sum_t(s):
    e = jnp.exp2(s - jnp.max(s, axis=0, keepdims=True))
    return e, jnp.sum(e, axis=0, keepdims=True)


def _softmax_heads_t(q, k_bf, vt_bf, kmasks, stacked):
    parts = []
    for h, s in enumerate(_scores_t(q, k_bf, kmasks, stacked)):
        e, l = _exp_sum_t(s)
        parts.append(_pv_t(vt_bf, e.astype(BF16), h, stacked) / l)
    return jnp.concatenate(parts, axis=0)


def _diff_heads_t(q, k_bf, vt_bf, lam, subln_col, lam_init, stacked):
    n = DIFF_HEADS
    m1 = [_lane_mask(HEAD_BLOCK, h * 2 * DIFF_DK, h * 2 * DIFF_DK + DIFF_DK) for h in range(n)]
    m2 = [_lane_mask(HEAD_BLOCK, h * 2 * DIFF_DK + DIFF_DK, (h + 1) * 2 * DIFF_DK) for h in range(n)]
    s = _scores_t(q, k_bf, m1 + m2, stacked)
    parts = []
    for h in range(n):
        e1, l1 = _exp_sum_t(s[h])
        e2, l2 = _exp_sum_t(s[n + h])
        p = e1 * (1.0 / l1) - e2 * (lam / l2)
        oh = _pv_t(vt_bf, p.astype(BF16), h, stacked)
        ms = jnp.mean(oh * oh, axis=0, keepdims=True)
        parts.append(oh * lax.rsqrt(ms + NORM_EPS))
    return jnp.concatenate(parts, axis=0) * subln_col * (1.0 - lam_init)


def _lambda(lamv, lam_init):
    a = jnp.sum(lamv[0:1] * lamv[1:2], axis=-1, keepdims=True)
    b = jnp.sum(lamv[2:3] * lamv[3:4], axis=-1, keepdims=True)
    return jnp.exp(a) - jnp.exp(b) + lam_init


def _ada_kernel(c_ref, w_ref, b_ref, o_ref):
    s = _silu(c_ref[...]).astype(BF16)
    o_ref[...] = _dot(s, w_ref[...].astype(BF16)) + b_ref[...]


def _ada_call(cvec, w_ada, b_ada):
    L, _, n = w_ada.shape
    tn = 1024
    return pl.pallas_call(
        _ada_kernel,
        grid=(L, n // tn),
        in_specs=[
            pl.BlockSpec((ADA_ROWS, D_MODEL), lambda l, j: (0, 0)),
            pl.BlockSpec((None, D_MODEL, tn), lambda l, j: (l, 0, j)),
            pl.BlockSpec((None, 1, tn), lambda l, j: (l, 0, j)),
        ],
        out_specs=pl.BlockSpec((None, ADA_ROWS, tn), lambda l, j: (l, 0, j)),
        out_shape=jax.ShapeDtypeStruct((L, ADA_ROWS, n), F32),
        compiler_params=pltpu.CompilerParams(
            dimension_semantics=("arbitrary", "arbitrary"), vmem_limit_bytes=VMEM_LIMIT_BYTES),
        name="ada",
    )(cvec, w_ada, b_ada.reshape(L, 1, n))


def _ffn_tile(x_ref, o_ref, mod, g_ref, wg_ref, wu_ref, wd_ref, fn_ref):
    x = x_ref[...]
    h = (_rmsnorm(x, g_ref[...]) * (1.0 + mod[1:2]) + mod[0:1]).astype(BF16)
    acc = None
    for lo, hi in FFN_CHUNKS:
        g = _dot(h, wg_ref[:, lo:hi].astype(BF16))
        u = _dot(h, wu_ref[:, lo:hi].astype(BF16))
        part = _dot((_silu(g) * u).astype(BF16), wd_ref[lo:hi, :].astype(BF16))
        acc = part if acc is None else acc + part
    y = x + (0.5 * mod[2:3]) * acc
    if fn_ref is not None:
        y = _rmsnorm(y, fn_ref[...])
    o_ref[...] = y


def _ffn_kernel(xc_ref, xl_ref, mod_ref, g_ref, wg_ref, wu_ref, wd_ref, *rest, n_ctx_tiles, final):
    fn_ref = rest[0] if final else None
    oc_ref, ol_ref = rest[-2:]
    mod = mod_ref[...]
    t = pl.program_id(0)

    @pl.when(t < n_ctx_tiles)
    def _():
        _ffn_tile(xc_ref, oc_ref, mod, g_ref, wg_ref, wu_ref, wd_ref, fn_ref)

    @pl.when(t >= n_ctx_tiles)
    def _():
        _ffn_tile(xl_ref, ol_ref, mod, g_ref, wg_ref, wu_ref, wd_ref, fn_ref)


def _resident(shape, index_map):
    return pl.BlockSpec(shape, index_map, pipeline_mode=pl.Buffered(1))


def _ffn_call(xc, xl, ada5, layer, group, norm, wg, wu, wd, final_norm, tm, name):
    n_ctx = xc.shape[0] // tm
    n_lat = xl.shape[0] // tm
    per_lat = LAT_SEQ // tm
    ctx_blk = lambda t: (jnp.minimum(t, n_ctx - 1), 0)
    lat_blk = lambda t: (jnp.maximum(t - n_ctx, 0), 0)
    ada_row = lambda t: jnp.maximum(t - n_ctx, -per_lat) // per_lat + 1
    in_specs = [
        pl.BlockSpec((tm, D_MODEL), ctx_blk),
        pl.BlockSpec((tm, D_MODEL), lat_blk),
        pl.BlockSpec((None, None, None, 3, D_MODEL), lambda t: (layer, ada_row(t), group, 0, 0)),
        _resident((None, 1, D_MODEL), lambda t: (layer, 0, 0)),
        _resident((None, D_MODEL, FFN_DIM), lambda t: (layer, 0, 0)),
        _resident((None, D_MODEL, FFN_DIM), lambda t: (layer, 0, 0)),
        _resident((None, FFN_DIM, D_MODEL), lambda t: (layer, 0, 0)),
    ]
    args = [xc, xl, ada5, norm, wg, wu, wd]
    if final_norm is not None:
        in_specs.append(_resident((1, D_MODEL), lambda t: (0, 0)))
        args.append(final_norm)
    return pl.pallas_call(
        functools.partial(_ffn_kernel, n_ctx_tiles=n_ctx, final=final_norm is not None),
        grid=(n_ctx + n_lat,),
        in_specs=in_specs,
        out_specs=[pl.BlockSpec((tm, D_MODEL), ctx_blk), pl.BlockSpec((tm, D_MODEL), lat_blk)],
        out_shape=[jax.ShapeDtypeStruct(xc.shape, F32), jax.ShapeDtypeStruct(xl.shape, F32)],
        compiler_params=pltpu.CompilerParams(
            dimension_semantics=("arbitrary",), vmem_limit_bytes=VMEM_LIMIT_BYTES),
        name=name,
    )(*args)


def _mix_i_kernel(x_ref, mod_ref, g_ref, wina_ref, winb_ref, qn_ref, wuq_ref, kvn_ref, wukv_ref, lamv_ref, subln_ref,
                  wout_ref, *refs, nb, lam_init):
    y_ref, ckv_ref, kr_ref, dk_ref, dv_ref, nk_ref, nv_ref = refs[-7:]
    S = CTX_SEQ
    x = x_ref[...]
    mod = mod_ref[...]
    h = (_rmsnorm(x, g_ref[...]) * (1.0 + mod[1:2]) + mod[0:1]).astype(BF16)
    proj = _Proj(_dot(h, wina_ref[...]), _dot(h, winb_ref[...]))
    q_cat = _dot(_rmsnorm(proj[:, P_CQ:P_CQ + MLA_Q_LORA], qn_ref[...]).astype(BF16), wuq_ref[...])
    ckv = _rmsnorm(proj[:, P_CKV:P_CKV + MLA_KV_LORA], kvn_ref[...])
    kv = _dot(ckv.astype(BF16), wukv_ref[...])
    kr_blk = proj[:, P_KR:P_KR + LANES]
    k_cat = jnp.concatenate([kv[:, 0:HEAD_BLOCK], _tile32(kr_blk)], axis=1).astype(BF16)
    v_mla = kv[:, HEAD_BLOCK:2 * HEAD_BLOCK]
    q_cat = (q_cat * MLA_SCALE).astype(BF16)
    dq = (proj[:, P_DQ:P_DQ + HEAD_BLOCK] * DIFF_SCALE).astype(BF16)
    dk = proj[:, P_DK:P_DK + HEAD_BLOCK]
    dv = proj[:, P_DV:P_DV + HEAD_BLOCK]
    nq = (proj[:, P_NQ:P_NQ + 2 * HEAD_BLOCK] * NAT_SCALE).astype(BF16)
    nk = proj[:, P_NK:P_NK + 2 * HEAD_BLOCK]
    nv = proj[:, P_NV:P_NV + 2 * HEAD_BLOCK]
    dk_bf, nk_bf = dk.astype(BF16), nk.astype(BF16)
    lam = _lambda(lamv_ref[...], lam_init)
    hmasks = _head_masks()
    mla_qm = _mla_qmasks()

    outs = []
    for j in range(nb):
        r0, r1 = j * S, (j + 1) * S
        ckv_ref[j] = ckv[r0:r1]
        kr_ref[j] = kr_blk[r0:r1, 0:MLA_ROPE]
        for hh in range(DIFF_HEADS):
            dk_ref[j, hh] = dk[r0:r1, hh * 64:(hh + 1) * 64]
            dv_ref[j, hh] = dv[r0:r1, hh * 64:(hh + 1) * 64]
        for hh in range(NAT_HEADS):
            nk_ref[j, hh] = nk[r0:r1, hh * 64:(hh + 1) * 64]
            nv_ref[j, hh] = nv[r0:r1, hh * 64:(hh + 1) * 64]
        tr = lambda v: v.T.astype(BF16)
        ot_mla = _softmax_heads_t(q_cat[r0:r1], k_cat[r0:r1], tr(v_mla[r0:r1]), mla_qm, True)
        ot_diff = _diff_heads_t(dq[r0:r1], dk_bf[r0:r1], tr(dv[r0:r1]), lam, subln_ref[...], lam_init, True)
        ot_nat = [
            _softmax_heads_t(nq[r0:r1, b * HEAD_BLOCK:(b + 1) * HEAD_BLOCK], nk_bf[r0:r1, b * HEAD_BLOCK:(b + 1) * HEAD_BLOCK],
                             tr(nv[r0:r1, b * HEAD_BLOCK:(b + 1) * HEAD_BLOCK]), hmasks, True)
            for b in range(2)
        ]
        outs.append(jnp.concatenate([ot_mla, ot_diff] + ot_nat, axis=0).T.astype(BF16))
    o = jnp.concatenate(outs, axis=0) if nb > 1 else outs[0]
    y_ref[...] = x + mod[2:3] * _dot(o, wout_ref[...])


def _state_shapes(B, L):
    return [jax.ShapeDtypeStruct(s, F32) for s in (
        (B, L, CTX_SEQ, MLA_KV_LORA), (B, L, CTX_SEQ, MLA_ROPE),
        (B, L, DIFF_HEADS, CTX_SEQ, 2 * DIFF_DK), (B, L, DIFF_HEADS, CTX_SEQ, DIFF_DV),
        (B, L, NAT_HEADS, CTX_SEQ, NAT_HD), (B, L, NAT_HEADS, CTX_SEQ, NAT_HD))]


def _mix_i_call(x, ada5, layer, n_layers, mix_norm, win, qn, wuq, kvn, wukv, lamv, subln, wout, states, nb, lam_init):
    T = x.shape[0]
    B = T // CTX_SEQ
    tm = nb * CTX_SEQ
    lyr = lambda shape: _resident((None,) + shape, lambda i: (layer,) + (0,) * len(shape))
    any_spec = pl.BlockSpec(memory_space=pl.ANY)
    in_specs = [
        pl.BlockSpec((tm, D_MODEL), lambda i: (i, 0)),
        _resident((None, None, None, 3, D_MODEL), lambda i: (layer, 0, 1, 0, 0)),
        lyr((1, D_MODEL)),
        lyr((D_MODEL, P_SPLIT)),
        lyr((D_MODEL, P_COLS - P_SPLIT)),
        lyr((1, MLA_Q_LORA)),
        lyr((MLA_Q_LORA, HEAD_BLOCK + LANES)),
        lyr((1, MLA_KV_LORA)),
        lyr((MLA_KV_LORA, 2 * HEAD_BLOCK)),
        lyr((4, DIFF_DK)),
        lyr((HEAD_BLOCK, 1)),
        lyr((D_MODEL, D_MODEL)),
    ]
    n_fixed = len(in_specs)
    in_specs += [any_spec] * len(states)
    out_specs = [
        pl.BlockSpec((tm, D_MODEL), lambda i: (i, 0)),
        pl.BlockSpec((nb, None, CTX_SEQ, MLA_KV_LORA), lambda i: (i, layer, 0, 0)),
        pl.BlockSpec((nb, None, CTX_SEQ, MLA_ROPE), lambda i: (i, layer, 0, 0)),
        pl.BlockSpec((nb, None, DIFF_HEADS, CTX_SEQ, 2 * DIFF_DK), lambda i: (i, layer, 0, 0, 0)),
        pl.BlockSpec((nb, None, DIFF_HEADS, CTX_SEQ, DIFF_DV), lambda i: (i, layer, 0, 0, 0)),
        pl.BlockSpec((nb, None, NAT_HEADS, CTX_SEQ, NAT_HD), lambda i: (i, layer, 0, 0, 0)),
        pl.BlockSpec((nb, None, NAT_HEADS, CTX_SEQ, NAT_HD), lambda i: (i, layer, 0, 0, 0)),
    ]
    out_shape = [jax.ShapeDtypeStruct((T, D_MODEL), F32)] + _state_shapes(B, n_layers)
    res = pl.pallas_call(
        functools.partial(_mix_i_kernel, nb=nb, lam_init=lam_init),
        grid=(B // nb,),
        in_specs=in_specs,
        out_specs=out_specs,
        out_shape=out_shape,
        input_output_aliases={n_fixed + k: 1 + k for k in range(len(states))},
        compiler_params=pltpu.CompilerParams(
            dimension_semantics=("arbitrary",), vmem_limit_bytes=VMEM_LIMIT_BYTES),
        name="mix_ctx",
    )(x, ada5, mix_norm, *win, qn, wuq, kvn, wukv, lamv, subln, wout, *states)
    return res[0], tuple(res[1:])


PROJ_ROWS = 256
DENSE_QROWS = MXU_DIM
N_KEYS = CTX_SEQ + LAT_SEQ
BIAS_LANES = NAT_WIN_ROWS * LANES


def _mix_ii_kernel(x_ref, mod_ref, g_ref, wina_ref, winb_ref, qn_ref, wuq_ref, kvn_ref, wukv_ref, lamv_ref, subln_ref,
                   wout_ref, cos_ref, snext_ref, sprev_ref, rpb_ref,
                   cckv_ref, ckr_ref, cdk_ref, cdv_ref, cnk_ref, cnv_ref,
                   y_ref,
                   qcat_s, ckv_s, kcat_s, vmla_s, dq_s, dk_s, dv_s, nq_s, nk_s, nv_s, o_s, bias_ref,
                   vmlat_s, dvt_s, *, lam_init):
    S = LAT_SEQ
    C = CTX_SEQ
    mod = mod_ref[...]
    lam = _lambda(lamv_ref[...], lam_init)
    hmasks = _head_masks()
    mla_qm = _mla_qmasks()

    @pl.when(pl.program_id(0) == 0)
    def _():
        qc = lax.broadcasted_iota(jnp.int32, (GRID_W, LANES), 0)
        kc = lax.broadcasted_iota(jnp.int32, (GRID_W, LANES), 1) % GRID_W
        c_start = jnp.clip(qc - NAT_WIN_COLS // 2, 0, GRID_W - NAT_WIN_COLS)
        in_win = (kc >= c_start) & (kc < c_start + NAT_WIN_COLS)
        for par in range(2):
            for h in range(NAT_HEADS):
                for p in range(NAT_WIN_ROWS):
                    row = jnp.broadcast_to(rpb_ref[par, h, p:p + 1, :], (GRID_W, LANES))
                    tile = pltpu.roll(row, LANES - (NAT_WIN_COLS - 1), 1, stride=1, stride_axis=0)
                    bias_ref[par, h // 4, (h % 4) * GRID_W:(h % 4 + 1) * GRID_W, p * LANES:(p + 1) * LANES] = (
                        jnp.where(in_win, tile * LOG2E, NEG_INF))

    ckv_s[0:C, :] = cckv_ref[...]
    kcat_s[0:C, HEAD_BLOCK:] = ckr_ref[...]
    dk_s[0:C, :] = cdk_ref[...]
    dv_s[0:C, :] = cdv_ref[...]

    def proj_body(i, carry):
        r = pl.multiple_of(i * PROJ_ROWS, PROJ_ROWS)
        rows = pl.ds(r, PROJ_ROWS)
        krows = pl.ds(C + r, PROJ_ROWS)
        cos, s_next, s_prev = cos_ref[rows, :], snext_ref[rows, :], sprev_ref[rows, :]
        rope = lambda v: _rope(v, cos, s_next, s_prev)
        x = x_ref[rows, :]
        h = (_rmsnorm(x, g_ref[...]) * (1.0 + mod[1:2]) + mod[0:1]).astype(BF16)
        proj = _Proj(_dot(h, wina_ref[...]), _dot(h, winb_ref[...]))
        q_cat = _dot(_rmsnorm(proj[:, P_CQ:P_CQ + MLA_Q_LORA], qn_ref[...]).astype(BF16), wuq_ref[...])
        qcat_s[rows, 0:HEAD_BLOCK] = (q_cat[:, 0:HEAD_BLOCK] * MLA_SCALE).astype(BF16)
        qcat_s[rows, HEAD_BLOCK:] = (rope(q_cat[:, HEAD_BLOCK:]) * MLA_SCALE).astype(BF16)
        ckv_s[krows, :] = _rmsnorm(proj[:, P_CKV:P_CKV + MLA_KV_LORA], kvn_ref[...]).astype(BF16)
        kcat_s[krows, HEAD_BLOCK:] = _tile32(rope(proj[:, P_KR:P_KR + LANES])).astype(BF16)
        for b in range(2):
            c0 = b * LANES
            dq_s[rows, c0:c0 + LANES] = (rope(proj[:, P_DQ + c0:P_DQ + c0 + LANES]) * DIFF_SCALE).astype(BF16)
            dk_s[krows, c0:c0 + LANES] = rope(proj[:, P_DK + c0:P_DK + c0 + LANES]).astype(BF16)
        dv_s[krows, :] = proj[:, P_DV:P_DV + HEAD_BLOCK].astype(BF16)
        nq_s[rows, :] = (proj[:, P_NQ:P_NQ + 2 * HEAD_BLOCK] * NAT_SCALE).astype(BF16)
        nk_s[rows, :] = proj[:, P_NK:P_NK + 2 * HEAD_BLOCK].astype(BF16)
        nv_s[rows, :] = proj[:, P_NV:P_NV + 2 * HEAD_BLOCK].astype(BF16)
        return carry

    lax.fori_loop(0, S // PROJ_ROWS, proj_body, 0)

    def kv_body(i, carry):
        rows = pl.ds(pl.multiple_of(i * PROJ_ROWS, PROJ_ROWS), PROJ_ROWS)
        kv = _dot(ckv_s[rows, :], wukv_ref[...])
        kcat_s[rows, 0:HEAD_BLOCK] = kv[:, 0:HEAD_BLOCK].astype(BF16)
        vmla_s[rows, :] = kv[:, HEAD_BLOCK:].astype(BF16)
        return carry

    lax.fori_loop(0, N_KEYS // PROJ_ROWS, kv_body, 0)

    for j in range(N_KEYS // PROJ_ROWS):
        c0, c1 = j * PROJ_ROWS, (j + 1) * PROJ_ROWS
        vmlat_s[:, c0:c1] = vmla_s[c0:c1, :].astype(F32).T.astype(BF16)
        dvt_s[:, c0:c1] = dv_s[c0:c1, :].astype(F32).T.astype(BF16)

    def dense_body(i, carry):
        rows = pl.ds(pl.multiple_of(i * DENSE_QROWS, DENSE_QROWS), DENSE_QROWS)
        ot_mla = _softmax_heads_t(qcat_s[rows, :], kcat_s[...], vmlat_s[...], mla_qm, False)
        o_s[rows, 0:HEAD_BLOCK] = ot_mla.T.astype(BF16)
        ot_diff = _diff_heads_t(dq_s[rows, :], dk_s[...], dvt_s[...], lam, subln_ref[...], lam_init, False)
        o_s[rows, HEAD_BLOCK:2 * HEAD_BLOCK] = ot_diff.T.astype(BF16)
        return carry

    lax.fori_loop(0, S // DENSE_QROWS, dense_body, 0)

    for r in range(LAT_ROWS):
        rs = min(max(r - NAT_WIN_ROWS // 2, 0), LAT_ROWS - NAT_WIN_ROWS)
        dr0 = rs - r + (NAT_WIN_ROWS - 1)
        par = dr0 % 2
        off = GRID_W * (dr0 - par)
        q0, q1 = r * GRID_W, (r + 1) * GRID_W
        k0, k1 = rs * GRID_W, (rs + NAT_WIN_ROWS) * GRID_W
        for b in range(2):
            c0, c1 = b * HEAD_BLOCK, (b + 1) * HEAD_BLOCK
            q = nq_s[q0:q1, c0:c1]
            zero = jnp.zeros_like(q)
            qs = jnp.concatenate([jnp.where(m, q, zero) for m in hmasks], axis=0)
            sw = _dot_t(qs, nk_s[k0:k1, c0:c1]) + bias_ref[par, b, :, off:off + NAT_WIN_ROWS * GRID_W]
            sc = _dot_t(qs, cnk_ref[:, c0:c1])
            m = jnp.maximum(jnp.max(sw, axis=-1, keepdims=True), jnp.max(sc, axis=-1, keepdims=True))
            ew = jnp.exp2(sw - m)
            ec = jnp.exp2(sc - m)
            l = jnp.sum(ew, axis=-1, keepdims=True) + jnp.sum(ec, axis=-1, keepdims=True)
            o = (_dot(ew.astype(BF16), nv_s[k0:k1, c0:c1]) + _dot(ec.astype(BF16), cnv_ref[:, c0:c1])) / l
            of = None
            for hh, hm in enumerate(hmasks):
                part = jnp.where(hm, o[hh * GRID_W:(hh + 1) * GRID_W], 0.0)
                of = part if of is None else of + part
            o_s[q0:q1, 2 * HEAD_BLOCK + c0:2 * HEAD_BLOCK + c1] = of.astype(BF16)

    def out_body(i, carry):
        rows = pl.ds(pl.multiple_of(i * PROJ_ROWS, PROJ_ROWS), PROJ_ROWS)
        y_ref[rows, :] = x_ref[rows, :] + mod[2:3] * _dot(o_s[rows, :], wout_ref[...])
        return carry

    lax.fori_loop(0, S // PROJ_ROWS, out_body, 0)


def _mix_ii_call(x, ada5, layer, mix_norm, win, qn, wuq, kvn, wukv, lamv, subln, wout, rope_tabs, rpb_rows,
                 caches, lam_init):
    T = x.shape[0]
    Bd = T // LAT_SEQ
    cckv, ckr, cdk, cdv, cnk, cnv = caches
    lyr = lambda shape: _resident((None,) + shape, lambda i: (layer,) + (0,) * len(shape))
    cache = lambda width: pl.BlockSpec((None, None, CTX_SEQ, width), lambda i: (i, layer, 0, 0))
    tab = _resident((LAT_SEQ, LANES), lambda i: (0, 0))
    in_specs = [
        pl.BlockSpec((LAT_SEQ, D_MODEL), lambda i: (i, 0), pipeline_mode=pl.Buffered(1)),
        pl.BlockSpec((None, None, None, 3, D_MODEL), lambda i: (layer, i + 1, 1, 0, 0)),
        lyr((1, D_MODEL)),
        lyr((D_MODEL, P_SPLIT)),
        lyr((D_MODEL, P_COLS - P_SPLIT)),
        lyr((1, MLA_Q_LORA)),
        lyr((MLA_Q_LORA, HEAD_BLOCK + LANES)),
        lyr((1, MLA_KV_LORA)),
        lyr((MLA_KV_LORA, 2 * HEAD_BLOCK)),
        lyr((4, DIFF_DK)),
        lyr((HEAD_BLOCK, 1)),
        lyr((D_MODEL, D_MODEL)),
        tab, tab, tab,
        lyr((2, NAT_HEADS, NAT_WIN_ROWS, LANES)),
        cache(MLA_KV_LORA), cache(LANES), cache(HEAD_BLOCK), cache(HEAD_BLOCK),
        cache(2 * HEAD_BLOCK), cache(2 * HEAD_BLOCK),
    ]
    scratch = [
        pltpu.VMEM((LAT_SEQ, HEAD_BLOCK + LANES), BF16),
        pltpu.VMEM((N_KEYS, MLA_KV_LORA), BF16),
        pltpu.VMEM((N_KEYS, HEAD_BLOCK + LANES), BF16),
        pltpu.VMEM((N_KEYS, HEAD_BLOCK), BF16),
        pltpu.VMEM((LAT_SEQ, HEAD_BLOCK), BF16),
        pltpu.VMEM((N_KEYS, HEAD_BLOCK), BF16),
        pltpu.VMEM((N_KEYS, HEAD_BLOCK), BF16),
        pltpu.VMEM((LAT_SEQ, 2 * HEAD_BLOCK), BF16),
        pltpu.VMEM((LAT_SEQ, 2 * HEAD_BLOCK), BF16),
        pltpu.VMEM((LAT_SEQ, 2 * HEAD_BLOCK), BF16),
        pltpu.VMEM((LAT_SEQ, D_MODEL), BF16),
        pltpu.VMEM((2, 2, HEAD_BLOCK, BIAS_LANES), F32),
        pltpu.VMEM((HEAD_BLOCK, N_KEYS), BF16),
        pltpu.VMEM((HEAD_BLOCK, N_KEYS), BF16),
    ]
    return pl.pallas_call(
        functools.partial(_mix_ii_kernel, lam_init=lam_init),
        grid=(Bd,),
        in_specs=in_specs,
        out_specs=pl.BlockSpec((LAT_SEQ, D_MODEL), lambda i: (i, 0)),
        out_shape=jax.ShapeDtypeStruct((T, D_MODEL), F32),
        scratch_shapes=scratch,
        compiler_params=pltpu.CompilerParams(
            dimension_semantics=("arbitrary",), vmem_limit_bytes=VMEM_LIMIT_BYTES),
        name="mix_lat",
    )(x, ada5, mix_norm, *win, qn, wuq, kvn, wukv, lamv, subln, wout, *rope_tabs, rpb_rows, *caches)


def _rpb_rows(rpb):
    L = rpb.shape[0]
    n_dr, n_dc = 2 * NAT_WIN_ROWS - 1, 2 * NAT_WIN_COLS - 1
    padded = jnp.pad(rpb.astype(F32), ((0, 0), (0, 0), (0, 2 * NAT_WIN_ROWS + 1 - n_dr), (0, GRID_W - n_dc)))
    even = padded[:, :, 0:2 * NAT_WIN_ROWS].reshape(L, NAT_HEADS, NAT_WIN_ROWS, LANES)
    odd = padded[:, :, 1:2 * NAT_WIN_ROWS + 1].reshape(L, NAT_HEADS, NAT_WIN_ROWS, LANES)
    return jnp.stack([even, odd], axis=1)


def _heads_to_lanes(cache):
    B, L, H, S, d = cache.shape
    return jnp.transpose(cache, (0, 1, 3, 2, 4)).reshape(B, L, S, H * d).astype(BF16)


def kernel(x_prompt, x_sample, cache_mla_ckv, cache_mla_krope, cache_diff_k, cache_diff_v, cache_nat_k, cache_nat_v, c, c_ctx, w_ada, b_ada, ffn1_norm, ffn1_w_gate, ffn1_w_up, ffn1_w_down, mix_norm, w_in, mla_q_norm, mla_w_uq, mla_kv_norm, mla_w_ukv, diff_lambda_q1, diff_lambda_k1, diff_lambda_q2, diff_lambda_k2, diff_subln, nat_rpb, w_out, ffn2_norm, ffn2_w_gate, ffn2_w_up, ffn2_w_down, final_norm):
    L = w_ada.shape[0]
    B, S_ctx, _ = x_prompt.shape
    Bd, S_lat, _ = x_sample.shape
    assert S_ctx == CTX_SEQ and S_lat == LAT_SEQ and 1 + Bd <= ADA_ROWS
    assert cache_mla_ckv.shape[2] == CTX_SEQ

    cvec = jnp.concatenate([c_ctx[None, :], c, jnp.zeros((ADA_ROWS - 1 - Bd, D_MODEL), F32)], axis=0)
    ada5 = _ada_call(cvec, w_ada, b_ada).reshape(L, ADA_ROWS, 3, 3, D_MODEL)

    bf = lambda w: w.astype(BF16)
    split = MLA_Q_LORA + MLA_KV_LORA + MLA_ROPE
    win = (bf(jnp.concatenate([w_in[:, :, :split], jnp.zeros((L, D_MODEL, LANES - MLA_ROPE), F32)], axis=2)),
           bf(w_in[:, :, split:]))
    wuq4 = mla_w_uq.reshape(L, MLA_Q_LORA, MLA_HEADS, MLA_NOPE + MLA_ROPE)
    wuq = bf(jnp.concatenate([wuq4[..., :MLA_NOPE].reshape(L, MLA_Q_LORA, -1),
                              wuq4[..., MLA_NOPE:].reshape(L, MLA_Q_LORA, -1)], axis=2))
    wukv4 = mla_w_ukv.reshape(L, MLA_KV_LORA, MLA_HEADS, MLA_NOPE + MLA_V)
    wukv = bf(jnp.concatenate([wukv4[..., :MLA_NOPE].reshape(L, MLA_KV_LORA, -1),
                               wukv4[..., MLA_NOPE:].reshape(L, MLA_KV_LORA, -1)], axis=2))
    wout = bf(w_out)
    lamv = jnp.stack([diff_lambda_q1, diff_lambda_k1, diff_lambda_q2, diff_lambda_k2], axis=1)
    subln = jnp.tile(diff_subln, (1, DIFF_HEADS)).reshape(L, HEAD_BLOCK, 1)
    r3 = lambda a: a.reshape(L, 1, a.shape[-1])
    n1, n2, nm, qn, kvn = r3(ffn1_norm), r3(ffn2_norm), r3(mix_norm), r3(mla_q_norm), r3(mla_kv_norm)
    fnorm = final_norm.reshape(1, D_MODEL)

    caches = (bf(cache_mla_ckv), bf(jnp.tile(cache_mla_krope, (1, 1, 1, LANES // MLA_ROPE))),
              _heads_to_lanes(cache_diff_k), _heads_to_lanes(cache_diff_v),
              _heads_to_lanes(cache_nat_k), _heads_to_lanes(cache_nat_v))
    rope_tabs = (jnp.asarray(_ROPE_COS), jnp.asarray(_ROPE_SNEXT), jnp.asarray(_ROPE_SPREV))
    rpb_rows = _rpb_rows(nat_rpb)

    states = tuple(jnp.zeros(s.shape, s.dtype) for s in _state_shapes(B, L))

    xi = x_prompt.reshape(B * CTX_SEQ, D_MODEL)
    xs = x_sample.reshape(Bd * LAT_SEQ, D_MODEL)
    tm = 512
    nb = 2 if B % 2 == 0 else 1
    for l in range(L):
        lam_init = 0.8 - 0.6 * math.exp(-0.3 * l)
        last = l == L - 1
        xi, xs = _ffn_call(xi, xs, ada5, l, 0, n1, ffn1_w_gate, ffn1_w_up, ffn1_w_down, None, tm, "ffn1")
        xi, states = _mix_i_call(xi, ada5, l, L, nm, win, qn, wuq, kvn, wukv, lamv, subln, wout, states, nb, lam_init)
        xs = _mix_ii_call(xs, ada5, l, nm, win, qn, wuq, kvn, wukv, lamv, subln, wout, rope_tabs,
                          rpb_rows, caches, lam_init)
        xi, xs = _ffn_call(xi, xs, ada5, l, 2, n2, ffn2_w_gate, ffn2_w_up, ffn2_w_down,
                           fnorm if last else None, tm, "ffn2")

    return (xi.reshape(B, CTX_SEQ, D_MODEL), xs.reshape(Bd, LAT_SEQ, D_MODEL)) + tuple(states)
```

```python
import functools
import math

import numpy as np
import jax
import jax.numpy as jnp
from jax import lax
from jax.experimental import pallas as pl
from jax.experimental.pallas import tpu as pltpu

F32 = jnp.float32
BF16 = jnp.bfloat16

D_MODEL = 1024
FFN_DIM = 2816
NORM_EPS = 1e-6
ROPE_THETA = 10000.0
GRID_W = 64
N_ADA = 9
NEG_INF = -1e30

MLA_HEADS = 4
MLA_Q_LORA = 256
MLA_KV_LORA = 128
MLA_NOPE = 64
MLA_ROPE = 32
MLA_V = 64
DIFF_HEADS = 4
DIFF_DK = 32
DIFF_DV = 64
NAT_HEADS = 8
NAT_HD = 64
NAT_WIN_ROWS = 8
NAT_WIN_COLS = 16

LOG2E = math.log2(math.e)
MLA_SCALE = (MLA_NOPE + MLA_ROPE) ** -0.5 * LOG2E
DIFF_SCALE = DIFF_DK ** -0.5 * LOG2E
NAT_SCALE = NAT_HD ** -0.5 * LOG2E

CTX_SEQ = 256
LAT_SEQ = 1024
LAT_ROWS = LAT_SEQ // GRID_W

LANES = 128
MXU_DIM = 256
VMEM_LIMIT_BYTES = 58 * 1024 * 1024

P_CQ = 0
P_CKV = 256
P_KR = 384
P_DQ = 512
P_DK = 768
P_DV = 1024
P_NQ = 1280
P_NK = 1792
P_NV = 2304
P_COLS = 2816
P_SPLIT = 512
HEAD_BLOCK = 256
ADA_ROWS = 8

FFN_CHUNKS = ((0, 512), (512, 1024), (1024, 1536), (1536, 2048), (2048, 2560), (2560, 2816))
FFN_TILE = 512
N_STATES = 6


def _rope_tables():
    t = np.arange(LAT_SEQ)
    pos = np.stack([t // GRID_W, t % GRID_W], axis=0).astype(np.float64)
    lane = np.arange(LANES)
    p = lane % 32
    axis = (p >= 16).astype(np.int64)
    freqs = ROPE_THETA ** (-(p % 8).astype(np.float64) / 8.0)
    ang = pos[axis, :].T * freqs[None, :]
    first = (p % 16) < 8
    cos = np.cos(ang)
    sin = np.sin(ang)
    s_next = np.where(first[None, :], -sin, 0.0)
    s_prev = np.where(first[None, :], 0.0, sin)
    return cos.astype(np.float32), s_next.astype(np.float32), s_prev.astype(np.float32)


_ROPE_COS, _ROPE_SNEXT, _ROPE_SPREV = _rope_tables()


def _rmsnorm(x, g):
    ms = jnp.mean(x * x, axis=-1, keepdims=True)
    return x * lax.rsqrt(ms + NORM_EPS) * g


def _silu(x):
    return x / (1.0 + jnp.exp(-x))


def _dot(a, b):
    return jnp.dot(a, b, preferred_element_type=F32)


def _dot_t(a, b):
    return lax.dot_general(a, b, (((1,), (1,)), ((), ())), preferred_element_type=F32)


class _Proj:
    def __init__(self, a, b):
        self.a, self.b = a, b

    def __getitem__(self, idx):
        lo, hi = idx[1].start, idx[1].stop
        return self.a[:, lo:hi] if hi <= P_SPLIT else self.b[:, lo - P_SPLIT:hi - P_SPLIT]


def _lane_mask(width, lo, hi):
    lane = lax.broadcasted_iota(jnp.int32, (1, width), 1)
    return (lane >= lo) & (lane < hi)


def _head_masks(width=HEAD_BLOCK, group=64, n=4):
    return [_lane_mask(width, h * group, (h + 1) * group) for h in range(n)]


def _mla_qmasks():
    lane = lax.broadcasted_iota(jnp.int32, (1, HEAD_BLOCK + LANES), 1)
    out = []
    for h in range(MLA_HEADS):
        nope = (lane >= h * MLA_NOPE) & (lane < (h + 1) * MLA_NOPE)
        rope = (lane >= HEAD_BLOCK + h * MLA_ROPE) & (lane < HEAD_BLOCK + (h + 1) * MLA_ROPE)
        out.append(nope | rope)
    return out


def _tile32(blk):
    return blk + pltpu.roll(blk, 32, 1) + pltpu.roll(blk, 64, 1) + pltpu.roll(blk, 96, 1)


def _rope(x, cos, s_next, s_prev):
    return x * cos + pltpu.roll(x, LANES - 8, 1) * s_next + pltpu.roll(x, 8, 1) * s_prev


def _stack_masked(q, masks):
    zero = jnp.zeros_like(q)
    return jnp.concatenate([jnp.where(m, q, zero) for m in masks], axis=0)


def _scores_t(q, k_bf, kmasks, stacked):
    sk = k_bf.shape[0]
    if stacked:
        s = _dot_t(_stack_masked(k_bf, kmasks), q)
        return [s[h * sk:(h + 1) * sk] for h in range(len(kmasks))]
    zero = jnp.zeros_like(k_bf)
    return [_dot_t(jnp.where(m, k_bf, zero), q) for m in kmasks]


def _pv_t(vt_bf, p_bf, h, short_keys):
    if short_keys:
        return _dot(vt_bf, p_bf)[h * 64:(h + 1) * 64]
    return _dot(vt_bf[h * 64:(h + 1) * 64], p_bf)


def _exp_sum_t(s):
    e = jnp.exp2(s - jnp.max(s, axis=0, keepdims=True))
    return e, jnp.sum(e, axis=0, keepdims=True)


def _softmax_heads_t(q, k_bf, vt_bf, kmasks, stacked):
    parts = []
    for h, s in enumerate(_scores_t(q, k_bf, kmasks, stacked)):
        e, l = _exp_sum_t(s)
        parts.append(_pv_t(vt_bf, e.astype(BF16), h, stacked) / l)
    return jnp.concatenate(parts, axis=0)


def _diff_heads_t(q, k_bf, vt_bf, lam, subln_col, lam_init, stacked):
    n = DIFF_HEADS
    m1 = [_lane_mask(HEAD_BLOCK, h * 2 * DIFF_DK, h * 2 * DIFF_DK + DIFF_DK) for h in range(n)]
    m2 = [_lane_mask(HEAD_BLOCK, h * 2 * DIFF_DK + DIFF_DK, (h + 1) * 2 * DIFF_DK) for h in range(n)]
    s = _scores_t(q, k_bf, m1 + m2, stacked)
    parts = []
    for h in range(n):
        e1, l1 = _exp_sum_t(s[h])
        e2, l2 = _exp_sum_t(s[n + h])
        p = e1 * (1.0 / l1) - e2 * (lam / l2)
        oh = _pv_t(vt_bf, p.astype(BF16), h, stacked)
        ms = jnp.mean(oh * oh, axis=0, keepdims=True)
        parts.append(oh * lax.rsqrt(ms + NORM_EPS))
    return jnp.concatenate(parts, axis=0) * subln_col * (1.0 - lam_init)


def _lambda(lamv, lam_init):
    a = jnp.sum(lamv[0:1] * lamv[1:2], axis=-1, keepdims=True)
    b = jnp.sum(lamv[2:3] * lamv[3:4], axis=-1, keepdims=True)
    return jnp.exp(a) - jnp.exp(b) + lam_init


def _ada_kernel(c_ref, w_ref, b_ref, o_ref):
    s = _silu(c_ref[...]).astype(BF16)
    o_ref[...] = _dot(s, w_ref[...].astype(BF16)) + b_ref[...]


def _ada_call(cvec, w_ada, b_ada):
    L, _, n = w_ada.shape
    tn = 1024
    return pl.pallas_call(
        _ada_kernel,
        grid=(L, n // tn),
        in_specs=[
            pl.BlockSpec((ADA_ROWS, D_MODEL), lambda l, j: (0, 0)),
            pl.BlockSpec((None, D_MODEL, tn), lambda l, j: (l, 0, j)),
            pl.BlockSpec((None, 1, tn), lambda l, j: (l, 0, j)),
        ],
        out_specs=pl.BlockSpec((None, ADA_ROWS, tn), lambda l, j: (l, 0, j)),
        out_shape=jax.ShapeDtypeStruct((L, ADA_ROWS, n), F32),
        compiler_params=pltpu.CompilerParams(
            dimension_semantics=("arbitrary", "arbitrary"), vmem_limit_bytes=VMEM_LIMIT_BYTES),
        name="ada",
    )(cvec, w_ada, b_ada.reshape(L, 1, n))


def _ffn_tile(x_ref, o_ref, mod, g_ref, wg_ref, wu_ref, wd_ref, fn_ref, chunk_copies=None):
    x = x_ref[...]
    h = (_rmsnorm(x, g_ref[...]) * (1.0 + mod[1:2]) + mod[0:1]).astype(BF16)
    acc = None
    for k, (lo, hi) in enumerate(FFN_CHUNKS):
        for c in (chunk_copies[k] if chunk_copies else ()):
            c.wait()
        g = _dot(h, wg_ref[:, lo:hi].astype(BF16))
        u = _dot(h, wu_ref[:, lo:hi].astype(BF16))
        part = _dot((_silu(g) * u).astype(BF16), wd_ref[lo:hi, :].astype(BF16))
        acc = part if acc is None else acc + part
    y = x + (0.5 * mod[2:3]) * acc
    if fn_ref is not None:
        y = _rmsnorm(y, fn_ref[...])
    o_ref[...] = y


def _state_fill_copies(t, state_refs, zero_refs, sem, n_layers):
    per_tile = FFN_TILE // CTX_SEQ
    copies = []
    for j in range(per_tile):
        for l in range(n_layers):
            for st, z in zip(state_refs, zero_refs):
                dst = st.at[t * per_tile + j, l]
                if dst.shape == z.shape:
                    copies.append(pltpu.make_async_copy(z, dst, sem))
                else:
                    nh = z.shape[0]
                    copies += [pltpu.make_async_copy(z, dst.at[h:h + nh], sem) for h in range(0, dst.shape[0], nh)]
    return copies


def _weight_copies(layer, wg_hbm, wu_hbm, wd_hbm, wg_s, wu_s, wd_s, sems):
    return [(pltpu.make_async_copy(wg_hbm.at[layer, :, lo:hi], wg_s.at[:, lo:hi], sems.at[k]),
             pltpu.make_async_copy(wu_hbm.at[layer, :, lo:hi], wu_s.at[:, lo:hi], sems.at[k]),
             pltpu.make_async_copy(wd_hbm.at[layer, lo:hi, :], wd_s.at[lo:hi, :], sems.at[k]))
            for k, (lo, hi) in enumerate(FFN_CHUNKS)]


def _ffn_kernel(xc_ref, xl_ref, mod_ref, g_ref, wg_hbm, wu_hbm, wd_hbm, *rest,
                layer, n_ctx_tiles, final, fill_layers):
    fn_ref = rest[0] if final else None
    oc_ref, ol_ref = rest[final:final + 2]
    n_out = 2 + (N_STATES if fill_layers else 0)
    wg_s, wu_s, wd_s, wsems = rest[final + n_out:final + n_out + 4]
    mod = mod_ref[...]
    t = pl.program_id(0)

    def fill_copies():
        if not fill_layers:
            return []
        state_refs = rest[final + 2:final + n_out]
        z_ckv, z_kr, z_heads, zsem = rest[final + n_out + 4:]
        return _state_fill_copies(t, state_refs, (z_ckv, z_kr, z_heads, z_heads, z_heads, z_heads), zsem, fill_layers)

    def ctx_tile(chunk_copies):
        fill = fill_copies()
        for c in fill:
            c.start()
        _ffn_tile(xc_ref, oc_ref, mod, g_ref, wg_s, wu_s, wd_s, fn_ref, chunk_copies)
        for c in fill:
            c.wait()

    @pl.when(t == 0)
    def _():
        if fill_layers:
            for z in rest[final + n_out + 4:-1]:
                z[...] = jnp.zeros(z.shape, z.dtype)
        chunk_copies = _weight_copies(layer, wg_hbm, wu_hbm, wd_hbm, wg_s, wu_s, wd_s, wsems)
        for copies in chunk_copies:
            for c in copies:
                c.start()
        ctx_tile(chunk_copies)

    @pl.when((t > 0) & (t < n_ctx_tiles))
    def _():
        ctx_tile(None)

    @pl.when(t >= n_ctx_tiles)
    def _():
        _ffn_tile(xl_ref, ol_ref, mod, g_ref, wg_s, wu_s, wd_s, fn_ref)


def _resident(shape, index_map):
    return pl.BlockSpec(shape, index_map, pipeline_mode=pl.Buffered(1))


def _state_shapes(B, L):
    return [jax.ShapeDtypeStruct(s, F32) for s in (
        (B, L, CTX_SEQ, MLA_KV_LORA), (B, L, CTX_SEQ, MLA_ROPE),
        (B, L, DIFF_HEADS, CTX_SEQ, 2 * DIFF_DK), (B, L, DIFF_HEADS, CTX_SEQ, DIFF_DV),
        (B, L, NAT_HEADS, CTX_SEQ, NAT_HD), (B, L, NAT_HEADS, CTX_SEQ, NAT_HD))]


def _ffn_call(xc, xl, ada5, layer, group, norm, wg, wu, wd, final_norm, name, fill_layers=0):
    tm = FFN_TILE
    n_ctx = xc.shape[0] // tm
    n_lat = xl.shape[0] // tm
    per_lat = LAT_SEQ // tm
    ctx_blk = lambda t: (jnp.minimum(t, n_ctx - 1), 0)
    lat_blk = lambda t: (jnp.maximum(t - n_ctx, 0), 0)
    ada_row = lambda t: jnp.maximum(t - n_ctx, -per_lat) // per_lat + 1
    in_specs = [
        pl.BlockSpec((tm, D_MODEL), ctx_blk),
        pl.BlockSpec((tm, D_MODEL), lat_blk),
        pl.BlockSpec((None, None, None, 3, D_MODEL), lambda t: (layer, ada_row(t), group, 0, 0)),
        _resident((None, 1, D_MODEL), lambda t: (layer, 0, 0)),
    ] + [pl.BlockSpec(memory_space=pl.ANY)] * 3
    args = [xc, xl, ada5, norm, wg, wu, wd]
    if final_norm is not None:
        in_specs.append(_resident((1, D_MODEL), lambda t: (0, 0)))
        args.append(final_norm)
    out_specs = [pl.BlockSpec((tm, D_MODEL), ctx_blk), pl.BlockSpec((tm, D_MODEL), lat_blk)]
    out_shape = [jax.ShapeDtypeStruct(xc.shape, F32), jax.ShapeDtypeStruct(xl.shape, F32)]
    scratch = [pltpu.VMEM((D_MODEL, FFN_DIM), F32), pltpu.VMEM((D_MODEL, FFN_DIM), F32),
               pltpu.VMEM((FFN_DIM, D_MODEL), F32), pltpu.SemaphoreType.DMA((len(FFN_CHUNKS),))]
    if fill_layers:
        states = _state_shapes(xc.shape[0] // CTX_SEQ, fill_layers)
        out_specs += [pl.BlockSpec(memory_space=pl.ANY)] * N_STATES
        out_shape += states
        scratch += [pltpu.VMEM(states[k].shape[2:], F32) for k in (0, 1, 2)] + [pltpu.SemaphoreType.DMA(())]
    return pl.pallas_call(
        functools.partial(_ffn_kernel, layer=layer, n_ctx_tiles=n_ctx, final=final_norm is not None,
                          fill_layers=fill_layers),
        grid=(n_ctx + n_lat,),
        in_specs=in_specs,
        out_specs=out_specs,
        out_shape=out_shape,
        scratch_shapes=scratch,
        compiler_params=pltpu.CompilerParams(
            dimension_semantics=("arbitrary",), vmem_limit_bytes=VMEM_LIMIT_BYTES),
        name=name,
    )(*args)


def _mix_i_kernel(x_ref, mod_ref, g_ref, wina_ref, winb_ref, qn_ref, wuq_ref, kvn_ref, wukv_ref, lamv_ref, subln_ref,
                  wout_ref, *refs, nb, lam_init):
    y_ref, ckv_ref, kr_ref, dk_ref, dv_ref, nk_ref, nv_ref = refs[-7:]
    S = CTX_SEQ
    x = x_ref[...]
    mod = mod_ref[...]
    h = (_rmsnorm(x, g_ref[...]) * (1.0 + mod[1:2]) + mod[0:1]).astype(BF16)
    proj = _Proj(_dot(h, wina_ref[...]), _dot(h, winb_ref[...]))
    q_cat = _dot(_rmsnorm(proj[:, P_CQ:P_CQ + MLA_Q_LORA], qn_ref[...]).astype(BF16), wuq_ref[...])
    ckv = _rmsnorm(proj[:, P_CKV:P_CKV + MLA_KV_LORA], kvn_ref[...])
    kv = _dot(ckv.astype(BF16), wukv_ref[...])
    kr_blk = proj[:, P_KR:P_KR + LANES]
    k_cat = jnp.concatenate([kv[:, 0:HEAD_BLOCK], _tile32(kr_blk)], axis=1).astype(BF16)
    v_mla = kv[:, HEAD_BLOCK:2 * HEAD_BLOCK]
    q_cat = (q_cat * MLA_SCALE).astype(BF16)
    dq = (proj[:, P_DQ:P_DQ + HEAD_BLOCK] * DIFF_SCALE).astype(BF16)
    dk = proj[:, P_DK:P_DK + HEAD_BLOCK]
    dv = proj[:, P_DV:P_DV + HEAD_BLOCK]
    nq = (proj[:, P_NQ:P_NQ + 2 * HEAD_BLOCK] * NAT_SCALE).astype(BF16)
    nk = proj[:, P_NK:P_NK + 2 * HEAD_BLOCK]
    nv = proj[:, P_NV:P_NV + 2 * HEAD_BLOCK]
    dk_bf, nk_bf = dk.astype(BF16), nk.astype(BF16)
    lam = _lambda(lamv_ref[...], lam_init)
    hmasks = _head_masks()
    mla_qm = _mla_qmasks()

    outs = []
    for j in range(nb):
        r0, r1 = j * S, (j + 1) * S
        ckv_ref[j] = ckv[r0:r1]
        kr_ref[j] = kr_blk[r0:r1, 0:MLA_ROPE]
        for hh in range(DIFF_HEADS):
            dk_ref[j, hh] = dk[r0:r1, hh * 64:(hh + 1) * 64]
            dv_ref[j, hh] = dv[r0:r1, hh * 64:(hh + 1) * 64]
        for hh in range(NAT_HEADS):
            nk_ref[j, hh] = nk[r0:r1, hh * 64:(hh + 1) * 64]
            nv_ref[j, hh] = nv[r0:r1, hh * 64:(hh + 1) * 64]
        tr = lambda v: v.T.astype(BF16)
        ot_mla = _softmax_heads_t(q_cat[r0:r1], k_cat[r0:r1], tr(v_mla[r0:r1]), mla_qm, True)
        ot_diff = _diff_heads_t(dq[r0:r1], dk_bf[r0:r1], tr(dv[r0:r1]), lam, subln_ref[...], lam_init, True)
        ot_nat = [
            _softmax_heads_t(nq[r0:r1, b * HEAD_BLOCK:(b + 1) * HEAD_BLOCK], nk_bf[r0:r1, b * HEAD_BLOCK:(b + 1) * HEAD_BLOCK],
                             tr(nv[r0:r1, b * HEAD_BLOCK:(b + 1) * HEAD_BLOCK]), hmasks, True)
            for b in range(2)
        ]
        outs.append(jnp.concatenate([ot_mla, ot_diff] + ot_nat, axis=0).T.astype(BF16))
    o = jnp.concatenate(outs, axis=0) if nb > 1 else outs[0]
    y_ref[...] = x + mod[2:3] * _dot(o, wout_ref[...])


def _mix_i_call(x, ada5, layer, n_layers, mix_norm, win, qn, wuq, kvn, wukv, lamv, subln, wout, states, nb, lam_init):
    T = x.shape[0]
    B = T // CTX_SEQ
    tm = nb * CTX_SEQ
    lyr = lambda shape: _resident((None,) + shape, lambda i: (layer,) + (0,) * len(shape))
    any_spec = pl.BlockSpec(memory_space=pl.ANY)
    in_specs = [
        pl.BlockSpec((tm, D_MODEL), lambda i: (i, 0)),
        _resident((None, None, None, 3, D_MODEL), lambda i: (layer, 0, 1, 0, 0)),
        lyr((1, D_MODEL)),
        lyr((D_MODEL, P_SPLIT)),
        lyr((D_MODEL, P_COLS - P_SPLIT)),
        lyr((1, MLA_Q_LORA)),
        lyr((MLA_Q_LORA, HEAD_BLOCK + LANES)),
        lyr((1, MLA_KV_LORA)),
        lyr((MLA_KV_LORA, 2 * HEAD_BLOCK)),
        lyr((4, DIFF_DK)),
        lyr((HEAD_BLOCK, 1)),
        lyr((D_MODEL, D_MODEL)),
    ]
    n_fixed = len(in_specs)
    in_specs += [any_spec] * len(states)
    out_specs = [
        pl.BlockSpec((tm, D_MODEL), lambda i: (i, 0)),
        pl.BlockSpec((nb, None, CTX_SEQ, MLA_KV_LORA), lambda i: (i, layer, 0, 0)),
        pl.BlockSpec((nb, None, CTX_SEQ, MLA_ROPE), lambda i: (i, layer, 0, 0)),
        pl.BlockSpec((nb, None, DIFF_HEADS, CTX_SEQ, 2 * DIFF_DK), lambda i: (i, layer, 0, 0, 0)),
        pl.BlockSpec((nb, None, DIFF_HEADS, CTX_SEQ, DIFF_DV), lambda i: (i, layer, 0, 0, 0)),
        pl.BlockSpec((nb, None, NAT_HEADS, CTX_SEQ, NAT_HD), lambda i: (i, layer, 0, 0, 0)),
        pl.BlockSpec((nb, None, NAT_HEADS, CTX_SEQ, NAT_HD), lambda i: (i, layer, 0, 0, 0)),
    ]
    out_shape = [jax.ShapeDtypeStruct((T, D_MODEL), F32)] + _state_shapes(B, n_layers)
    res = pl.pallas_call(
        functools.partial(_mix_i_kernel, nb=nb, lam_init=lam_init),
        grid=(B // nb,),
        in_specs=in_specs,
        out_specs=out_specs,
        out_shape=out_shape,
        input_output_aliases={n_fixed + k: 1 + k for k in range(len(states))},
        compiler_params=pltpu.CompilerParams(
            dimension_semantics=("arbitrary",), vmem_limit_bytes=VMEM_LIMIT_BYTES),
        name="mix_ctx",
    )(x, ada5, mix_norm, *win, qn, wuq, kvn, wukv, lamv, subln, wout, *states)
    return res[0], tuple(res[1:])


PROJ_ROWS = 256
DENSE_QROWS = MXU_DIM
N_KEYS = CTX_SEQ + LAT_SEQ
BIAS_LANES = NAT_WIN_ROWS * LANES


def _mix_ii_kernel(x_ref, mod_ref, g_ref, wina_ref, winb_ref, qn_ref, wuq_ref, kvn_ref, wukv_ref, lamv_ref, subln_ref,
                   wout_ref, cos_ref, snext_ref, sprev_ref, rpb_ref,
                   cckv_ref, ckr_ref, cdk_ref, cdv_ref, cnk_ref, cnv_ref,
                   y_ref,
                   qcat_s, ckv_s, kcat_s, vmla_s, dq_s, dk_s, dv_s, nq_s, nk_s, nv_s, o_s, bias_ref,
                   vmlat_s, dvt_s, *, lam_init):
    S = LAT_SEQ
    C = CTX_SEQ
    mod = mod_ref[...]
    lam = _lambda(lamv_ref[...], lam_init)
    hmasks = _head_masks()
    mla_qm = _mla_qmasks()

    @pl.when(pl.program_id(0) == 0)
    def _():
        qc = lax.broadcasted_iota(jnp.int32, (GRID_W, LANES), 0)
        kc = lax.broadcasted_iota(jnp.int32, (GRID_W, LANES), 1) % GRID_W
        c_start = jnp.clip(qc - NAT_WIN_COLS // 2, 0, GRID_W - NAT_WIN_COLS)
        in_win = (kc >= c_start) & (kc < c_start + NAT_WIN_COLS)
        for par in range(2):
            for h in range(NAT_HEADS):
                for p in range(NAT_WIN_ROWS):
                    row = jnp.broadcast_to(rpb_ref[par, h, p:p + 1, :], (GRID_W, LANES))
                    tile = pltpu.roll(row, LANES - (NAT_WIN_COLS - 1), 1, stride=1, stride_axis=0)
                    bias_ref[par, h // 4, (h % 4) * GRID_W:(h % 4 + 1) * GRID_W, p * LANES:(p + 1) * LANES] = (
                        jnp.where(in_win, tile * LOG2E, NEG_INF))

    ckv_s[0:C, :] = cckv_ref[...]
    kcat_s[0:C, HEAD_BLOCK:] = ckr_ref[...]
    dk_s[0:C, :] = cdk_ref[...]
    dv_s[0:C, :] = cdv_ref[...]

    def proj_body(i, carry):
        r = pl.multiple_of(i * PROJ_ROWS, PROJ_ROWS)
        rows = pl.ds(r, PROJ_ROWS)
        krows = pl.ds(C + r, PROJ_ROWS)
        cos, s_next, s_prev = cos_ref[rows, :], snext_ref[rows, :], sprev_ref[rows, :]
        rope = lambda v: _rope(v, cos, s_next, s_prev)
        x = x_ref[rows, :]
        h = (_rmsnorm(x, g_ref[...]) * (1.0 + mod[1:2]) + mod[0:1]).astype(BF16)
        proj = _Proj(_dot(h, wina_ref[...]), _dot(h, winb_ref[...]))
        q_cat = _dot(_rmsnorm(proj[:, P_CQ:P_CQ + MLA_Q_LORA], qn_ref[...]).astype(BF16), wuq_ref[...])
        qcat_s[rows, 0:HEAD_BLOCK] = (q_cat[:, 0:HEAD_BLOCK] * MLA_SCALE).astype(BF16)
        qcat_s[rows, HEAD_BLOCK:] = (rope(q_cat[:, HEAD_BLOCK:]) * MLA_SCALE).astype(BF16)
        ckv_s[krows, :] = _rmsnorm(proj[:, P_CKV:P_CKV + MLA_KV_LORA], kvn_ref[...]).astype(BF16)
        kcat_s[krows, HEAD_BLOCK:] = _tile32(rope(proj[:, P_KR:P_KR + LANES])).astype(BF16)
        for b in range(2):
            c0 = b * LANES
            dq_s[rows, c0:c0 + LANES] = (rope(proj[:, P_DQ + c0:P_DQ + c0 + LANES]) * DIFF_SCALE).astype(BF16)
            dk_s[krows, c0:c0 + LANES] = rope(proj[:, P_DK + c0:P_DK + c0 + LANES]).astype(BF16)
        dv_s[krows, :] = proj[:, P_DV:P_DV + HEAD_BLOCK].astype(BF16)
        nq_s[rows, :] = (proj[:, P_NQ:P_NQ + 2 * HEAD_BLOCK] * NAT_SCALE).astype(BF16)
        nk_s[rows, :] = proj[:, P_NK:P_NK + 2 * HEAD_BLOCK].astype(BF16)
        nv_s[rows, :] = proj[:, P_NV:P_NV + 2 * HEAD_BLOCK].astype(BF16)
        return carry

    lax.fori_loop(0, S // PROJ_ROWS, proj_body, 0)

    def kv_body(i, carry):
        rows = pl.ds(pl.multiple_of(i * PROJ_ROWS, PROJ_ROWS), PROJ_ROWS)
        kv = _dot(ckv_s[rows, :], wukv_ref[...])
        kcat_s[rows, 0:HEAD_BLOCK] = kv[:, 0:HEAD_BLOCK].astype(BF16)
        vmla_s[rows, :] = kv[:, HEAD_BLOCK:].astype(BF16)
        return carry

    lax.fori_loop(0, N_KEYS // PROJ_ROWS, kv_body, 0)

    for j in range(N_KEYS // PROJ_ROWS):
        c0, c1 = j * PROJ_ROWS, (j + 1) * PROJ_ROWS
        vmlat_s[:, c0:c1] = vmla_s[c0:c1, :].astype(F32).T.astype(BF16)
        dvt_s[:, c0:c1] = dv_s[c0:c1, :].astype(F32).T.astype(BF16)

    def dense_body(i, carry):
        rows = pl.ds(pl.multiple_of(i * DENSE_QROWS, DENSE_QROWS), DENSE_QROWS)
        ot_mla = _softmax_heads_t(qcat_s[rows, :], kcat_s[...], vmlat_s[...], mla_qm, False)
        o_s[rows, 0:HEAD_BLOCK] = ot_mla.T.astype(BF16)
        ot_diff = _diff_heads_t(dq_s[rows, :], dk_s[...], dvt_s[...], lam, subln_ref[...], lam_init, False)
        o_s[rows, HEAD_BLOCK:2 * HEAD_BLOCK] = ot_diff.T.astype(BF16)
        return carry

    lax.fori_loop(0, S // DENSE_QROWS, dense_body, 0)

    for r in range(LAT_ROWS):
        rs = min(max(r - NAT_WIN_ROWS // 2, 0), LAT_ROWS - NAT_WIN_ROWS)
        dr0 = rs - r + (NAT_WIN_ROWS - 1)
        par = dr0 % 2
        off = GRID_W * (dr0 - par)
        q0, q1 = r * GRID_W, (r + 1) * GRID_W
        k0, k1 = rs * GRID_W, (rs + NAT_WIN_ROWS) * GRID_W
        for b in range(2):
            c0, c1 = b * HEAD_BLOCK, (b + 1) * HEAD_BLOCK
            q = nq_s[q0:q1, c0:c1]
            zero = jnp.zeros_like(q)
            qs = jnp.concatenate([jnp.where(m, q, zero) for m in hmasks], axis=0)
            sw = _dot_t(qs, nk_s[k0:k1, c0:c1]) + bias_ref[par, b, :, off:off + NAT_WIN_ROWS * GRID_W]
            sc = _dot_t(qs, cnk_ref[:, c0:c1])
            m = jnp.maximum(jnp.max(sw, axis=-1, keepdims=True), jnp.max(sc, axis=-1, keepdims=True))
            ew = jnp.exp2(sw - m)
            ec = jnp.exp2(sc - m)
            l = jnp.sum(ew, axis=-1, keepdims=True) + jnp.sum(ec, axis=-1, keepdims=True)
            o = (_dot(ew.astype(BF16), nv_s[k0:k1, c0:c1]) + _dot(ec.astype(BF16), cnv_ref[:, c0:c1])) / l
            of = None
            for hh, hm in enumerate(hmasks):
                part = jnp.where(hm, o[hh * GRID_W:(hh + 1) * GRID_W], 0.0)
                of = part if of is None else of + part
            o_s[q0:q1, 2 * HEAD_BLOCK + c0:2 * HEAD_BLOCK + c1] = of.astype(BF16)

    def out_body(i, carry):
        rows = pl.ds(pl.multiple_of(i * PROJ_ROWS, PROJ_ROWS), PROJ_ROWS)
        y_ref[rows, :] = x_ref[rows, :] + mod[2:3] * _dot(o_s[rows, :], wout_ref[...])
        return carry

    lax.fori_loop(0, S // PROJ_ROWS, out_body, 0)


def _mix_ii_call(x, ada5, layer, mix_norm, win, qn, wuq, kvn, wukv, lamv, subln, wout, rope_tabs, rpb_rows,
                 caches, lam_init):
    T = x.shape[0]
    Bd = T // LAT_SEQ
    cckv, ckr, cdk, cdv, cnk, cnv = caches
    lyr = lambda shape: _resident((None,) + shape, lambda i: (layer,) + (0,) * len(shape))
    cache = lambda width: pl.BlockSpec((None, None, CTX_SEQ, width), lambda i: (i, layer, 0, 0))
    tab = _resident((LAT_SEQ, LANES), lambda i: (0, 0))
    in_specs = [
        pl.BlockSpec((LAT_SEQ, D_MODEL), lambda i: (i, 0), pipeline_mode=pl.Buffered(1)),
        pl.BlockSpec((None, None, None, 3, D_MODEL), lambda i: (layer, i + 1, 1, 0, 0)),
        lyr((1, D_MODEL)),
        lyr((D_MODEL, P_SPLIT)),
        lyr((D_MODEL, P_COLS - P_SPLIT)),
        lyr((1, MLA_Q_LORA)),
        lyr((MLA_Q_LORA, HEAD_BLOCK + LANES)),
        lyr((1, MLA_KV_LORA)),
        lyr((MLA_KV_LORA, 2 * HEAD_BLOCK)),
        lyr((4, DIFF_DK)),
        lyr((HEAD_BLOCK, 1)),
        lyr((D_MODEL, D_MODEL)),
        tab, tab, tab,
        lyr((2, NAT_HEADS, NAT_WIN_ROWS, LANES)),
        cache(MLA_KV_LORA), cache(LANES), cache(HEAD_BLOCK), cache(HEAD_BLOCK),
        cache(2 * HEAD_BLOCK), cache(2 * HEAD_BLOCK),
    ]
    scratch = [
        pltpu.VMEM((LAT_SEQ, HEAD_BLOCK + LANES), BF16),
        pltpu.VMEM((N_KEYS, MLA_KV_LORA), BF16),
        pltpu.VMEM((N_KEYS, HEAD_BLOCK + LANES), BF16),
        pltpu.VMEM((N_KEYS, HEAD_BLOCK), BF16),
        pltpu.VMEM((LAT_SEQ, HEAD_BLOCK), BF16),
        pltpu.VMEM((N_KEYS, HEAD_BLOCK), BF16),
        pltpu.VMEM((N_KEYS, HEAD_BLOCK), BF16),
        pltpu.VMEM((LAT_SEQ, 2 * HEAD_BLOCK), BF16),
        pltpu.VMEM((LAT_SEQ, 2 * HEAD_BLOCK), BF16),
        pltpu.VMEM((LAT_SEQ, 2 * HEAD_BLOCK), BF16),
        pltpu.VMEM((LAT_SEQ, D_MODEL), BF16),
        pltpu.VMEM((2, 2, HEAD_BLOCK, BIAS_LANES), F32),
        pltpu.VMEM((HEAD_BLOCK, N_KEYS), BF16),
        pltpu.VMEM((HEAD_BLOCK, N_KEYS), BF16),
    ]
    return pl.pallas_call(
        functools.partial(_mix_ii_kernel, lam_init=lam_init),
        grid=(Bd,),
        in_specs=in_specs,
        out_specs=pl.BlockSpec((LAT_SEQ, D_MODEL), lambda i: (i, 0)),
        out_shape=jax.ShapeDtypeStruct((T, D_MODEL), F32),
        scratch_shapes=scratch,
        compiler_params=pltpu.CompilerParams(
            dimension_semantics=("arbitrary",), vmem_limit_bytes=VMEM_LIMIT_BYTES),
        name="mix_lat",
    )(x, ada5, mix_norm, *win, qn, wuq, kvn, wukv, lamv, subln, wout, *rope_tabs, rpb_rows, *caches)


def _rpb_rows(rpb):
    L = rpb.shape[0]
    n_dr, n_dc = 2 * NAT_WIN_ROWS - 1, 2 * NAT_WIN_COLS - 1
    padded = jnp.pad(rpb.astype(F32), ((0, 0), (0, 0), (0, 2 * NAT_WIN_ROWS + 1 - n_dr), (0, GRID_W - n_dc)))
    even = padded[:, :, 0:2 * NAT_WIN_ROWS].reshape(L, NAT_HEADS, NAT_WIN_ROWS, LANES)
    odd = padded[:, :, 1:2 * NAT_WIN_ROWS + 1].reshape(L, NAT_HEADS, NAT_WIN_ROWS, LANES)
    return jnp.stack([even, odd], axis=1)


def _heads_to_lanes(cache):
    B, L, H, S, d = cache.shape
    return jnp.transpose(cache, (0, 1, 3, 2, 4)).reshape(B, L, S, H * d).astype(BF16)


def kernel(x_prompt, x_sample, cache_mla_ckv, cache_mla_krope, cache_diff_k, cache_diff_v, cache_nat_k, cache_nat_v, c, c_ctx, w_ada, b_ada, ffn1_norm, ffn1_w_gate, ffn1_w_up, ffn1_w_down, mix_norm, w_in, mla_q_norm, mla_w_uq, mla_kv_norm, mla_w_ukv, diff_lambda_q1, diff_lambda_k1, diff_lambda_q2, diff_lambda_k2, diff_subln, nat_rpb, w_out, ffn2_norm, ffn2_w_gate, ffn2_w_up, ffn2_w_down, final_norm):
    L = w_ada.shape[0]
    B, S_ctx, _ = x_prompt.shape
    Bd, S_lat, _ = x_sample.shape
    assert S_ctx == CTX_SEQ and S_lat == LAT_SEQ and 1 + Bd <= ADA_ROWS
    assert cache_mla_ckv.shape[2] == CTX_SEQ and (B * CTX_SEQ) % FFN_TILE == 0

    cvec = jnp.concatenate([c_ctx[None, :], c, jnp.zeros((ADA_ROWS - 1 - Bd, D_MODEL), F32)], axis=0)
    ada5 = _ada_call(cvec, w_ada, b_ada).reshape(L, ADA_ROWS, 3, 3, D_MODEL)

    bf = lambda w: w.astype(BF16)
    split = MLA_Q_LORA + MLA_KV_LORA + MLA_ROPE
    win = (bf(jnp.concatenate([w_in[:, :, :split], jnp.zeros((L, D_MODEL, LANES - MLA_ROPE), F32)], axis=2)),
           bf(w_in[:, :, split:]))
    wuq4 = mla_w_uq.reshape(L, MLA_Q_LORA, MLA_HEADS, MLA_NOPE + MLA_ROPE)
    wuq = bf(jnp.concatenate([wuq4[..., :MLA_NOPE].reshape(L, MLA_Q_LORA, -1),
                              wuq4[..., MLA_NOPE:].reshape(L, MLA_Q_LORA, -1)], axis=2))
    wukv4 = mla_w_ukv.reshape(L, MLA_KV_LORA, MLA_HEADS, MLA_NOPE + MLA_V)
    wukv = bf(jnp.concatenate([wukv4[..., :MLA_NOPE].reshape(L, MLA_KV_LORA, -1),
                               wukv4[..., MLA_NOPE:].reshape(L, MLA_KV_LORA, -1)], axis=2))
    wout = bf(w_out)
    lamv = jnp.stack([diff_lambda_q1, diff_lambda_k1, diff_lambda_q2, diff_lambda_k2], axis=1)
    subln = jnp.tile(diff_subln, (1, DIFF_HEADS)).reshape(L, HEAD_BLOCK, 1)
    r3 = lambda a: a.reshape(L, 1, a.shape[-1])
    n1, n2, nm, qn, kvn = r3(ffn1_norm), r3(ffn2_norm), r3(mix_norm), r3(mla_q_norm), r3(mla_kv_norm)
    fnorm = final_norm.reshape(1, D_MODEL)

    caches = (bf(cache_mla_ckv), bf(jnp.tile(cache_mla_krope, (1, 1, 1, LANES // MLA_ROPE))),
              _heads_to_lanes(cache_diff_k), _heads_to_lanes(cache_diff_v),
              _heads_to_lanes(cache_nat_k), _heads_to_lanes(cache_nat_v))
    rope_tabs = (jnp.asarray(_ROPE_COS), jnp.asarray(_ROPE_SNEXT), jnp.asarray(_ROPE_SPREV))
    rpb_rows = _rpb_rows(nat_rpb)

    xi = x_prompt.reshape(B * CTX_SEQ, D_MODEL)
    xs = x_sample.reshape(Bd * LAT_SEQ, D_MODEL)
    nb = FFN_TILE // CTX_SEQ
    states = None
    for l in range(L):
        lam_init = 0.8 - 0.6 * math.exp(-0.3 * l)
        last = l == L - 1
        xi, xs, *st = _ffn_call(xi, xs, ada5, l, 0, n1, ffn1_w_gate, ffn1_w_up, ffn1_w_down, None, "ffn1",
                                fill_layers=L if l == 0 else 0)
        states = tuple(st) if l == 0 else states
        xi, states = _mix_i_call(xi, ada5, l, L, nm, win, qn, wuq, kvn, wukv, lamv, subln, wout, states, nb, lam_init)
        xs = _mix_ii_call(xs, ada5, l, nm, win, qn, wuq, kvn, wukv, lamv, subln, wout, rope_tabs,
                          rpb_rows, caches, lam_init)
        xi, xs = _ffn_call(xi, xs, ada5, l, 2, n2, ffn2_w_gate, ffn2_w_up, ffn2_w_down,
                           fnorm if last else None, "ffn2")

    return (xi.reshape(B, CTX_SEQ, D_MODEL), xs.reshape(Bd, LAT_SEQ, D_MODEL)) + tuple(states)
```

```python
import functools
import math

import numpy as np
import jax
import jax.numpy as jnp
from jax import lax
from jax.experimental import pallas as pl
from jax.experimental.pallas import tpu as pltpu

F32 = jnp.float32
BF16 = jnp.bfloat16

D_MODEL = 1024
FFN_DIM = 2816
NORM_EPS = 1e-6
ROPE_THETA = 10000.0
GRID_W = 64
N_ADA = 9
NEG_INF = -1e30

MLA_HEADS = 4
MLA_Q_LORA = 256
MLA_KV_LORA = 128
MLA_NOPE = 64
MLA_ROPE = 32
MLA_V = 64
DIFF_HEADS = 4
DIFF_DK = 32
DIFF_DV = 64
NAT_HEADS = 8
NAT_HD = 64
NAT_WIN_ROWS = 8
NAT_WIN_COLS = 16

LOG2E = math.log2(math.e)
MLA_SCALE = (MLA_NOPE + MLA_ROPE) ** -0.5 * LOG2E
DIFF_SCALE = DIFF_DK ** -0.5 * LOG2E
NAT_SCALE = NAT_HD ** -0.5 * LOG2E

CTX_SEQ = 256
LAT_SEQ = 1024
LAT_ROWS = LAT_SEQ // GRID_W

LANES = 128
MXU_DIM = 256
VMEM_LIMIT_BYTES = 58 * 1024 * 1024

P_CQ = 0
P_CKV = 256
P_KR = 384
P_DQ = 512
P_DK = 768
P_DV = 1024
P_NQ = 1280
P_NK = 1792
P_NV = 2304
P_COLS = 2816
P_SPLIT = 512
HEAD_BLOCK = 256
ADA_ROWS = 8

FFN_CHUNKS = ((0, 512), (512, 1024), (1024, 1536), (1536, 2048), (2048, 2560), (2560, 2816))
FFN_TILE = 512
N_STATES = 6


def _rope_tables():
    t = np.arange(LAT_SEQ)
    pos = np.stack([t // GRID_W, t % GRID_W], axis=0).astype(np.float64)
    lane = np.arange(LANES)
    p = lane % 32
    axis = (p >= 16).astype(np.int64)
    freqs = ROPE_THETA ** (-(p % 8).astype(np.float64) / 8.0)
    ang = pos[axis, :].T * freqs[None, :]
    first = (p % 16) < 8
    cos = np.cos(ang)
    sin = np.sin(ang)
    s_next = np.where(first[None, :], -sin, 0.0)
    s_prev = np.where(first[None, :], 0.0, sin)
    return cos.astype(np.float32), s_next.astype(np.float32), s_prev.astype(np.float32)


_ROPE_COS, _ROPE_SNEXT, _ROPE_SPREV = _rope_tables()


def _rmsnorm(x, g):
    ms = jnp.mean(x * x, axis=-1, keepdims=True)
    return x * lax.rsqrt(ms + NORM_EPS) * g


def _silu(x):
    return x / (1.0 + jnp.exp(-x))


def _dot(a, b):
    return jnp.dot(a, b, preferred_element_type=F32)


def _dot_t(a, b):
    return lax.dot_general(a, b, (((1,), (1,)), ((), ())), preferred_element_type=F32)


class _Proj:
    def __init__(self, a, b):
        self.a, self.b = a, b

    def __getitem__(self, idx):
        lo, hi = idx[1].start, idx[1].stop
        return self.a[:, lo:hi] if hi <= P_SPLIT else self.b[:, lo - P_SPLIT:hi - P_SPLIT]


def _lane_mask(width, lo, hi):
    lane = lax.broadcasted_iota(jnp.int32, (1, width), 1)
    return (lane >= lo) & (lane < hi)


def _head_masks(width=HEAD_BLOCK, group=64, n=4):
    return [_lane_mask(width, h * group, (h + 1) * group) for h in range(n)]


def _mla_qmasks():
    lane = lax.broadcasted_iota(jnp.int32, (1, HEAD_BLOCK + LANES), 1)
    out = []
    for h in range(MLA_HEADS):
        nope = (lane >= h * MLA_NOPE) & (lane < (h + 1) * MLA_NOPE)
        rope = (lane >= HEAD_BLOCK + h * MLA_ROPE) & (lane < HEAD_BLOCK + (h + 1) * MLA_ROPE)
        out.append(nope | rope)
    return out


def _tile32(blk):
    return blk + pltpu.roll(blk, 32, 1) + pltpu.roll(blk, 64, 1) + pltpu.roll(blk, 96, 1)


def _rope(x, cos, s_next, s_prev):
    return x * cos + pltpu.roll(x, LANES - 8, 1) * s_next + pltpu.roll(x, 8, 1) * s_prev


def _stack_masked(q, masks):
    zero = jnp.zeros_like(q)
    return jnp.concatenate([jnp.where(m, q, zero) for m in masks], axis=0)


def _scores_t(q, k_bf, kmasks, stacked):
    sk = k_bf.shape[0]
    if stacked:
        s = _dot_t(_stack_masked(k_bf, kmasks), q)
        return [s[h * sk:(h + 1) * sk] for h in range(len(kmasks))]
    zero = jnp.zeros_like(k_bf)
    return [_dot_t(jnp.where(m, k_bf, zero), q) for m in kmasks]


def _pv_t(vt_bf, p_bf, h, short_keys):
    if short_keys:
        return _dot(vt_bf, p_bf)[h * 64:(h + 1) * 64]
    return _dot(vt_bf[h * 64:(h + 1) * 64], p_bf)


def _exp_sum_t(s):
    e = jnp.exp2(s - jnp.max(s, axis=0, keepdims=True))
    return e, jnp.sum(e, axis=0, keepdims=True)


def _softmax_heads_t(q, k_bf, vt_bf, kmasks, stacked):
    parts = []
    for h, s in enumerate(_scores_t(q, k_bf, kmasks, stacked)):
        e, l = _exp_sum_t(s)
        parts.append(_pv_t(vt_bf, e.astype(BF16), h, stacked) / l)
    return jnp.concatenate(parts, axis=0)


def _diff_heads_t(q, k_bf, vt_bf, lam, subln_col, lam_init, stacked):
    n = DIFF_HEADS
    m1 = [_lane_mask(HEAD_BLOCK, h * 2 * DIFF_DK, h * 2 * DIFF_DK + DIFF_DK) for h in range(n)]
    m2 = [_lane_mask(HEAD_BLOCK, h * 2 * DIFF_DK + DIFF_DK, (h + 1) * 2 * DIFF_DK) for h in range(n)]
    s = _scores_t(q, k_bf, m1 + m2, stacked)
    parts = []
    for h in range(n):
        e1, l1 = _exp_sum_t(s[h])
        e2, l2 = _exp_sum_t(s[n + h])
        p = e1 * (1.0 / l1) - e2 * (lam / l2)
        oh = _pv_t(vt_bf, p.astype(BF16), h, stacked)
        ms = jnp.mean(oh * oh, axis=0, keepdims=True)
        parts.append(oh * lax.rsqrt(ms + NORM_EPS))
    return jnp.concatenate(parts, axis=0) * subln_col * (1.0 - lam_init)


def _lambda(lamv, lam_init):
    a = jnp.sum(lamv[0:1] * lamv[1:2], axis=-1, keepdims=True)
    b = jnp.sum(lamv[2:3] * lamv[3:4], axis=-1, keepdims=True)
    return jnp.exp(a) - jnp.exp(b) + lam_init


def _ada_kernel(c_ref, w_ref, b_ref, o_ref):
    s = _silu(c_ref[...]).astype(BF16)
    o_ref[...] = _dot(s, w_ref[...].astype(BF16)) + b_ref[...]


def _ada_call(cvec, w_ada, b_ada):
    L, _, n = w_ada.shape
    tn = 1024
    return pl.pallas_call(
        _ada_kernel,
        grid=(L, n // tn),
        in_specs=[
            pl.BlockSpec((ADA_ROWS, D_MODEL), lambda l, j: (0, 0)),
            pl.BlockSpec((None, D_MODEL, tn), lambda l, j: (l, 0, j)),
            pl.BlockSpec((None, 1, tn), lambda l, j: (l, 0, j)),
        ],
        out_specs=pl.BlockSpec((None, ADA_ROWS, tn), lambda l, j: (l, 0, j)),
        out_shape=jax.ShapeDtypeStruct((L, ADA_ROWS, n), F32),
        compiler_params=pltpu.CompilerParams(
            dimension_semantics=("arbitrary", "arbitrary"), vmem_limit_bytes=VMEM_LIMIT_BYTES),
        name="ada",
    )(cvec, w_ada, b_ada.reshape(L, 1, n))


def _ffn_tile(x_ref, o_ref, mod, g_ref, wg_ref, wu_ref, wd_ref, fn_ref):
    x = x_ref[...]
    h = (_rmsnorm(x, g_ref[...]) * (1.0 + mod[1:2]) + mod[0:1]).astype(BF16)
    acc = None
    for lo, hi in FFN_CHUNKS:
        g = _dot(h, wg_ref[:, lo:hi].astype(BF16))
        u = _dot(h, wu_ref[:, lo:hi].astype(BF16))
        part = _dot((_silu(g) * u).astype(BF16), wd_ref[lo:hi, :].astype(BF16))
        acc = part if acc is None else acc + part
    y = x + (0.5 * mod[2:3]) * acc
    if fn_ref is not None:
        y = _rmsnorm(y, fn_ref[...])
    o_ref[...] = y


def _state_fill_copies(t, state_refs, zero_refs, sem, n_layers):
    per_tile = FFN_TILE // CTX_SEQ
    copies = []
    for j in range(per_tile):
        for l in range(n_layers):
            for st, z in zip(state_refs, zero_refs):
                dst = st.at[t * per_tile + j, l]
                if dst.shape == z.shape:
                    copies.append(pltpu.make_async_copy(z, dst, sem))
                else:
                    nh = z.shape[0]
                    copies += [pltpu.make_async_copy(z, dst.at[h:h + nh], sem) for h in range(0, dst.shape[0], nh)]
    return copies


def _ffn_kernel(xc_ref, xl_ref, mod_ref, g_ref, wg_ref, wu_ref, wd_ref, *rest, n_ctx_tiles, final, fill_layers):
    fn_ref = rest[0] if final else None
    oc_ref, ol_ref = rest[final:final + 2]
    mod = mod_ref[...]
    t = pl.program_id(0)
    if fill_layers:
        state_refs = rest[final + 2:final + 2 + N_STATES]
        z_ckv, z_kr, z_heads, zsem = rest[final + 2 + N_STATES:]

        @pl.when(t == 0)
        def _():
            for z in (z_ckv, z_kr, z_heads):
                z[...] = jnp.zeros(z.shape, z.dtype)

    @pl.when(t < n_ctx_tiles)
    def _():
        fill = []
        if fill_layers:
            fill = _state_fill_copies(t, state_refs, (z_ckv, z_kr, z_heads, z_heads, z_heads, z_heads), zsem,
                                      fill_layers)
        for c in fill:
            c.start()
        _ffn_tile(xc_ref, oc_ref, mod, g_ref, wg_ref, wu_ref, wd_ref, fn_ref)
        for c in fill:
            c.wait()

    @pl.when(t >= n_ctx_tiles)
    def _():
        _ffn_tile(xl_ref, ol_ref, mod, g_ref, wg_ref, wu_ref, wd_ref, fn_ref)


def _resident(shape, index_map):
    return pl.BlockSpec(shape, index_map, pipeline_mode=pl.Buffered(1))


def _state_shapes(B, L):
    return [jax.ShapeDtypeStruct(s, F32) for s in (
        (B, L, CTX_SEQ, MLA_KV_LORA), (B, L, CTX_SEQ, MLA_ROPE),
        (B, L, DIFF_HEADS, CTX_SEQ, 2 * DIFF_DK), (B, L, DIFF_HEADS, CTX_SEQ, DIFF_DV),
        (B, L, NAT_HEADS, CTX_SEQ, NAT_HD), (B, L, NAT_HEADS, CTX_SEQ, NAT_HD))]


def _ffn_call(xc, xl, ada5, layer, group, norm, wg, wu, wd, final_norm, name, fill_layers=0):
    tm = FFN_TILE
    n_ctx = xc.shape[0] // tm
    n_lat = xl.shape[0] // tm
    per_lat = LAT_SEQ // tm
    ctx_blk = lambda t: (jnp.minimum(t, n_ctx - 1), 0)
    lat_blk = lambda t: (jnp.maximum(t - n_ctx, 0), 0)
    ada_row = lambda t: jnp.maximum(t - n_ctx, -per_lat) // per_lat + 1
    in_specs = [
        pl.BlockSpec((tm, D_MODEL), ctx_blk),
        pl.BlockSpec((tm, D_MODEL), lat_blk),
        pl.BlockSpec((None, None, None, 3, D_MODEL), lambda t: (layer, ada_row(t), group, 0, 0)),
        _resident((None, 1, D_MODEL), lambda t: (layer, 0, 0)),
        _resident((None, D_MODEL, FFN_DIM), lambda t: (layer, 0, 0)),
        _resident((None, D_MODEL, FFN_DIM), lambda t: (layer, 0, 0)),
        _resident((None, FFN_DIM, D_MODEL), lambda t: (layer, 0, 0)),
    ]
    args = [xc, xl, ada5, norm, wg, wu, wd]
    if final_norm is not None:
        in_specs.append(_resident((1, D_MODEL), lambda t: (0, 0)))
        args.append(final_norm)
    out_specs = [pl.BlockSpec((tm, D_MODEL), ctx_blk), pl.BlockSpec((tm, D_MODEL), lat_blk)]
    out_shape = [jax.ShapeDtypeStruct(xc.shape, F32), jax.ShapeDtypeStruct(xl.shape, F32)]
    scratch = []
    if fill_layers:
        states = _state_shapes(xc.shape[0] // CTX_SEQ, fill_layers)
        out_specs += [pl.BlockSpec(memory_space=pl.ANY)] * N_STATES
        out_shape += states
        scratch += [pltpu.VMEM(states[k].shape[2:], F32) for k in (0, 1, 2)] + [pltpu.SemaphoreType.DMA(())]
    return pl.pallas_call(
        functools.partial(_ffn_kernel, n_ctx_tiles=n_ctx, final=final_norm is not None, fill_layers=fill_layers),
        grid=(n_ctx + n_lat,),
        in_specs=in_specs,
        out_specs=out_specs,
        out_shape=out_shape,
        scratch_shapes=scratch,
        compiler_params=pltpu.CompilerParams(
            dimension_semantics=("arbitrary",), vmem_limit_bytes=VMEM_LIMIT_BYTES),
        name=name,
    )(*args)


def _mix_i_kernel(x_ref, mod_ref, g_ref, wina_ref, winb_ref, qn_ref, wuq_ref, kvn_ref, wukv_ref, lamv_ref, subln_ref,
                  wout_ref, *refs, nb, lam_init):
    y_ref, ckv_ref, kr_ref, dk_ref, dv_ref, nk_ref, nv_ref = refs[-7:]
    S = CTX_SEQ
    x = x_ref[...]
    mod = mod_ref[...]
    h = (_rmsnorm(x, g_ref[...]) * (1.0 + mod[1:2]) + mod[0:1]).astype(BF16)
    proj = _Proj(_dot(h, wina_ref[...]), _dot(h, winb_ref[...]))
    q_cat = _dot(_rmsnorm(proj[:, P_CQ:P_CQ + MLA_Q_LORA], qn_ref[...]).astype(BF16), wuq_ref[...])
    ckv = _rmsnorm(proj[:, P_CKV:P_CKV + MLA_KV_LORA], kvn_ref[...])
    kv = _dot(ckv.astype(BF16), wukv_ref[...])
    kr_blk = proj[:, P_KR:P_KR + LANES]
    k_cat = jnp.concatenate([kv[:, 0:HEAD_BLOCK], _tile32(kr_blk)], axis=1).astype(BF16)
    v_mla = kv[:, HEAD_BLOCK:2 * HEAD_BLOCK]
    q_cat = (q_cat * MLA_SCALE).astype(BF16)
    dq = (proj[:, P_DQ:P_DQ + HEAD_BLOCK] * DIFF_SCALE).astype(BF16)
    dk = proj[:, P_DK:P_DK + HEAD_BLOCK]
    dv = proj[:, P_DV:P_DV + HEAD_BLOCK]
    nq = (proj[:, P_NQ:P_NQ + 2 * HEAD_BLOCK] * NAT_SCALE).astype(BF16)
    nk = proj[:, P_NK:P_NK + 2 * HEAD_BLOCK]
    nv = proj[:, P_NV:P_NV + 2 * HEAD_BLOCK]
    dk_bf, nk_bf = dk.astype(BF16), nk.astype(BF16)
    lam = _lambda(lamv_ref[...], lam_init)
    hmasks = _head_masks()
    mla_qm = _mla_qmasks()

    outs = []
    for j in range(nb):
        r0, r1 = j * S, (j + 1) * S
        ckv_ref[j] = ckv[r0:r1]
        kr_ref[j] = kr_blk[r0:r1, 0:MLA_ROPE]
        for hh in range(DIFF_HEADS):
            dk_ref[j, hh] = dk[r0:r1, hh * 64:(hh + 1) * 64]
            dv_ref[j, hh] = dv[r0:r1, hh * 64:(hh + 1) * 64]
        for hh in range(NAT_HEADS):
            nk_ref[j, hh] = nk[r0:r1, hh * 64:(hh + 1) * 64]
            nv_ref[j, hh] = nv[r0:r1, hh * 64:(hh + 1) * 64]
        tr = lambda v: v.T.astype(BF16)
        ot_mla = _softmax_heads_t(q_cat[r0:r1], k_cat[r0:r1], tr(v_mla[r0:r1]), mla_qm, True)
        ot_diff = _diff_heads_t(dq[r0:r1], dk_bf[r0:r1], tr(dv[r0:r1]), lam, subln_ref[...], lam_init, True)
        ot_nat = [
            _softmax_heads_t(nq[r0:r1, b * HEAD_BLOCK:(b + 1) * HEAD_BLOCK], nk_bf[r0:r1, b * HEAD_BLOCK:(b + 1) * HEAD_BLOCK],
                             tr(nv[r0:r1, b * HEAD_BLOCK:(b + 1) * HEAD_BLOCK]), hmasks, True)
            for b in range(2)
        ]
        outs.append(jnp.concatenate([ot_mla, ot_diff] + ot_nat, axis=0).T.astype(BF16))
    o = jnp.concatenate(outs, axis=0) if nb > 1 else outs[0]
    y_ref[...] = x + mod[2:3] * _dot(o, wout_ref[...])


def _mix_i_call(x, ada5, layer, n_layers, mix_norm, win, qn, wuq, kvn, wukv, lamv, subln, wout, states, nb, lam_init):
    T = x.shape[0]
    B = T // CTX_SEQ
    tm = nb * CTX_SEQ
    lyr = lambda shape: _resident((None,) + shape, lambda i: (layer,) + (0,) * len(shape))
    any_spec = pl.BlockSpec(memory_space=pl.ANY)
    in_specs = [
        pl.BlockSpec((tm, D_MODEL), lambda i: (i, 0)),
        _resident((None, None, None, 3, D_MODEL), lambda i: (layer, 0, 1, 0, 0)),
        lyr((1, D_MODEL)),
        lyr((D_MODEL, P_SPLIT)),
        lyr((D_MODEL, P_COLS - P_SPLIT)),
        lyr((1, MLA_Q_LORA)),
        lyr((MLA_Q_LORA, HEAD_BLOCK + LANES)),
        lyr((1, MLA_KV_LORA)),
        lyr((MLA_KV_LORA, 2 * HEAD_BLOCK)),
        lyr((4, DIFF_DK)),
        lyr((HEAD_BLOCK, 1)),
        lyr((D_MODEL, D_MODEL)),
    ]
    n_fixed = len(in_specs)
    in_specs += [any_spec] * len(states)
    out_specs = [
        pl.BlockSpec((tm, D_MODEL), lambda i: (i, 0)),
        pl.BlockSpec((nb, None, CTX_SEQ, MLA_KV_LORA), lambda i: (i, layer, 0, 0)),
        pl.BlockSpec((nb, None, CTX_SEQ, MLA_ROPE), lambda i: (i, layer, 0, 0)),
        pl.BlockSpec((nb, None, DIFF_HEADS, CTX_SEQ, 2 * DIFF_DK), lambda i: (i, layer, 0, 0, 0)),
        pl.BlockSpec((nb, None, DIFF_HEADS, CTX_SEQ, DIFF_DV), lambda i: (i, layer, 0, 0, 0)),
        pl.BlockSpec((nb, None, NAT_HEADS, CTX_SEQ, NAT_HD), lambda i: (i, layer, 0, 0, 0)),
        pl.BlockSpec((nb, None, NAT_HEADS, CTX_SEQ, NAT_HD), lambda i: (i, layer, 0, 0, 0)),
    ]
    out_shape = [jax.ShapeDtypeStruct((T, D_MODEL), F32)] + _state_shapes(B, n_layers)
    res = pl.pallas_call(
        functools.partial(_mix_i_kernel, nb=nb, lam_init=lam_init),
        grid=(B // nb,),
        in_specs=in_specs,
        out_specs=out_specs,
        out_shape=out_shape,
        input_output_aliases={n_fixed + k: 1 + k for k in range(len(states))},
        compiler_params=pltpu.CompilerParams(
            dimension_semantics=("arbitrary",), vmem_limit_bytes=VMEM_LIMIT_BYTES),
        name="mix_ctx",
    )(x, ada5, mix_norm, *win, qn, wuq, kvn, wukv, lamv, subln, wout, *states)
    return res[0], tuple(res[1:])


PROJ_ROWS = 256
DENSE_QROWS = MXU_DIM
N_KEYS = CTX_SEQ + LAT_SEQ
BIAS_LANES = NAT_WIN_ROWS * LANES


def _mix_ii_kernel(x_ref, mod_ref, g_ref, wina_ref, winb_ref, qn_ref, wuq_ref, kvn_ref, wukv_ref, lamv_ref, subln_ref,
                   wout_ref, cos_ref, snext_ref, sprev_ref, rpb_ref,
                   cckv_ref, ckr_ref, cdk_ref, cdv_ref, cnk_ref, cnv_ref,
                   y_ref,
                   qcat_s, ckv_s, kcat_s, vmla_s, dq_s, dk_s, dv_s, nq_s, nk_s, nv_s, o_s, bias_ref,
                   vmlat_s, dvt_s, *, lam_init):
    S = LAT_SEQ
    C = CTX_SEQ
    mod = mod_ref[...]
    lam = _lambda(lamv_ref[...], lam_init)
    hmasks = _head_masks()
    mla_qm = _mla_qmasks()

    @pl.when(pl.program_id(0) == 0)
    def _():
        qc = lax.broadcasted_iota(jnp.int32, (GRID_W, LANES), 0)
        kc = lax.broadcasted_iota(jnp.int32, (GRID_W, LANES), 1) % GRID_W
        c_start = jnp.clip(qc - NAT_WIN_COLS // 2, 0, GRID_W - NAT_WIN_COLS)
        in_win = (kc >= c_start) & (kc < c_start + NAT_WIN_COLS)
        for par in range(2):
            for h in range(NAT_HEADS):
                for p in range(NAT_WIN_ROWS):
                    row = jnp.broadcast_to(rpb_ref[par, h, p:p + 1, :], (GRID_W, LANES))
                    tile = pltpu.roll(row, LANES - (NAT_WIN_COLS - 1), 1, stride=1, stride_axis=0)
                    bias_ref[par, h // 4, (h % 4) * GRID_W:(h % 4 + 1) * GRID_W, p * LANES:(p + 1) * LANES] = (
                        jnp.where(in_win, tile * LOG2E, NEG_INF))

    ckv_s[0:C, :] = cckv_ref[...]
    kcat_s[0:C, HEAD_BLOCK:] = ckr_ref[...]
    dk_s[0:C, :] = cdk_ref[...]
    dv_s[0:C, :] = cdv_ref[...]

    def proj_body(i, carry):
        r = pl.multiple_of(i * PROJ_ROWS, PROJ_ROWS)
        rows = pl.ds(r, PROJ_ROWS)
        krows = pl.ds(C + r, PROJ_ROWS)
        cos, s_next, s_prev = cos_ref[rows, :], snext_ref[rows, :], sprev_ref[rows, :]
        rope = lambda v: _rope(v, cos, s_next, s_prev)
        x = x_ref[rows, :]
        h = (_rmsnorm(x, g_ref[...]) * (1.0 + mod[1:2]) + mod[0:1]).astype(BF16)
        proj = _Proj(_dot(h, wina_ref[...]), _dot(h, winb_ref[...]))
        q_cat = _dot(_rmsnorm(proj[:, P_CQ:P_CQ + MLA_Q_LORA], qn_ref[...]).astype(BF16), wuq_ref[...])
        qcat_s[rows, 0:HEAD_BLOCK] = (q_cat[:, 0:HEAD_BLOCK] * MLA_SCALE).astype(BF16)
        qcat_s[rows, HEAD_BLOCK:] = (rope(q_cat[:, HEAD_BLOCK:]) * MLA_SCALE).astype(BF16)
        ckv_s[krows, :] = _rmsnorm(proj[:, P_CKV:P_CKV + MLA_KV_LORA], kvn_ref[...]).astype(BF16)
        kcat_s[krows, HEAD_BLOCK:] = _tile32(rope(proj[:, P_KR:P_KR + LANES])).astype(BF16)
        for b in range(2):
            c0 = b * LANES
            dq_s[rows, c0:c0 + LANES] = (rope(proj[:, P_DQ + c0:P_DQ + c0 + LANES]) * DIFF_SCALE).astype(BF16)
            dk_s[krows, c0:c0 + LANES] = rope(proj[:, P_DK + c0:P_DK + c0 + LANES]).astype(BF16)
        dv_s[krows, :] = proj[:, P_DV:P_DV + HEAD_BLOCK].astype(BF16)
        nq_s[rows, :] = (proj[:, P_NQ:P_NQ + 2 * HEAD_BLOCK] * NAT_SCALE).astype(BF16)
        nk_s[rows, :] = proj[:, P_NK:P_NK + 2 * HEAD_BLOCK].astype(BF16)
        nv_s[rows, :] = proj[:, P_NV:P_NV + 2 * HEAD_BLOCK].astype(BF16)
        return carry

    lax.fori_loop(0, S // PROJ_ROWS, proj_body, 0)

    def kv_body(i, carry):
        rows = pl.ds(pl.multiple_of(i * PROJ_ROWS, PROJ_ROWS), PROJ_ROWS)
        kv = _dot(ckv_s[rows, :], wukv_ref[...])
        kcat_s[rows, 0:HEAD_BLOCK] = kv[:, 0:HEAD_BLOCK].astype(BF16)
        vmla_s[rows, :] = kv[:, HEAD_BLOCK:].astype(BF16)
        return carry

    lax.fori_loop(0, N_KEYS // PROJ_ROWS, kv_body, 0)

    for j in range(N_KEYS // PROJ_ROWS):
        c0, c1 = j * PROJ_ROWS, (j + 1) * PROJ_ROWS
        vmlat_s[:, c0:c1] = vmla_s[c0:c1, :].astype(F32).T.astype(BF16)
        dvt_s[:, c0:c1] = dv_s[c0:c1, :].astype(F32).T.astype(BF16)

    def dense_body(i, carry):
        rows = pl.ds(pl.multiple_of(i * DENSE_QROWS, DENSE_QROWS), DENSE_QROWS)
        ot_mla = _softmax_heads_t(qcat_s[rows, :], kcat_s[...], vmlat_s[...], mla_qm, False)
        o_s[rows, 0:HEAD_BLOCK] = ot_mla.T.astype(BF16)
        ot_diff = _diff_heads_t(dq_s[rows, :], dk_s[...], dvt_s[...], lam, subln_ref[...], lam_init, False)
        o_s[rows, HEAD_BLOCK:2 * HEAD_BLOCK] = ot_diff.T.astype(BF16)
        return carry

    lax.fori_loop(0, S // DENSE_QROWS, dense_body, 0)

    for r in range(LAT_ROWS):
        rs = min(max(r - NAT_WIN_ROWS // 2, 0), LAT_ROWS - NAT_WIN_ROWS)
        dr0 = rs - r + (NAT_WIN_ROWS - 1)
        par = dr0 % 2
        off = GRID_W * (dr0 - par)
        q0, q1 = r * GRID_W, (r + 1) * GRID_W
        k0, k1 = rs * GRID_W, (rs + NAT_WIN_ROWS) * GRID_W
        for b in range(2):
            c0, c1 = b * HEAD_BLOCK, (b + 1) * HEAD_BLOCK
            q = nq_s[q0:q1, c0:c1]
            zero = jnp.zeros_like(q)
            qs = jnp.concatenate([jnp.where(m, q, zero) for m in hmasks], axis=0)
            sw = _dot_t(qs, nk_s[k0:k1, c0:c1]) + bias_ref[par, b, :, off:off + NAT_WIN_ROWS * GRID_W]
            sc = _dot_t(qs, cnk_ref[:, c0:c1])
            m = jnp.maximum(jnp.max(sw, axis=-1, keepdims=True), jnp.max(sc, axis=-1, keepdims=True))
            ew = jnp.exp2(sw - m)
            ec = jnp.exp2(sc - m)
            l = jnp.sum(ew, axis=-1, keepdims=True) + jnp.sum(ec, axis=-1, keepdims=True)
            o = (_dot(ew.astype(BF16), nv_s[k0:k1, c0:c1]) + _dot(ec.astype(BF16), cnv_ref[:, c0:c1])) / l
            of = None
            for hh, hm in enumerate(hmasks):
                part = jnp.where(hm, o[hh * GRID_W:(hh + 1) * GRID_W], 0.0)
                of = part if of is None else of + part
            o_s[q0:q1, 2 * HEAD_BLOCK + c0:2 * HEAD_BLOCK + c1] = of.astype(BF16)

    def out_body(i, carry):
        rows = pl.ds(pl.multiple_of(i * PROJ_ROWS, PROJ_ROWS), PROJ_ROWS)
        y_ref[rows, :] = x_ref[rows, :] + mod[2:3] * _dot(o_s[rows, :], wout_ref[...])
        return carry

    lax.fori_loop(0, S // PROJ_ROWS, out_body, 0)


def _mix_ii_call(x, ada5, layer, mix_norm, win, qn, wuq, kvn, wukv, lamv, subln, wout, rope_tabs, rpb_rows,
                 caches, lam_init):
    T = x.shape[0]
    Bd = T // LAT_SEQ
    cckv, ckr, cdk, cdv, cnk, cnv = caches
    lyr = lambda shape: _resident((None,) + shape, lambda i: (layer,) + (0,) * len(shape))
    cache = lambda width: pl.BlockSpec((None, None, CTX_SEQ, width), lambda i: (i, layer, 0, 0))
    tab = _resident((LAT_SEQ, LANES), lambda i: (0, 0))
    in_specs = [
        pl.BlockSpec((LAT_SEQ, D_MODEL), lambda i: (i, 0), pipeline_mode=pl.Buffered(1)),
        pl.BlockSpec((None, None, None, 3, D_MODEL), lambda i: (layer, i + 1, 1, 0, 0)),
        lyr((1, D_MODEL)),
        lyr((D_MODEL, P_SPLIT)),
        lyr((D_MODEL, P_COLS - P_SPLIT)),
        lyr((1, MLA_Q_LORA)),
        lyr((MLA_Q_LORA, HEAD_BLOCK + LANES)),
        lyr((1, MLA_KV_LORA)),
        lyr((MLA_KV_LORA, 2 * HEAD_BLOCK)),
        lyr((4, DIFF_DK)),
        lyr((HEAD_BLOCK, 1)),
        lyr((D_MODEL, D_MODEL)),
        tab, tab, tab,
        lyr((2, NAT_HEADS, NAT_WIN_ROWS, LANES)),
        cache(MLA_KV_LORA), cache(LANES), cache(HEAD_BLOCK), cache(HEAD_BLOCK),
        cache(2 * HEAD_BLOCK), cache(2 * HEAD_BLOCK),
    ]
    scratch = [
        pltpu.VMEM((LAT_SEQ, HEAD_BLOCK + LANES), BF16),
        pltpu.VMEM((N_KEYS, MLA_KV_LORA), BF16),
        pltpu.VMEM((N_KEYS, HEAD_BLOCK + LANES), BF16),
        pltpu.VMEM((N_KEYS, HEAD_BLOCK), BF16),
        pltpu.VMEM((LAT_SEQ, HEAD_BLOCK), BF16),
        pltpu.VMEM((N_KEYS, HEAD_BLOCK), BF16),
        pltpu.VMEM((N_KEYS, HEAD_BLOCK), BF16),
        pltpu.VMEM((LAT_SEQ, 2 * HEAD_BLOCK), BF16),
        pltpu.VMEM((LAT_SEQ, 2 * HEAD_BLOCK), BF16),
        pltpu.VMEM((LAT_SEQ, 2 * HEAD_BLOCK), BF16),
        pltpu.VMEM((LAT_SEQ, D_MODEL), BF16),
        pltpu.VMEM((2, 2, HEAD_BLOCK, BIAS_LANES), F32),
        pltpu.VMEM((HEAD_BLOCK, N_KEYS), BF16),
        pltpu.VMEM((HEAD_BLOCK, N_KEYS), BF16),
    ]
    return pl.pallas_call(
        functools.partial(_mix_ii_kernel, lam_init=lam_init),
        grid=(Bd,),
        in_specs=in_specs,
        out_specs=pl.BlockSpec((LAT_SEQ, D_MODEL), lambda i: (i, 0)),
        out_shape=jax.ShapeDtypeStruct((T, D_MODEL), F32),
        scratch_shapes=scratch,
        compiler_params=pltpu.CompilerParams(
            dimension_semantics=("arbitrary",), vmem_limit_bytes=VMEM_LIMIT_BYTES),
        name="mix_lat",
    )(x, ada5, mix_norm, *win, qn, wuq, kvn, wukv, lamv, subln, wout, *rope_tabs, rpb_rows, *caches)


def _rpb_rows(rpb):
    L = rpb.shape[0]
    n_dr, n_dc = 2 * NAT_WIN_ROWS - 1, 2 * NAT_WIN_COLS - 1
    padded = jnp.pad(rpb.astype(F32), ((0, 0), (0, 0), (0, 2 * NAT_WIN_ROWS + 1 - n_dr), (0, GRID_W - n_dc)))
    even = padded[:, :, 0:2 * NAT_WIN_ROWS].reshape(L, NAT_HEADS, NAT_WIN_ROWS, LANES)
    odd = padded[:, :, 1:2 * NAT_WIN_ROWS + 1].reshape(L, NAT_HEADS, NAT_WIN_ROWS, LANES)
    return jnp.stack([even, odd], axis=1)


def _heads_to_lanes(cache):
    B, L, H, S, d = cache.shape
    return jnp.transpose(cache, (0, 1, 3, 2, 4)).reshape(B, L, S, H * d).astype(BF16)


def kernel(x_prompt, x_sample, cache_mla_ckv, cache_mla_krope, cache_diff_k, cache_diff_v, cache_nat_k, cache_nat_v, c, c_ctx, w_ada, b_ada, ffn1_norm, ffn1_w_gate, ffn1_w_up, ffn1_w_down, mix_norm, w_in, mla_q_norm, mla_w_uq, mla_kv_norm, mla_w_ukv, diff_lambda_q1, diff_lambda_k1, diff_lambda_q2, diff_lambda_k2, diff_subln, nat_rpb, w_out, ffn2_norm, ffn2_w_gate, ffn2_w_up, ffn2_w_down, final_norm):
    L = w_ada.shape[0]
    B, S_ctx, _ = x_prompt.shape
    Bd, S_lat, _ = x_sample.shape
    assert S_ctx == CTX_SEQ and S_lat == LAT_SEQ and 1 + Bd <= ADA_ROWS
    assert cache_mla_ckv.shape[2] == CTX_SEQ and (B * CTX_SEQ) % FFN_TILE == 0

    cvec = jnp.concatenate([c_ctx[None, :], c, jnp.zeros((ADA_ROWS - 1 - Bd, D_MODEL), F32)], axis=0)
    ada5 = _ada_call(cvec, w_ada, b_ada).reshape(L, ADA_ROWS, 3, 3, D_MODEL)

    bf = lambda w: w.astype(BF16)
    split = MLA_Q_LORA + MLA_KV_LORA + MLA_ROPE
    w_in16 = bf(w_in)
    win = (jnp.concatenate([w_in16[:, :, :split], jnp.zeros((L, D_MODEL, LANES - MLA_ROPE), BF16)], axis=2),
           w_in16[:, :, split:])
    wuq4 = mla_w_uq.reshape(L, MLA_Q_LORA, MLA_HEADS, MLA_NOPE + MLA_ROPE)
    wuq = bf(jnp.concatenate([wuq4[..., :MLA_NOPE].reshape(L, MLA_Q_LORA, -1),
                              wuq4[..., MLA_NOPE:].reshape(L, MLA_Q_LORA, -1)], axis=2))
    wukv4 = mla_w_ukv.reshape(L, MLA_KV_LORA, MLA_HEADS, MLA_NOPE + MLA_V)
    wukv = bf(jnp.concatenate([wukv4[..., :MLA_NOPE].reshape(L, MLA_KV_LORA, -1),
                               wukv4[..., MLA_NOPE:].reshape(L, MLA_KV_LORA, -1)], axis=2))
    wout = bf(w_out)
    lamv = jnp.stack([diff_lambda_q1, diff_lambda_k1, diff_lambda_q2, diff_lambda_k2], axis=1)
    subln = jnp.tile(diff_subln, (1, DIFF_HEADS)).reshape(L, HEAD_BLOCK, 1)
    r3 = lambda a: a.reshape(L, 1, a.shape[-1])
    n1, n2, nm, qn, kvn = r3(ffn1_norm), r3(ffn2_norm), r3(mix_norm), r3(mla_q_norm), r3(mla_kv_norm)
    fnorm = final_norm.reshape(1, D_MODEL)

    caches = (bf(cache_mla_ckv), bf(jnp.tile(cache_mla_krope, (1, 1, 1, LANES // MLA_ROPE))),
              _heads_to_lanes(cache_diff_k), _heads_to_lanes(cache_diff_v),
              _heads_to_lanes(cache_nat_k), _heads_to_lanes(cache_nat_v))
    rope_tabs = (jnp.asarray(_ROPE_COS), jnp.asarray(_ROPE_SNEXT), jnp.asarray(_ROPE_SPREV))
    rpb_rows = _rpb_rows(nat_rpb)

    xi = x_prompt.reshape(B * CTX_SEQ, D_MODEL)
    xs = x_sample.reshape(Bd * LAT_SEQ, D_MODEL)
    nb = FFN_TILE // CTX_SEQ
    states = None
    for l in range(L):
        lam_init = 0.8 - 0.6 * math.exp(-0.3 * l)
        last = l == L - 1
        xi, xs, *st = _ffn_call(xi, xs, ada5, l, 0, n1, ffn1_w_gate, ffn1_w_up, ffn1_w_down, None, "ffn1",
                                fill_layers=L if l == 0 else 0)
        states = tuple(st) if l == 0 else states
        xi, states = _mix_i_call(xi, ada5, l, L, nm, win, qn, wuq, kvn, wukv, lamv, subln, wout, states, nb, lam_init)
        xs = _mix_ii_call(xs, ada5, l, nm, win, qn, wuq, kvn, wukv, lamv, subln, wout, rope_tabs,
                          rpb_rows, caches, lam_init)
        xi, xs = _ffn_call(xi, xs, ada5, l, 2, n2, ffn2_w_gate, ffn2_w_up, ffn2_w_down,
                           fnorm if last else None, "ffn2")

    return (xi.reshape(B, CTX_SEQ, D_MODEL), xs.reshape(Bd, LAT_SEQ, D_MODEL)) + tuple(states)
```

```python
import functools
import math

import numpy as np
import jax
import jax.numpy as jnp
from jax import lax
from jax.experimental import pallas as pl
from jax.experimental.pallas import tpu as pltpu

F32 = jnp.float32
BF16 = jnp.bfloat16

D_MODEL = 1024
FFN_DIM = 2816
NORM_EPS = 1e-6
ROPE_THETA = 10000.0
GRID_W = 64
N_ADA = 9
NEG_INF = -1e30

MLA_HEADS = 4
MLA_Q_LORA = 256
MLA_KV_LORA = 128
MLA_NOPE = 64
MLA_ROPE = 32
MLA_V = 64
DIFF_HEADS = 4
DIFF_DK = 32
DIFF_DV = 64
NAT_HEADS = 8
NAT_HD = 64
NAT_WIN_ROWS = 8
NAT_WIN_COLS = 16

LOG2E = math.log2(math.e)
MLA_SCALE = (MLA_NOPE + MLA_ROPE) ** -0.5 * LOG2E
DIFF_SCALE = DIFF_DK ** -0.5 * LOG2E
NAT_SCALE = NAT_HD ** -0.5 * LOG2E

CTX_SEQ = 256
LAT_SEQ = 1024
LAT_ROWS = LAT_SEQ // GRID_W

LANES = 128
MXU_DIM = 256
VMEM_LIMIT_BYTES = 58 * 1024 * 1024

P_CQ = 0
P_CKV = 256
P_KR = 384
P_DQ = 512
P_DK = 768
P_DV = 1024
P_NQ = 1280
P_NK = 1792
P_NV = 2304
P_COLS = 2816
P_SPLIT = 512
HEAD_BLOCK = 256
ADA_ROWS = 8

FFN_CHUNKS = ((0, 512), (512, 1024), (1024, 1536), (1536, 2048), (2048, 2560), (2560, 2816))
FFN_TILE = 512
N_STATES = 6


def _rope_tables():
    t = np.arange(LAT_SEQ)
    pos = np.stack([t // GRID_W, t % GRID_W], axis=0).astype(np.float64)
    lane = np.arange(LANES)
    p = lane % 32
    axis = (p >= 16).astype(np.int64)
    freqs = ROPE_THETA ** (-(p % 8).astype(np.float64) / 8.0)
    ang = pos[axis, :].T * freqs[None, :]
    first = (p % 16) < 8
    cos = np.cos(ang)
    sin = np.sin(ang)
    s_next = np.where(first[None, :], -sin, 0.0)
    s_prev = np.where(first[None, :], 0.0, sin)
    return cos.astype(np.float32), s_next.astype(np.float32), s_prev.astype(np.float32)


_ROPE_COS, _ROPE_SNEXT, _ROPE_SPREV = _rope_tables()


def _rmsnorm(x, g):
    ms = jnp.mean(x * x, axis=-1, keepdims=True)
    return x * lax.rsqrt(ms + NORM_EPS) * g


def _silu(x):
    return x / (1.0 + jnp.exp(-x))


def _dot(a, b):
    return jnp.dot(a, b, preferred_element_type=F32)


def _dot_t(a, b):
    return lax.dot_general(a, b, (((1,), (1,)), ((), ())), preferred_element_type=F32)


class _Proj:
    def __init__(self, a, b):
        self.a, self.b = a, b

    def __getitem__(self, idx):
        lo, hi = idx[1].start, idx[1].stop
        return self.a[:, lo:hi] if hi <= P_SPLIT else self.b[:, lo - P_SPLIT:hi - P_SPLIT]


def _lane_mask(width, lo, hi):
    lane = lax.broadcasted_iota(jnp.int32, (1, width), 1)
    return (lane >= lo) & (lane < hi)


def _head_masks(width=HEAD_BLOCK, group=64, n=4):
    return [_lane_mask(width, h * group, (h + 1) * group) for h in range(n)]


def _mla_qmasks():
    lane = lax.broadcasted_iota(jnp.int32, (1, HEAD_BLOCK + LANES), 1)
    out = []
    for h in range(MLA_HEADS):
        nope = (lane >= h * MLA_NOPE) & (lane < (h + 1) * MLA_NOPE)
        rope = (lane >= HEAD_BLOCK + h * MLA_ROPE) & (lane < HEAD_BLOCK + (h + 1) * MLA_ROPE)
        out.append(nope | rope)
    return out


def _tile32(blk):
    return blk + pltpu.roll(blk, 32, 1) + pltpu.roll(blk, 64, 1) + pltpu.roll(blk, 96, 1)


def _rope(x, cos, s_next, s_prev):
    return x * cos + pltpu.roll(x, LANES - 8, 1) * s_next + pltpu.roll(x, 8, 1) * s_prev


def _stack_masked(q, masks):
    zero = jnp.zeros_like(q)
    return jnp.concatenate([jnp.where(m, q, zero) for m in masks], axis=0)


def _scores_t(q, k_bf, kmasks, stacked):
    sk = k_bf.shape[0]
    if stacked:
        s = _dot_t(_stack_masked(k_bf, kmasks), q)
        return [s[h * sk:(h + 1) * sk] for h in range(len(kmasks))]
    zero = jnp.zeros_like(k_bf)
    return [_dot_t(jnp.where(m, k_bf, zero), q) for m in kmasks]


def _pv_t(vt_bf, p_bf, h, short_keys):
    if short_keys:
        return _dot(vt_bf, p_bf)[h * 64:(h + 1) * 64]
    return _dot(vt_bf[h * 64:(h + 1) * 64], p_bf)


def _exp_sum_t(s):
    e = jnp.exp2(s - jnp.max(s, axis=0, keepdims=True))
    return e, jnp.sum(e, axis=0, keepdims=True)


def _softmax_heads_t(q, k_bf, vt_bf, kmasks, stacked):
    parts = []
    for h, s in enumerate(_scores_t(q, k_bf, kmasks, stacked)):
        e, l = _exp_sum_t(s)
        parts.append(_pv_t(vt_bf, e.astype(BF16), h, stacked) / l)
    return jnp.concatenate(parts, axis=0)


def _diff_heads_t(q, k_bf, vt_bf, lam, subln_col, lam_init, stacked):
    n = DIFF_HEADS
    m1 = [_lane_mask(HEAD_BLOCK, h * 2 * DIFF_DK, h * 2 * DIFF_DK + DIFF_DK) for h in range(n)]
    m2 = [_lane_mask(HEAD_BLOCK, h * 2 * DIFF_DK + DIFF_DK, (h + 1) * 2 * DIFF_DK) for h in range(n)]
    s = _scores_t(q, k_bf, m1 + m2, stacked)
    parts = []
    for h in range(n):
        e1, l1 = _exp_sum_t(s[h])
        e2, l2 = _exp_sum_t(s[n + h])
        p = e1 * (1.0 / l1) - e2 * (lam / l2)
        oh = _pv_t(vt_bf, p.astype(BF16), h, stacked)
        ms = jnp.mean(oh * oh, axis=0, keepdims=True)
        parts.append(oh * lax.rsqrt(ms + NORM_EPS))
    return jnp.concatenate(parts, axis=0) * subln_col * (1.0 - lam_init)


def _lambda(lamv, lam_init):
    a = jnp.sum(lamv[0:1] * lamv[1:2], axis=-1, keepdims=True)
    b = jnp.sum(lamv[2:3] * lamv[3:4], axis=-1, keepdims=True)
    return jnp.exp(a) - jnp.exp(b) + lam_init


def _ada_kernel(c_ref, w_ref, b_ref, o_ref):
    s = _silu(c_ref[...]).astype(BF16)
    o_ref[...] = _dot(s, w_ref[...].astype(BF16)) + b_ref[...]


def _ada_call(cvec, w_ada, b_ada):
    L, _, n = w_ada.shape
    tn = 1024
    return pl.pallas_call(
        _ada_kernel,
        grid=(L, n // tn),
        in_specs=[
            pl.BlockSpec((ADA_ROWS, D_MODEL), lambda l, j: (0, 0)),
            pl.BlockSpec((None, D_MODEL, tn), lambda l, j: (l, 0, j)),
            pl.BlockSpec((None, 1, tn), lambda l, j: (l, 0, j)),
        ],
        out_specs=pl.BlockSpec((None, ADA_ROWS, tn), lambda l, j: (l, 0, j)),
        out_shape=jax.ShapeDtypeStruct((L, ADA_ROWS, n), F32),
        compiler_params=pltpu.CompilerParams(
            dimension_semantics=("arbitrary", "arbitrary"), vmem_limit_bytes=VMEM_LIMIT_BYTES),
        name="ada",
    )(cvec, w_ada, b_ada.reshape(L, 1, n))


def _ffn_tile(x_ref, o_ref, mod, g_ref, wg_ref, wu_ref, wd_ref, fn_ref):
    x = x_ref[...]
    h = (_rmsnorm(x, g_ref[...]) * (1.0 + mod[1:2]) + mod[0:1]).astype(BF16)
    acc = None
    for lo, hi in FFN_CHUNKS:
        g = _dot(h, wg_ref[:, lo:hi].astype(BF16))
        u = _dot(h, wu_ref[:, lo:hi].astype(BF16))
        part = _dot((_silu(g) * u).astype(BF16), wd_ref[lo:hi, :].astype(BF16))
        acc = part if acc is None else acc + part
    y = x + (0.5 * mod[2:3]) * acc
    if fn_ref is not None:
        y = _rmsnorm(y, fn_ref[...])
    o_ref[...] = y


def _state_fill_copies(t, state_refs, zero_refs, sem, n_layers):
    per_tile = FFN_TILE // CTX_SEQ
    copies = []
    for j in range(per_tile):
        for l in range(n_layers):
            for st, z in zip(state_refs, zero_refs):
                dst = st.at[t * per_tile + j, l]
                if dst.shape == z.shape:
                    copies.append(pltpu.make_async_copy(z, dst, sem))
                else:
                    nh = z.shape[0]
                    copies += [pltpu.make_async_copy(z, dst.at[h:h + nh], sem) for h in range(0, dst.shape[0], nh)]
    return copies


def _ffn_kernel(xc_ref, xl_ref, mod_ref, g_ref, wg_ref, wu_ref, wd_ref, *rest, n_ctx_tiles, final, fill_layers):
    fn_ref = rest[0] if final else None
    oc_ref, ol_ref = rest[final:final + 2]
    mod = mod_ref[...]
    t = pl.program_id(0)
    if fill_layers:
        state_refs = rest[final + 2:final + 2 + N_STATES]
        z_ckv, z_kr, z_heads, zsem = rest[final + 2 + N_STATES:]

        @pl.when(t == 0)
        def _():
            for z in (z_ckv, z_kr, z_heads):
                z[...] = jnp.zeros(z.shape, z.dtype)

    @pl.when(t < n_ctx_tiles)
    def _():
        fill = []
        if fill_layers:
            fill = _state_fill_copies(t, state_refs, (z_ckv, z_kr, z_heads, z_heads, z_heads, z_heads), zsem,
                                      fill_layers)
        for c in fill:
            c.start()
        _ffn_tile(xc_ref, oc_ref, mod, g_ref, wg_ref, wu_ref, wd_ref, fn_ref)
        for c in fill:
            c.wait()

    @pl.when(t >= n_ctx_tiles)
    def _():
        _ffn_tile(xl_ref, ol_ref, mod, g_ref, wg_ref, wu_ref, wd_ref, fn_ref)


def _resident(shape, index_map):
    return pl.BlockSpec(shape, index_map, pipeline_mode=pl.Buffered(1))


def _state_shapes(B, L):
    return [jax.ShapeDtypeStruct(s, F32) for s in (
        (B, L, CTX_SEQ, MLA_KV_LORA), (B, L, CTX_SEQ, MLA_ROPE),
        (B, L, DIFF_HEADS, CTX_SEQ, 2 * DIFF_DK), (B, L, DIFF_HEADS, CTX_SEQ, DIFF_DV),
        (B, L, NAT_HEADS, CTX_SEQ, NAT_HD), (B, L, NAT_HEADS, CTX_SEQ, NAT_HD))]


def _ffn_call(xc, xl, ada5, layer, group, norm, wg, wu, wd, final_norm, name, fill_layers=0):
    tm = FFN_TILE
    n_ctx = xc.shape[0] // tm
    n_lat = xl.shape[0] // tm
    per_lat = LAT_SEQ // tm
    ctx_blk = lambda t: (jnp.minimum(t, n_ctx - 1), 0)
    lat_blk = lambda t: (jnp.maximum(t - n_ctx, 0), 0)
    ada_row = lambda t: jnp.maximum(t - n_ctx, -per_lat) // per_lat + 1
    in_specs = [
        pl.BlockSpec((tm, D_MODEL), ctx_blk),
        pl.BlockSpec((tm, D_MODEL), lat_blk),
        pl.BlockSpec((None, None, None, 3, D_MODEL), lambda t: (layer, ada_row(t), group, 0, 0)),
        _resident((None, 1, D_MODEL), lambda t: (layer, 0, 0)),
        _resident((None, D_MODEL, FFN_DIM), lambda t: (layer, 0, 0)),
        _resident((None, D_MODEL, FFN_DIM), lambda t: (layer, 0, 0)),
        _resident((None, FFN_DIM, D_MODEL), lambda t: (layer, 0, 0)),
    ]
    args = [xc, xl, ada5, norm, wg, wu, wd]
    if final_norm is not None:
        in_specs.append(_resident((1, D_MODEL), lambda t: (0, 0)))
        args.append(final_norm)
    out_specs = [pl.BlockSpec((tm, D_MODEL), ctx_blk), pl.BlockSpec((tm, D_MODEL), lat_blk)]
    out_shape = [jax.ShapeDtypeStruct(xc.shape, F32), jax.ShapeDtypeStruct(xl.shape, F32)]
    scratch = []
    if fill_layers:
        states = _state_shapes(xc.shape[0] // CTX_SEQ, fill_layers)
        out_specs += [pl.BlockSpec(memory_space=pl.ANY)] * N_STATES
        out_shape += states
        scratch += [pltpu.VMEM(states[k].shape[2:], F32) for k in (0, 1, 2)] + [pltpu.SemaphoreType.DMA(())]
    return pl.pallas_call(
        functools.partial(_ffn_kernel, n_ctx_tiles=n_ctx, final=final_norm is not None, fill_layers=fill_layers),
        grid=(n_ctx + n_lat,),
        in_specs=in_specs,
        out_specs=out_specs,
        out_shape=out_shape,
        scratch_shapes=scratch,
        compiler_params=pltpu.CompilerParams(
            dimension_semantics=("arbitrary",), vmem_limit_bytes=VMEM_LIMIT_BYTES),
        name=name,
    )(*args)


def _mix_i_kernel(x_ref, mod_ref, g_ref, wina_ref, winb_ref, qn_ref, wuq_ref, kvn_ref, wukv_ref, lamv_ref, subln_ref,
                  wout_ref, *refs, nb, lam_init):
    y_ref, ckv_ref, kr_ref, dk_ref, dv_ref, nk_ref, nv_ref = refs[-7:]
    S = CTX_SEQ
    x = x_ref[...]
    mod = mod_ref[...]
    h = (_rmsnorm(x, g_ref[...]) * (1.0 + mod[1:2]) + mod[0:1]).astype(BF16)
    proj = _Proj(_dot(h, wina_ref[...]), _dot(h, winb_ref[...]))
    q_cat = _dot(_rmsnorm(proj[:, P_CQ:P_CQ + MLA_Q_LORA], qn_ref[...]).astype(BF16), wuq_ref[...])
    ckv = _rmsnorm(proj[:, P_CKV:P_CKV + MLA_KV_LORA], kvn_ref[...])
    kv = _dot(ckv.astype(BF16), wukv_ref[...])
    kr_blk = proj[:, P_KR:P_KR + LANES]
    k_cat = jnp.concatenate([kv[:, 0:HEAD_BLOCK], _tile32(kr_blk)], axis=1).astype(BF16)
    v_mla = kv[:, HEAD_BLOCK:2 * HEAD_BLOCK]
    q_cat = (q_cat * MLA_SCALE).astype(BF16)
    dq = (proj[:, P_DQ:P_DQ + HEAD_BLOCK] * DIFF_SCALE).astype(BF16)
    dk = proj[:, P_DK:P_DK + HEAD_BLOCK]
    dv = proj[:, P_DV:P_DV + HEAD_BLOCK]
    nq = (proj[:, P_NQ:P_NQ + 2 * HEAD_BLOCK] * NAT_SCALE).astype(BF16)
    nk = proj[:, P_NK:P_NK + 2 * HEAD_BLOCK]
    nv = proj[:, P_NV:P_NV + 2 * HEAD_BLOCK]
    dk_bf, nk_bf = dk.astype(BF16), nk.astype(BF16)
    lam = _lambda(lamv_ref[...], lam_init)
    hmasks = _head_masks()
    mla_qm = _mla_qmasks()

    outs = []
    for j in range(nb):
        r0, r1 = j * S, (j + 1) * S
        ckv_ref[j] = ckv[r0:r1]
        kr_ref[j] = kr_blk[r0:r1, 0:MLA_ROPE]
        for hh in range(DIFF_HEADS):
            dk_ref[j, hh] = dk[r0:r1, hh * 64:(hh + 1) * 64]
            dv_ref[j, hh] = dv[r0:r1, hh * 64:(hh + 1) * 64]
        for hh in range(NAT_HEADS):
            nk_ref[j, hh] = nk[r0:r1, hh * 64:(hh + 1) * 64]
            nv_ref[j, hh] = nv[r0:r1, hh * 64:(hh + 1) * 64]
        tr = lambda v: v.T.astype(BF16)
        ot_mla = _softmax_heads_t(q_cat[r0:r1], k_cat[r0:r1], tr(v_mla[r0:r1]), mla_qm, True)
        ot_diff = _diff_heads_t(dq[r0:r1], dk_bf[r0:r1], tr(dv[r0:r1]), lam, subln_ref[...], lam_init, True)
        ot_nat = [
            _softmax_heads_t(nq[r0:r1, b * HEAD_BLOCK:(b + 1) * HEAD_BLOCK], nk_bf[r0:r1, b * HEAD_BLOCK:(b + 1) * HEAD_BLOCK],
                             tr(nv[r0:r1, b * HEAD_BLOCK:(b + 1) * HEAD_BLOCK]), hmasks, True)
            for b in range(2)
        ]
        outs.append(jnp.concatenate([ot_mla, ot_diff] + ot_nat, axis=0).T.astype(BF16))
    o = jnp.concatenate(outs, axis=0) if nb > 1 else outs[0]
    y_ref[...] = x + mod[2:3] * _dot(o, wout_ref[...])


def _mix_i_call(x, ada5, layer, n_layers, mix_norm, win, qn, wuq, kvn, wukv, lamv, subln, wout, states, nb, lam_init):
    T = x.shape[0]
    B = T // CTX_SEQ
    tm = nb * CTX_SEQ
    lyr = lambda shape: _resident((None,) + shape, lambda i: (layer,) + (0,) * len(shape))
    any_spec = pl.BlockSpec(memory_space=pl.ANY)
    in_specs = [
        pl.BlockSpec((tm, D_MODEL), lambda i: (i, 0)),
        _resident((None, None, None, 3, D_MODEL), lambda i: (layer, 0, 1, 0, 0)),
        lyr((1, D_MODEL)),
        lyr((D_MODEL, P_SPLIT)),
        lyr((D_MODEL, P_COLS - P_SPLIT)),
        lyr((1, MLA_Q_LORA)),
        lyr((MLA_Q_LORA, HEAD_BLOCK + LANES)),
        lyr((1, MLA_KV_LORA)),
        lyr((MLA_KV_LORA, 2 * HEAD_BLOCK)),
        lyr((4, DIFF_DK)),
        lyr((HEAD_BLOCK, 1)),
        lyr((D_MODEL, D_MODEL)),
    ]
    n_fixed = len(in_specs)
    in_specs += [any_spec] * len(states)
    out_specs = [
        pl.BlockSpec((tm, D_MODEL), lambda i: (i, 0)),
        pl.BlockSpec((nb, None, CTX_SEQ, MLA_KV_LORA), lambda i: (i, layer, 0, 0)),
        pl.BlockSpec((nb, None, CTX_SEQ, MLA_ROPE), lambda i: (i, layer, 0, 0)),
        pl.BlockSpec((nb, None, DIFF_HEADS, CTX_SEQ, 2 * DIFF_DK), lambda i: (i, layer, 0, 0, 0)),
        pl.BlockSpec((nb, None, DIFF_HEADS, CTX_SEQ, DIFF_DV), lambda i: (i, layer, 0, 0, 0)),
        pl.BlockSpec((nb, None, NAT_HEADS, CTX_SEQ, NAT_HD), lambda i: (i, layer, 0, 0, 0)),
        pl.BlockSpec((nb, None, NAT_HEADS, CTX_SEQ, NAT_HD), lambda i: (i, layer, 0, 0, 0)),
    ]
    out_shape = [jax.ShapeDtypeStruct((T, D_MODEL), F32)] + _state_shapes(B, n_layers)
    res = pl.pallas_call(
        functools.partial(_mix_i_kernel, nb=nb, lam_init=lam_init),
        grid=(B // nb,),
        in_specs=in_specs,
        out_specs=out_specs,
        out_shape=out_shape,
        input_output_aliases={n_fixed + k: 1 + k for k in range(len(states))},
        compiler_params=pltpu.CompilerParams(
            dimension_semantics=("arbitrary",), vmem_limit_bytes=VMEM_LIMIT_BYTES),
        name="mix_ctx",
    )(x, ada5, mix_norm, *win, qn, wuq, kvn, wukv, lamv, subln, wout, *states)
    return res[0], tuple(res[1:])


PROJ_ROWS = 256
DENSE_QROWS = MXU_DIM
N_KEYS = CTX_SEQ + LAT_SEQ
BIAS_LANES = NAT_WIN_ROWS * LANES


def _mix_ii_kernel(x_ref, mod_ref, g_ref, wina_ref, winb_ref, qn_ref, wuq_ref, kvn_ref, wukv_ref, lamv_ref, subln_ref,
                   wout_ref, cos_ref, snext_ref, sprev_ref, rpb_ref,
                   cckv_ref, ckr_ref, cdk_ref, cdv_ref, cnk_ref, cnv_ref,
                   y_ref,
                   qcat_s, ckv_s, kcat_s, vmla_s, dq_s, dk_s, dv_s, nq_s, nk_s, nv_s, o_s, bias_ref,
                   vmlat_s, dvt_s, *, lam_init):
    S = LAT_SEQ
    C = CTX_SEQ
    mod = mod_ref[...]
    lam = _lambda(lamv_ref[...], lam_init)
    hmasks = _head_masks()
    mla_qm = _mla_qmasks()

    @pl.when(pl.program_id(0) == 0)
    def _():
        qc = lax.broadcasted_iota(jnp.int32, (GRID_W, LANES), 0)
        kc = lax.broadcasted_iota(jnp.int32, (GRID_W, LANES), 1) % GRID_W
        c_start = jnp.clip(qc - NAT_WIN_COLS // 2, 0, GRID_W - NAT_WIN_COLS)
        in_win = (kc >= c_start) & (kc < c_start + NAT_WIN_COLS)
        for par in range(2):
            for h in range(NAT_HEADS):
                for p in range(NAT_WIN_ROWS):
                    row = jnp.broadcast_to(rpb_ref[par, h, p:p + 1, :], (GRID_W, LANES))
                    tile = pltpu.roll(row, LANES - (NAT_WIN_COLS - 1), 1, stride=1, stride_axis=0)
                    bias_ref[par, h // 4, (h % 4) * GRID_W:(h % 4 + 1) * GRID_W, p * LANES:(p + 1) * LANES] = (
                        jnp.where(in_win, tile * LOG2E, NEG_INF))

    ckv_s[0:C, :] = cckv_ref[...]
    kcat_s[0:C, HEAD_BLOCK:] = ckr_ref[...]
    dk_s[0:C, :] = cdk_ref[...]
    dv_s[0:C, :] = cdv_ref[...]

    def proj_body(i, carry):
        r = pl.multiple_of(i * PROJ_ROWS, PROJ_ROWS)
        rows = pl.ds(r, PROJ_ROWS)
        krows = pl.ds(C + r, PROJ_ROWS)
        cos, s_next, s_prev = cos_ref[rows, :], snext_ref[rows, :], sprev_ref[rows, :]
        rope = lambda v: _rope(v, cos, s_next, s_prev)
        x = x_ref[rows, :]
        h = (_rmsnorm(x, g_ref[...]) * (1.0 + mod[1:2]) + mod[0:1]).astype(BF16)
        proj = _Proj(_dot(h, wina_ref[...]), _dot(h, winb_ref[...]))
        q_cat = _dot(_rmsnorm(proj[:, P_CQ:P_CQ + MLA_Q_LORA], qn_ref[...]).astype(BF16), wuq_ref[...])
        qcat_s[rows, 0:HEAD_BLOCK] = (q_cat[:, 0:HEAD_BLOCK] * MLA_SCALE).astype(BF16)
        qcat_s[rows, HEAD_BLOCK:] = (rope(q_cat[:, HEAD_BLOCK:]) * MLA_SCALE).astype(BF16)
        ckv_s[krows, :] = _rmsnorm(proj[:, P_CKV:P_CKV + MLA_KV_LORA], kvn_ref[...]).astype(BF16)
        kcat_s[krows, HEAD_BLOCK:] = _tile32(rope(proj[:, P_KR:P_KR + LANES])).astype(BF16)
        for b in range(2):
            c0 = b * LANES
            dq_s[rows, c0:c0 + LANES] = (rope(proj[:, P_DQ + c0:P_DQ + c0 + LANES]) * DIFF_SCALE).astype(BF16)
            dk_s[krows, c0:c0 + LANES] = rope(proj[:, P_DK + c0:P_DK + c0 + LANES]).astype(BF16)
        dv_s[krows, :] = proj[:, P_DV:P_DV + HEAD_BLOCK].astype(BF16)
        nq_s[rows, :] = (proj[:, P_NQ:P_NQ + 2 * HEAD_BLOCK] * NAT_SCALE).astype(BF16)
        nk_s[rows, :] = proj[:, P_NK:P_NK + 2 * HEAD_BLOCK].astype(BF16)
        nv_s[rows, :] = proj[:, P_NV:P_NV + 2 * HEAD_BLOCK].astype(BF16)
        return carry

    lax.fori_loop(0, S // PROJ_ROWS, proj_body, 0)

    def kv_body(i, carry):
        rows = pl.ds(pl.multiple_of(i * PROJ_ROWS, PROJ_ROWS), PROJ_ROWS)
        kv = _dot(ckv_s[rows, :], wukv_ref[...])
        kcat_s[rows, 0:HEAD_BLOCK] = kv[:, 0:HEAD_BLOCK].astype(BF16)
        vmla_s[rows, :] = kv[:, HEAD_BLOCK:].astype(BF16)
        return carry

    lax.fori_loop(0, N_KEYS // PROJ_ROWS, kv_body, 0)

    for j in range(N_KEYS // PROJ_ROWS):
        c0, c1 = j * PROJ_ROWS, (j + 1) * PROJ_ROWS
        vmlat_s[:, c0:c1] = vmla_s[c0:c1, :].astype(F32).T.astype(BF16)
        dvt_s[:, c0:c1] = dv_s[c0:c1, :].astype(F32).T.astype(BF16)

    def dense_body(i, carry):
        rows = pl.ds(pl.multiple_of(i * DENSE_QROWS, DENSE_QROWS), DENSE_QROWS)
        ot_mla = _softmax_heads_t(qcat_s[rows, :], kcat_s[...], vmlat_s[...], mla_qm, False)
        o_s[rows, 0:HEAD_BLOCK] = ot_mla.T.astype(BF16)
        ot_diff = _diff_heads_t(dq_s[rows, :], dk_s[...], dvt_s[...], lam, subln_ref[...], lam_init, False)
        o_s[rows, HEAD_BLOCK:2 * HEAD_BLOCK] = ot_diff.T.astype(BF16)
        return carry

    lax.fori_loop(0, S // DENSE_QROWS, dense_body, 0)

    for r in range(LAT_ROWS):
        rs = min(max(r - NAT_WIN_ROWS // 2, 0), LAT_ROWS - NAT_WIN_ROWS)
        dr0 = rs - r + (NAT_WIN_ROWS - 1)
        par = dr0 % 2
        off = GRID_W * (dr0 - par)
        q0, q1 = r * GRID_W, (r + 1) * GRID_W
        k0, k1 = rs * GRID_W, (rs + NAT_WIN_ROWS) * GRID_W
        for b in range(2):
            c0, c1 = b * HEAD_BLOCK, (b + 1) * HEAD_BLOCK
            q = nq_s[q0:q1, c0:c1]
            zero = jnp.zeros_like(q)
            qs = jnp.concatenate([jnp.where(m, q, zero) for m in hmasks], axis=0)
            sw = _dot_t(qs, nk_s[k0:k1, c0:c1]) + bias_ref[par, b, :, off:off + NAT_WIN_ROWS * GRID_W]
            sc = _dot_t(qs, cnk_ref[:, c0:c1])
            m = jnp.maximum(jnp.max(sw, axis=-1, keepdims=True), jnp.max(sc, axis=-1, keepdims=True))
            ew = jnp.exp2(sw - m)
            ec = jnp.exp2(sc - m)
            l = jnp.sum(ew, axis=-1, keepdims=True) + jnp.sum(ec, axis=-1, keepdims=True)
            o = (_dot(ew.astype(BF16), nv_s[k0:k1, c0:c1]) + _dot(ec.astype(BF16), cnv_ref[:, c0:c1])) / l
            of = None
            for hh, hm in enumerate(hmasks):
                part = jnp.where(hm, o[hh * GRID_W:(hh + 1) * GRID_W], 0.0)
                of = part if of is None else of + part
            o_s[q0:q1, 2 * HEAD_BLOCK + c0:2 * HEAD_BLOCK + c1] = of.astype(BF16)

    def out_body(i, carry):
        rows = pl.ds(pl.multiple_of(i * PROJ_ROWS, PROJ_ROWS), PROJ_ROWS)
        y_ref[rows, :] = x_ref[rows, :] + mod[2:3] * _dot(o_s[rows, :], wout_ref[...])
        return carry

    lax.fori_loop(0, S // PROJ_ROWS, out_body, 0)


def _mix_ii_call(x, ada5, layer, mix_norm, win, qn, wuq, kvn, wukv, lamv, subln, wout, rope_tabs, rpb_rows,
                 caches, lam_init):
    T = x.shape[0]
    Bd = T // LAT_SEQ
    cckv, ckr, cdk, cdv, cnk, cnv = caches
    lyr = lambda shape: _resident((None,) + shape, lambda i: (layer,) + (0,) * len(shape))
    cache = lambda width: pl.BlockSpec((None, None, CTX_SEQ, width), lambda i: (i, layer, 0, 0))
    tab = _resident((LAT_SEQ, LANES), lambda i: (0, 0))
    in_specs = [
        pl.BlockSpec((LAT_SEQ, D_MODEL), lambda i: (i, 0), pipeline_mode=pl.Buffered(1)),
        pl.BlockSpec((None, None, None, 3, D_MODEL), lambda i: (layer, i + 1, 1, 0, 0)),
        lyr((1, D_MODEL)),
        lyr((D_MODEL, P_SPLIT)),
        lyr((D_MODEL, P_COLS - P_SPLIT)),
        lyr((1, MLA_Q_LORA)),
        lyr((MLA_Q_LORA, HEAD_BLOCK + LANES)),
        lyr((1, MLA_KV_LORA)),
        lyr((MLA_KV_LORA, 2 * HEAD_BLOCK)),
        lyr((4, DIFF_DK)),
        lyr((HEAD_BLOCK, 1)),
        lyr((D_MODEL, D_MODEL)),
        tab, tab, tab,
        lyr((2, NAT_HEADS, NAT_WIN_ROWS, LANES)),
        cache(MLA_KV_LORA), cache(LANES), cache(HEAD_BLOCK), cache(HEAD_BLOCK),
        cache(2 * HEAD_BLOCK), cache(2 * HEAD_BLOCK),
    ]
    scratch = [
        pltpu.VMEM((LAT_SEQ, HEAD_BLOCK + LANES), BF16),
        pltpu.VMEM((N_KEYS, MLA_KV_LORA), BF16),
        pltpu.VMEM((N_KEYS, HEAD_BLOCK + LANES), BF16),
        pltpu.VMEM((N_KEYS, HEAD_BLOCK), BF16),
        pltpu.VMEM((LAT_SEQ, HEAD_BLOCK), BF16),
        pltpu.VMEM((N_KEYS, HEAD_BLOCK), BF16),
        pltpu.VMEM((N_KEYS, HEAD_BLOCK), BF16),
        pltpu.VMEM((LAT_SEQ, 2 * HEAD_BLOCK), BF16),
        pltpu.VMEM((LAT_SEQ, 2 * HEAD_BLOCK), BF16),
        pltpu.VMEM((LAT_SEQ, 2 * HEAD_BLOCK), BF16),
        pltpu.VMEM((LAT_SEQ, D_MODEL), BF16),
        pltpu.VMEM((2, 2, HEAD_BLOCK, BIAS_LANES), F32),
        pltpu.VMEM((HEAD_BLOCK, N_KEYS), BF16),
        pltpu.VMEM((HEAD_BLOCK, N_KEYS), BF16),
    ]
    return pl.pallas_call(
        functools.partial(_mix_ii_kernel, lam_init=lam_init),
        grid=(Bd,),
        in_specs=in_specs,
        out_specs=pl.BlockSpec((LAT_SEQ, D_MODEL), lambda i: (i, 0)),
        out_shape=jax.ShapeDtypeStruct((T, D_MODEL), F32),
        scratch_shapes=scratch,
        compiler_params=pltpu.CompilerParams(
            dimension_semantics=("arbitrary",), vmem_limit_bytes=VMEM_LIMIT_BYTES),
        name="mix_lat",
    )(x, ada5, mix_norm, *win, qn, wuq, kvn, wukv, lamv, subln, wout, *rope_tabs, rpb_rows, *caches)


W_IN_SPLIT = MLA_Q_LORA + MLA_KV_LORA + MLA_ROPE
WIN_PREP_ROWS = 256


def _win_prep_kernel(w_ref, a_ref, b_ref):
    w = w_ref[...]
    pad = jnp.zeros((w.shape[0], P_SPLIT - W_IN_SPLIT), w.dtype)
    a_ref[...] = jnp.concatenate([w[:, :W_IN_SPLIT], pad], axis=1).astype(BF16)
    b_ref[...] = w[:, W_IN_SPLIT:].astype(BF16)


def _win_prep_call(w_in):
    L, D, n = w_in.shape
    assert n - W_IN_SPLIT == P_COLS - P_SPLIT
    blk = lambda width: pl.BlockSpec((None, WIN_PREP_ROWS, width), lambda l, r: (l, r, 0))
    return pl.pallas_call(
        _win_prep_kernel,
        grid=(L, D // WIN_PREP_ROWS),
        in_specs=[blk(n)],
        out_specs=[blk(P_SPLIT), blk(P_COLS - P_SPLIT)],
        out_shape=[jax.ShapeDtypeStruct((L, D, P_SPLIT), BF16), jax.ShapeDtypeStruct((L, D, P_COLS - P_SPLIT), BF16)],
        compiler_params=pltpu.CompilerParams(dimension_semantics=("arbitrary", "arbitrary")),
        name="win_prep",
    )(w_in)


def _rpb_rows(rpb):
    L = rpb.shape[0]
    n_dr, n_dc = 2 * NAT_WIN_ROWS - 1, 2 * NAT_WIN_COLS - 1
    padded = jnp.pad(rpb.astype(F32), ((0, 0), (0, 0), (0, 2 * NAT_WIN_ROWS + 1 - n_dr), (0, GRID_W - n_dc)))
    even = padded[:, :, 0:2 * NAT_WIN_ROWS].reshape(L, NAT_HEADS, NAT_WIN_ROWS, LANES)
    odd = padded[:, :, 1:2 * NAT_WIN_ROWS + 1].reshape(L, NAT_HEADS, NAT_WIN_ROWS, LANES)
    return jnp.stack([even, odd], axis=1)


def _heads_to_lanes(cache):
    B, L, H, S, d = cache.shape
    return jnp.transpose(cache, (0, 1, 3, 2, 4)).reshape(B, L, S, H * d).astype(BF16)


def kernel(x_prompt, x_sample, cache_mla_ckv, cache_mla_krope, cache_diff_k, cache_diff_v, cache_nat_k, cache_nat_v, c, c_ctx, w_ada, b_ada, ffn1_norm, ffn1_w_gate, ffn1_w_up, ffn1_w_down, mix_norm, w_in, mla_q_norm, mla_w_uq, mla_kv_norm, mla_w_ukv, diff_lambda_q1, diff_lambda_k1, diff_lambda_q2, diff_lambda_k2, diff_subln, nat_rpb, w_out, ffn2_norm, ffn2_w_gate, ffn2_w_up, ffn2_w_down, final_norm):
    L = w_ada.shape[0]
    B, S_ctx, _ = x_prompt.shape
    Bd, S_lat, _ = x_sample.shape
    assert S_ctx == CTX_SEQ and S_lat == LAT_SEQ and 1 + Bd <= ADA_ROWS
    assert cache_mla_ckv.shape[2] == CTX_SEQ and (B * CTX_SEQ) % FFN_TILE == 0

    cvec = jnp.concatenate([c_ctx[None, :], c, jnp.zeros((ADA_ROWS - 1 - Bd, D_MODEL), F32)], axis=0)
    ada5 = _ada_call(cvec, w_ada, b_ada).reshape(L, ADA_ROWS, 3, 3, D_MODEL)

    bf = lambda w: w.astype(BF16)
    win = _win_prep_call(w_in)
    wuq4 = mla_w_uq.reshape(L, MLA_Q_LORA, MLA_HEADS, MLA_NOPE + MLA_ROPE)
    wuq = bf(jnp.concatenate([wuq4[..., :MLA_NOPE].reshape(L, MLA_Q_LORA, -1),
                              wuq4[..., MLA_NOPE:].reshape(L, MLA_Q_LORA, -1)], axis=2))
    wukv4 = mla_w_ukv.reshape(L, MLA_KV_LORA, MLA_HEADS, MLA_NOPE + MLA_V)
    wukv = bf(jnp.concatenate([wukv4[..., :MLA_NOPE].reshape(L, MLA_KV_LORA, -1),
                               wukv4[..., MLA_NOPE:].reshape(L, MLA_KV_LORA, -1)], axis=2))
    wout = bf(w_out)
    lamv = jnp.stack([diff_lambda_q1, diff_lambda_k1, diff_lambda_q2, diff_lambda_k2], axis=1)
    subln = jnp.tile(diff_subln, (1, DIFF_HEADS)).reshape(L, HEAD_BLOCK, 1)
    r3 = lambda a: a.reshape(L, 1, a.shape[-1])
    n1, n2, nm, qn, kvn = r3(ffn1_norm), r3(ffn2_norm), r3(mix_norm), r3(mla_q_norm), r3(mla_kv_norm)
    fnorm = final_norm.reshape(1, D_MODEL)

    caches = (bf(cache_mla_ckv), bf(jnp.tile(cache_mla_krope, (1, 1, 1, LANES // MLA_ROPE))),
              _heads_to_lanes(cache_diff_k), _heads_to_lanes(cache_diff_v),
              _heads_to_lanes(cache_nat_k), _heads_to_lanes(cache_nat_v))
    rope_tabs = (jnp.asarray(_ROPE_COS), jnp.asarray(_ROPE_SNEXT), jnp.asarray(_ROPE_SPREV))
    rpb_rows = _rpb_rows(nat_rpb)

    xi = x_prompt.reshape(B * CTX_SEQ, D_MODEL)
    xs = x_sample.reshape(Bd * LAT_SEQ, D_MODEL)
    nb = FFN_TILE // CTX_SEQ
    states = None
    for l in range(L):
        lam_init = 0.8 - 0.6 * math.exp(-0.3 * l)
        last = l == L - 1
        xi, xs, *st = _ffn_call(xi, xs, ada5, l, 0, n1, ffn1_w_gate, ffn1_w_up, ffn1_w_down, None, "ffn1",
                                fill_layers=L if l == 0 else 0)
        states = tuple(st) if l == 0 else states
        xi, states = _mix_i_call(xi, ada5, l, L, nm, win, qn, wuq, kvn, wukv, lamv, subln, wout, states, nb, lam_init)
        xs = _mix_ii_call(xs, ada5, l, nm, win, qn, wuq, kvn, wukv, lamv, subln, wout, rope_tabs,
                          rpb_rows, caches, lam_init)
        xi, xs = _ffn_call(xi, xs, ada5, l, 2, n2, ffn2_w_gate, ffn2_w_up, ffn2_w_down,
                           fnorm if last else None, "ffn2")

    return (xi.reshape(B, CTX_SEQ, D_MODEL), xs.reshape(Bd, LAT_SEQ, D_MODEL)) + tuple(states)
```

```python
import functools
import math

import numpy as np
import jax
import jax.numpy as jnp
from jax import lax
from jax.experimental import pallas as pl
from jax.experimental.pallas import tpu as pltpu

F32 = jnp.float32
BF16 = jnp.bfloat16

D_MODEL = 1024
FFN_DIM = 2816
NORM_EPS = 1e-6
ROPE_THETA = 10000.0
GRID_W = 64
N_ADA = 9
NEG_INF = -1e30

MLA_HEADS = 4
MLA_Q_LORA = 256
MLA_KV_LORA = 128
MLA_NOPE = 64
MLA_ROPE = 32
MLA_V = 64
DIFF_HEADS = 4
DIFF_DK = 32
DIFF_DV = 64
NAT_HEADS = 8
NAT_HD = 64
NAT_WIN_ROWS = 8
NAT_WIN_COLS = 16

LOG2E = math.log2(math.e)
MLA_SCALE = (MLA_NOPE + MLA_ROPE) ** -0.5 * LOG2E
DIFF_SCALE = DIFF_DK ** -0.5 * LOG2E
NAT_SCALE = NAT_HD ** -0.5 * LOG2E

CTX_SEQ = 256
LAT_SEQ = 1024
LAT_ROWS = LAT_SEQ // GRID_W

LANES = 128
MXU_DIM = 256
VMEM_LIMIT_BYTES = 58 * 1024 * 1024

P_CQ = 0
P_CKV = 256
P_KR = 384
P_DQ = 512
P_DK = 768
P_DV = 1024
P_NQ = 1280
P_NK = 1792
P_NV = 2304
P_COLS = 2816
P_SPLIT = 512
HEAD_BLOCK = 256
ADA_ROWS = 8

FFN_CHUNKS = ((0, 512), (512, 1024), (1024, 1536), (1536, 2048), (2048, 2560), (2560, 2816))
FFN_TILE = 512
N_STATES = 6


def _rope_tables():
    t = np.arange(LAT_SEQ)
    pos = np.stack([t // GRID_W, t % GRID_W], axis=0).astype(np.float64)
    lane = np.arange(LANES)
    p = lane % 32
    axis = (p >= 16).astype(np.int64)
    freqs = ROPE_THETA ** (-(p % 8).astype(np.float64) / 8.0)
    ang = pos[axis, :].T * freqs[None, :]
    first = (p % 16) < 8
    cos = np.cos(ang)
    sin = np.sin(ang)
    s_next = np.where(first[None, :], -sin, 0.0)
    s_prev = np.where(first[None, :], 0.0, sin)
    return cos.astype(np.float32), s_next.astype(np.float32), s_prev.astype(np.float32)


_ROPE_COS, _ROPE_SNEXT, _ROPE_SPREV = _rope_tables()


def _rmsnorm(x, g):
    ms = jnp.mean(x * x, axis=-1, keepdims=True)
    return x * lax.rsqrt(ms + NORM_EPS) * g


def _silu(x):
    return x / (1.0 + jnp.exp(-x))


def _dot(a, b):
    return jnp.dot(a, b, preferred_element_type=F32)


def _dot_t(a, b):
    return lax.dot_general(a, b, (((1,), (1,)), ((), ())), preferred_element_type=F32)


class _Proj:
    def __init__(self, a, b):
        self.a, self.b = a, b

    def __getitem__(self, idx):
        lo, hi = idx[1].start, idx[1].stop
        return self.a[:, lo:hi] if hi <= P_SPLIT else self.b[:, lo - P_SPLIT:hi - P_SPLIT]


def _lane_mask(width, lo, hi):
    lane = lax.broadcasted_iota(jnp.int32, (1, width), 1)
    return (lane >= lo) & (lane < hi)


def _head_masks(width=HEAD_BLOCK, group=64, n=4):
    return [_lane_mask(width, h * group, (h + 1) * group) for h in range(n)]


def _mla_qmasks():
    lane = lax.broadcasted_iota(jnp.int32, (1, HEAD_BLOCK + LANES), 1)
    out = []
    for h in range(MLA_HEADS):
        nope = (lane >= h * MLA_NOPE) & (lane < (h + 1) * MLA_NOPE)
        rope = (lane >= HEAD_BLOCK + h * MLA_ROPE) & (lane < HEAD_BLOCK + (h + 1) * MLA_ROPE)
        out.append(nope | rope)
    return out


def _tile32(blk):
    return blk + pltpu.roll(blk, 32, 1) + pltpu.roll(blk, 64, 1) + pltpu.roll(blk, 96, 1)


def _rope(x, cos, s_next, s_prev):
    return x * cos + pltpu.roll(x, LANES - 8, 1) * s_next + pltpu.roll(x, 8, 1) * s_prev


def _stack_masked(q, masks):
    zero = jnp.zeros_like(q)
    return jnp.concatenate([jnp.where(m, q, zero) for m in masks], axis=0)


def _scores_t(q, k_bf, kmasks, stacked):
    sk = k_bf.shape[0]
    if stacked:
        s = _dot_t(_stack_masked(k_bf, kmasks), q)
        return [s[h * sk:(h + 1) * sk] for h in range(len(kmasks))]
    zero = jnp.zeros_like(k_bf)
    return [_dot_t(jnp.where(m, k_bf, zero), q) for m in kmasks]


def _pv_t(vt_bf, p_bf, h, short_keys):
    if short_keys:
        return _dot(vt_bf, p_bf)[h * 64:(h + 1) * 64]
    return _dot(vt_bf[h * 64:(h + 1) * 64], p_bf)


def _exp_sum_t(s):
    e = jnp.exp2(s - jnp.max(s, axis=0, keepdims=True))
    return e, jnp.sum(e, axis=0, keepdims=True)


def _softmax_heads_t(q, k_bf, vt_bf, kmasks, stacked):
    parts = []
    for h, s in enumerate(_scores_t(q, k_bf, kmasks, stacked)):
        e, l = _exp_sum_t(s)
        parts.append(_pv_t(vt_bf, e.astype(BF16), h, stacked) / l)
    return jnp.concatenate(parts, axis=0)


def _diff_heads_t(q, k_bf, vt_bf, lam, subln_col, lam_init, stacked):
    n = DIFF_HEADS
    m1 = [_lane_mask(HEAD_BLOCK, h * 2 * DIFF_DK, h * 2 * DIFF_DK + DIFF_DK) for h in range(n)]
    m2 = [_lane_mask(HEAD_BLOCK, h * 2 * DIFF_DK + DIFF_DK, (h + 1) * 2 * DIFF_DK) for h in range(n)]
    s = _scores_t(q, k_bf, m1 + m2, stacked)
    parts = []
    for h in range(n):
        e1, l1 = _exp_sum_t(s[h])
        e2, l2 = _exp_sum_t(s[n + h])
        p = e1 * (1.0 / l1) - e2 * (lam / l2)
        oh = _pv_t(vt_bf, p.astype(BF16), h, stacked)
        ms = jnp.mean(oh * oh, axis=0, keepdims=True)
        parts.append(oh * lax.rsqrt(ms + NORM_EPS))
    return jnp.concatenate(parts, axis=0) * subln_col * (1.0 - lam_init)


def _lambda(lamv, lam_init):
    a = jnp.sum(lamv[0:1] * lamv[1:2], axis=-1, keepdims=True)
    b = jnp.sum(lamv[2:3] * lamv[3:4], axis=-1, keepdims=True)
    return jnp.exp(a) - jnp.exp(b) + lam_init


def _ada_kernel(c_ref, w_ref, b_ref, o_ref):
    s = _silu(c_ref[...]).astype(BF16)
    o_ref[...] = _dot(s, w_ref[...].astype(BF16)) + b_ref[...]


def _ada_call(cvec, w_ada, b_ada):
    L, _, n = w_ada.shape
    tn = 1024
    return pl.pallas_call(
        _ada_kernel,
        grid=(L, n // tn),
        in_specs=[
            pl.BlockSpec((ADA_ROWS, D_MODEL), lambda l, j: (0, 0)),
            pl.BlockSpec((None, D_MODEL, tn), lambda l, j: (l, 0, j)),
            pl.BlockSpec((None, 1, tn), lambda l, j: (l, 0, j)),
        ],
        out_specs=pl.BlockSpec((None, ADA_ROWS, tn), lambda l, j: (l, 0, j)),
        out_shape=jax.ShapeDtypeStruct((L, ADA_ROWS, n), F32),
        compiler_params=pltpu.CompilerParams(
            dimension_semantics=("arbitrary", "arbitrary"), vmem_limit_bytes=VMEM_LIMIT_BYTES),
        name="ada",
    )(cvec, w_ada, b_ada.reshape(L, 1, n))


def _ffn_tile(x_ref, o_ref, mod, g_ref, wg_ref, wu_ref, wd_ref, fn_ref):
    x = x_ref[...]
    h = (_rmsnorm(x, g_ref[...]) * (1.0 + mod[1:2]) + mod[0:1]).astype(BF16)
    acc = None
    for lo, hi in FFN_CHUNKS:
        g = _dot(h, wg_ref[:, lo:hi].astype(BF16))
        u = _dot(h, wu_ref[:, lo:hi].astype(BF16))
        part = _dot((_silu(g) * u).astype(BF16), wd_ref[lo:hi, :].astype(BF16))
        acc = part if acc is None else acc + part
    y = x + (0.5 * mod[2:3]) * acc
    if fn_ref is not None:
        y = _rmsnorm(y, fn_ref[...])
    o_ref[...] = y


def _state_fill_copies(t, state_refs, zero_refs, sem, n_layers):
    per_tile = FFN_TILE // CTX_SEQ
    copies = []
    for j in range(per_tile):
        for l in range(n_layers):
            for st, z in zip(state_refs, zero_refs):
                dst = st.at[t * per_tile + j, l]
                if dst.shape == z.shape:
                    copies.append(pltpu.make_async_copy(z, dst, sem))
                else:
                    nh = z.shape[0]
                    copies += [pltpu.make_async_copy(z, dst.at[h:h + nh], sem) for h in range(0, dst.shape[0], nh)]
    return copies


def _ffn_kernel(xc_ref, xl_ref, mod_ref, g_ref, wg_ref, wu_ref, wd_ref, *rest, n_ctx_tiles, final, fill_layers):
    fn_ref = rest[0] if final else None
    oc_ref, ol_ref = rest[final:final + 2]
    mod = mod_ref[...]
    t = pl.program_id(0)
    if fill_layers:
        state_refs = rest[final + 2:final + 2 + N_STATES]
        z_ckv, z_kr, z_heads, zsem = rest[final + 2 + N_STATES:]

        @pl.when(t == 0)
        def _():
            for z in (z_ckv, z_kr, z_heads):
                z[...] = jnp.zeros(z.shape, z.dtype)

    @pl.when(t < n_ctx_tiles)
    def _():
        fill = []
        if fill_layers:
            fill = _state_fill_copies(t, state_refs, (z_ckv, z_kr, z_heads, z_heads, z_heads, z_heads), zsem,
                                      fill_layers)
        for c in fill:
            c.start()
        _ffn_tile(xc_ref, oc_ref, mod, g_ref, wg_ref, wu_ref, wd_ref, fn_ref)
        for c in fill:
            c.wait()

    @pl.when(t >= n_ctx_tiles)
    def _():
        _ffn_tile(xl_ref, ol_ref, mod, g_ref, wg_ref, wu_ref, wd_ref, fn_ref)


def _resident(shape, index_map):
    return pl.BlockSpec(shape, index_map, pipeline_mode=pl.Buffered(1))


def _state_shapes(B, L):
    return [jax.ShapeDtypeStruct(s, F32) for s in (
        (B, L, CTX_SEQ, MLA_KV_LORA), (B, L, CTX_SEQ, MLA_ROPE),
        (B, L, DIFF_HEADS, CTX_SEQ, 2 * DIFF_DK), (B, L, DIFF_HEADS, CTX_SEQ, DIFF_DV),
        (B, L, NAT_HEADS, CTX_SEQ, NAT_HD), (B, L, NAT_HEADS, CTX_SEQ, NAT_HD))]


def _ffn_call(xc, xl, ada5, layer, group, norm, wg, wu, wd, final_norm, name, fill_layers=0):
    tm = FFN_TILE
    n_ctx = xc.shape[0] // tm
    n_lat = xl.shape[0] // tm
    per_lat = LAT_SEQ // tm
    ctx_blk = lambda t: (jnp.minimum(t, n_ctx - 1), 0)
    lat_blk = lambda t: (jnp.maximum(t - n_ctx, 0), 0)
    ada_row = lambda t: jnp.maximum(t - n_ctx, -per_lat) // per_lat + 1
    in_specs = [
        pl.BlockSpec((tm, D_MODEL), ctx_blk),
        pl.BlockSpec((tm, D_MODEL), lat_blk),
        pl.BlockSpec((None, None, None, 3, D_MODEL), lambda t: (layer, ada_row(t), group, 0, 0)),
        _resident((None, 1, D_MODEL), lambda t: (layer, 0, 0)),
        _resident((None, D_MODEL, FFN_DIM), lambda t: (layer, 0, 0)),
        _resident((None, D_MODEL, FFN_DIM), lambda t: (layer, 0, 0)),
        _resident((None, FFN_DIM, D_MODEL), lambda t: (layer, 0, 0)),
    ]
    args = [xc, xl, ada5, norm, wg, wu, wd]
    if final_norm is not None:
        in_specs.append(_resident((1, D_MODEL), lambda t: (0, 0)))
        args.append(final_norm)
    out_specs = [pl.BlockSpec((tm, D_MODEL), ctx_blk), pl.BlockSpec((tm, D_MODEL), lat_blk)]
    out_shape = [jax.ShapeDtypeStruct(xc.shape, F32), jax.ShapeDtypeStruct(xl.shape, F32)]
    scratch = []
    if fill_layers:
        states = _state_shapes(xc.shape[0] // CTX_SEQ, fill_layers)
        out_specs += [pl.BlockSpec(memory_space=pl.ANY)] * N_STATES
        out_shape += states
        scratch += [pltpu.VMEM(states[k].shape[2:], F32) for k in (0, 1, 2)] + [pltpu.SemaphoreType.DMA(())]
    return pl.pallas_call(
        functools.partial(_ffn_kernel, n_ctx_tiles=n_ctx, final=final_norm is not None, fill_layers=fill_layers),
        grid=(n_ctx + n_lat,),
        in_specs=in_specs,
        out_specs=out_specs,
        out_shape=out_shape,
        scratch_shapes=scratch,
        compiler_params=pltpu.CompilerParams(
            dimension_semantics=("arbitrary",), vmem_limit_bytes=VMEM_LIMIT_BYTES),
        name=name,
    )(*args)


def _mix_i_kernel(x_ref, mod_ref, g_ref, wina_ref, winb_ref, qn_ref, wuq_ref, kvn_ref, wukv_ref, lamv_ref, subln_ref,
                  wout_ref, *refs, nb, lam_init):
    y_ref, ckv_ref, kr_ref, dk_ref, dv_ref, nk_ref, nv_ref = refs[-7:]
    S = CTX_SEQ
    x = x_ref[...]
    mod = mod_ref[...]
    h = (_rmsnorm(x, g_ref[...]) * (1.0 + mod[1:2]) + mod[0:1]).astype(BF16)
    proj = _Proj(_dot(h, wina_ref[...]), _dot(h, winb_ref[...]))
    q_cat = _dot(_rmsnorm(proj[:, P_CQ:P_CQ + MLA_Q_LORA], qn_ref[...]).astype(BF16), wuq_ref[...])
    ckv = _rmsnorm(proj[:, P_CKV:P_CKV + MLA_KV_LORA], kvn_ref[...])
    kv = _dot(ckv.astype(BF16), wukv_ref[...])
    kr_blk = proj[:, P_KR:P_KR + LANES]
    k_cat = jnp.concatenate([kv[:, 0:HEAD_BLOCK], _tile32(kr_blk)], axis=1).astype(BF16)
    v_mla = kv[:, HEAD_BLOCK:2 * HEAD_BLOCK]
    q_cat = (q_cat * MLA_SCALE).astype(BF16)
    dq = (proj[:, P_DQ:P_DQ + HEAD_BLOCK] * DIFF_SCALE).astype(BF16)
    dk = proj[:, P_DK:P_DK + HEAD_BLOCK]
    dv = proj[:, P_DV:P_DV + HEAD_BLOCK]
    nq = (proj[:, P_NQ:P_NQ + 2 * HEAD_BLOCK] * NAT_SCALE).astype(BF16)
    nk = proj[:, P_NK:P_NK + 2 * HEAD_BLOCK]
    nv = proj[:, P_NV:P_NV + 2 * HEAD_BLOCK]
    dk_bf, nk_bf = dk.astype(BF16), nk.astype(BF16)
    lam = _lambda(lamv_ref[...], lam_init)
    hmasks = _head_masks()
    mla_qm = _mla_qmasks()

    outs = []
    for j in range(nb):
        r0, r1 = j * S, (j + 1) * S
        ckv_ref[j] = ckv[r0:r1]
        kr_ref[j] = kr_blk[r0:r1, 0:MLA_ROPE]
        for hh in range(DIFF_HEADS):
            dk_ref[j, hh] = dk[r0:r1, hh * 64:(hh + 1) * 64]
            dv_ref[j, hh] = dv[r0:r1, hh * 64:(hh + 1) * 64]
        for hh in range(NAT_HEADS):
            nk_ref[j, hh] = nk[r0:r1, hh * 64:(hh + 1) * 64]
            nv_ref[j, hh] = nv[r0:r1, hh * 64:(hh + 1) * 64]
        tr = lambda v: v.T.astype(BF16)
        ot_mla = _softmax_heads_t(q_cat[r0:r1], k_cat[r0:r1], tr(v_mla[r0:r1]), mla_qm, True)
        ot_diff = _diff_heads_t(dq[r0:r1], dk_bf[r0:r1], tr(dv[r0:r1]), lam, subln_ref[...], lam_init, True)
        ot_nat = [
            _softmax_heads_t(nq[r0:r1, b * HEAD_BLOCK:(b + 1) * HEAD_BLOCK], nk_bf[r0:r1, b * HEAD_BLOCK:(b + 1) * HEAD_BLOCK],
                             tr(nv[r0:r1, b * HEAD_BLOCK:(b + 1) * HEAD_BLOCK]), hmasks, True)
            for b in range(2)
        ]
        outs.append(jnp.concatenate([ot_mla, ot_diff] + ot_nat, axis=0).T.astype(BF16))
    o = jnp.concatenate(outs, axis=0) if nb > 1 else outs[0]
    y_ref[...] = x + mod[2:3] * _dot(o, wout_ref[...])


def _mix_i_call(x, ada5, layer, n_layers, mix_norm, win, qn, wuq, kvn, wukv, lamv, subln, wout, states, nb, lam_init):
    T = x.shape[0]
    B = T // CTX_SEQ
    tm = nb * CTX_SEQ
    lyr = lambda shape: _resident((None,) + shape, lambda i: (layer,) + (0,) * len(shape))
    any_spec = pl.BlockSpec(memory_space=pl.ANY)
    in_specs = [
        pl.BlockSpec((tm, D_MODEL), lambda i: (i, 0)),
        _resident((None, None, None, 3, D_MODEL), lambda i: (layer, 0, 1, 0, 0)),
        lyr((1, D_MODEL)),
        lyr((D_MODEL, P_SPLIT)),
        lyr((D_MODEL, P_COLS - P_SPLIT)),
        lyr((1, MLA_Q_LORA)),
        lyr((MLA_Q_LORA, HEAD_BLOCK + LANES)),
        lyr((1, MLA_KV_LORA)),
        lyr((MLA_KV_LORA, 2 * HEAD_BLOCK)),
        lyr((4, DIFF_DK)),
        lyr((HEAD_BLOCK, 1)),
        lyr((D_MODEL, D_MODEL)),
    ]
    n_fixed = len(in_specs)
    in_specs += [any_spec] * len(states)
    out_specs = [
        pl.BlockSpec((tm, D_MODEL), lambda i: (i, 0)),
        pl.BlockSpec((nb, None, CTX_SEQ, MLA_KV_LORA), lambda i: (i, layer, 0, 0)),
        pl.BlockSpec((nb, None, CTX_SEQ, MLA_ROPE), lambda i: (i, layer, 0, 0)),
        pl.BlockSpec((nb, None, DIFF_HEADS, CTX_SEQ, 2 * DIFF_DK), lambda i: (i, layer, 0, 0, 0)),
        pl.BlockSpec((nb, None, DIFF_HEADS, CTX_SEQ, DIFF_DV), lambda i: (i, layer, 0, 0, 0)),
        pl.BlockSpec((nb, None, NAT_HEADS, CTX_SEQ, NAT_HD), lambda i: (i, layer, 0, 0, 0)),
        pl.BlockSpec((nb, None, NAT_HEADS, CTX_SEQ, NAT_HD), lambda i: (i, layer, 0, 0, 0)),
    ]
    out_shape = [jax.ShapeDtypeStruct((T, D_MODEL), F32)] + _state_shapes(B, n_layers)
    res = pl.pallas_call(
        functools.partial(_mix_i_kernel, nb=nb, lam_init=lam_init),
        grid=(B // nb,),
        in_specs=in_specs,
        out_specs=out_specs,
        out_shape=out_shape,
        input_output_aliases={n_fixed + k: 1 + k for k in range(len(states))},
        compiler_params=pltpu.CompilerParams(
            dimension_semantics=("arbitrary",), vmem_limit_bytes=VMEM_LIMIT_BYTES),
        name="mix_ctx",
    )(x, ada5, mix_norm, *win, qn, wuq, kvn, wukv, lamv, subln, wout, *states)
    return res[0], tuple(res[1:])


PROJ_ROWS = 256
DENSE_QROWS = MXU_DIM
N_KEYS = CTX_SEQ + LAT_SEQ
BIAS_LANES = NAT_WIN_ROWS * LANES


def _mix_ii_kernel(x_ref, mod_ref, g_ref, wina_ref, winb_ref, qn_ref, wuq_ref, kvn_ref, wukv_ref, lamv_ref, subln_ref,
                   wout_ref, cos_ref, snext_ref, sprev_ref, rpb_ref,
                   cckv_ref, ckr_ref, cdk_ref, cdv_ref, cnk_ref, cnv_ref,
                   y_ref,
                   qcat_s, ckv_s, kcat_s, vmla_s, dq_s, dk_s, dv_s, nq_s, nk_s, nv_s, o_s, bias_ref,
                   vmlat_s, dvt_s, cnv_s, *, lam_init):
    S = LAT_SEQ
    C = CTX_SEQ
    mod = mod_ref[...]
    lam = _lambda(lamv_ref[...], lam_init)
    hmasks = _head_masks()
    mla_qm = _mla_qmasks()

    @pl.when(pl.program_id(0) == 0)
    def _():
        qc = lax.broadcasted_iota(jnp.int32, (GRID_W, LANES), 0)
        kc = lax.broadcasted_iota(jnp.int32, (GRID_W, LANES), 1) % GRID_W
        c_start = jnp.clip(qc - NAT_WIN_COLS // 2, 0, GRID_W - NAT_WIN_COLS)
        in_win = (kc >= c_start) & (kc < c_start + NAT_WIN_COLS)
        for par in range(2):
            for h in range(NAT_HEADS):
                for p in range(NAT_WIN_ROWS):
                    row = jnp.broadcast_to(rpb_ref[par, h, p:p + 1, :], (GRID_W, LANES))
                    tile = pltpu.roll(row, LANES - (NAT_WIN_COLS - 1), 1, stride=1, stride_axis=0)
                    bias_ref[par, h // 4, (h % 4) * GRID_W:(h % 4 + 1) * GRID_W, p * LANES:(p + 1) * LANES] = (
                        jnp.where(in_win, tile * LOG2E, NEG_INF))

    ckv_s[0:C, :] = cckv_ref[...]
    kr_t = ckr_ref[...]
    kcat_s[0:C, HEAD_BLOCK:] = jnp.concatenate([kr_t] * (LANES // MLA_ROPE), axis=0).T.astype(BF16)
    dk_s[0:C, :] = cdk_ref[...].T.astype(BF16)
    dvt_s[:, 0:C] = cdv_ref[...].astype(BF16)
    cnv_s[...] = cnv_ref[...].T.astype(BF16)

    def proj_body(i, carry):
        r = pl.multiple_of(i * PROJ_ROWS, PROJ_ROWS)
        rows = pl.ds(r, PROJ_ROWS)
        krows = pl.ds(C + r, PROJ_ROWS)
        cos, s_next, s_prev = cos_ref[rows, :], snext_ref[rows, :], sprev_ref[rows, :]
        rope = lambda v: _rope(v, cos, s_next, s_prev)
        x = x_ref[rows, :]
        h = (_rmsnorm(x, g_ref[...]) * (1.0 + mod[1:2]) + mod[0:1]).astype(BF16)
        proj = _Proj(_dot(h, wina_ref[...]), _dot(h, winb_ref[...]))
        q_cat = _dot(_rmsnorm(proj[:, P_CQ:P_CQ + MLA_Q_LORA], qn_ref[...]).astype(BF16), wuq_ref[...])
        qcat_s[rows, 0:HEAD_BLOCK] = (q_cat[:, 0:HEAD_BLOCK] * MLA_SCALE).astype(BF16)
        qcat_s[rows, HEAD_BLOCK:] = (rope(q_cat[:, HEAD_BLOCK:]) * MLA_SCALE).astype(BF16)
        ckv_s[krows, :] = _rmsnorm(proj[:, P_CKV:P_CKV + MLA_KV_LORA], kvn_ref[...]).astype(BF16)
        kcat_s[krows, HEAD_BLOCK:] = _tile32(rope(proj[:, P_KR:P_KR + LANES])).astype(BF16)
        for b in range(2):
            c0 = b * LANES
            dq_s[rows, c0:c0 + LANES] = (rope(proj[:, P_DQ + c0:P_DQ + c0 + LANES]) * DIFF_SCALE).astype(BF16)
            dk_s[krows, c0:c0 + LANES] = rope(proj[:, P_DK + c0:P_DK + c0 + LANES]).astype(BF16)
        dv_s[krows, :] = proj[:, P_DV:P_DV + HEAD_BLOCK].astype(BF16)
        nq_s[rows, :] = (proj[:, P_NQ:P_NQ + 2 * HEAD_BLOCK] * NAT_SCALE).astype(BF16)
        nk_s[rows, :] = proj[:, P_NK:P_NK + 2 * HEAD_BLOCK].astype(BF16)
        nv_s[rows, :] = proj[:, P_NV:P_NV + 2 * HEAD_BLOCK].astype(BF16)
        return carry

    lax.fori_loop(0, S // PROJ_ROWS, proj_body, 0)

    def kv_body(i, carry):
        rows = pl.ds(pl.multiple_of(i * PROJ_ROWS, PROJ_ROWS), PROJ_ROWS)
        kv = _dot(ckv_s[rows, :], wukv_ref[...])
        kcat_s[rows, 0:HEAD_BLOCK] = kv[:, 0:HEAD_BLOCK].astype(BF16)
        vmla_s[rows, :] = kv[:, HEAD_BLOCK:].astype(BF16)
        return carry

    lax.fori_loop(0, N_KEYS // PROJ_ROWS, kv_body, 0)

    for j in range(N_KEYS // PROJ_ROWS):
        c0, c1 = j * PROJ_ROWS, (j + 1) * PROJ_ROWS
        vmlat_s[:, c0:c1] = vmla_s[c0:c1, :].astype(F32).T.astype(BF16)
        if c0 >= C:
            dvt_s[:, c0:c1] = dv_s[c0:c1, :].astype(F32).T.astype(BF16)

    def dense_body(i, carry):
        rows = pl.ds(pl.multiple_of(i * DENSE_QROWS, DENSE_QROWS), DENSE_QROWS)
        ot_mla = _softmax_heads_t(qcat_s[rows, :], kcat_s[...], vmlat_s[...], mla_qm, False)
        o_s[rows, 0:HEAD_BLOCK] = ot_mla.T.astype(BF16)
        ot_diff = _diff_heads_t(dq_s[rows, :], dk_s[...], dvt_s[...], lam, subln_ref[...], lam_init, False)
        o_s[rows, HEAD_BLOCK:2 * HEAD_BLOCK] = ot_diff.T.astype(BF16)
        return carry

    lax.fori_loop(0, S // DENSE_QROWS, dense_body, 0)

    for r in range(LAT_ROWS):
        rs = min(max(r - NAT_WIN_ROWS // 2, 0), LAT_ROWS - NAT_WIN_ROWS)
        dr0 = rs - r + (NAT_WIN_ROWS - 1)
        par = dr0 % 2
        off = GRID_W * (dr0 - par)
        q0, q1 = r * GRID_W, (r + 1) * GRID_W
        k0, k1 = rs * GRID_W, (rs + NAT_WIN_ROWS) * GRID_W
        for b in range(2):
            c0, c1 = b * HEAD_BLOCK, (b + 1) * HEAD_BLOCK
            q = nq_s[q0:q1, c0:c1]
            zero = jnp.zeros_like(q)
            qs = jnp.concatenate([jnp.where(m, q, zero) for m in hmasks], axis=0)
            sw = _dot_t(qs, nk_s[k0:k1, c0:c1]) + bias_ref[par, b, :, off:off + NAT_WIN_ROWS * GRID_W]
            sc = _dot(qs, cnk_ref[c0:c1, :].astype(BF16))
            m = jnp.maximum(jnp.max(sw, axis=-1, keepdims=True), jnp.max(sc, axis=-1, keepdims=True))
            ew = jnp.exp2(sw - m)
            ec = jnp.exp2(sc - m)
            l = jnp.sum(ew, axis=-1, keepdims=True) + jnp.sum(ec, axis=-1, keepdims=True)
            o = (_dot(ew.astype(BF16), nv_s[k0:k1, c0:c1]) + _dot(ec.astype(BF16), cnv_s[:, c0:c1])) / l
            of = None
            for hh, hm in enumerate(hmasks):
                part = jnp.where(hm, o[hh * GRID_W:(hh + 1) * GRID_W], 0.0)
                of = part if of is None else of + part
            o_s[q0:q1, 2 * HEAD_BLOCK + c0:2 * HEAD_BLOCK + c1] = of.astype(BF16)

    def out_body(i, carry):
        rows = pl.ds(pl.multiple_of(i * PROJ_ROWS, PROJ_ROWS), PROJ_ROWS)
        y_ref[rows, :] = x_ref[rows, :] + mod[2:3] * _dot(o_s[rows, :], wout_ref[...])
        return carry

    lax.fori_loop(0, S // PROJ_ROWS, out_body, 0)


def _mix_ii_call(x, ada5, layer, mix_norm, win, qn, wuq, kvn, wukv, lamv, subln, wout, rope_tabs, rpb_rows,
                 caches, lam_init):
    T = x.shape[0]
    Bd = T // LAT_SEQ
    cckv, ckr, cdk, cdv, cnk, cnv = caches
    lyr = lambda shape: _resident((None,) + shape, lambda i: (layer,) + (0,) * len(shape))
    cache = lambda rows, width: pl.BlockSpec((None, None, rows, width), lambda i: (i, layer, 0, 0),
                                             pipeline_mode=pl.Buffered(1))
    tab = _resident((LAT_SEQ, LANES), lambda i: (0, 0))
    in_specs = [
        pl.BlockSpec((LAT_SEQ, D_MODEL), lambda i: (i, 0), pipeline_mode=pl.Buffered(1)),
        pl.BlockSpec((None, None, None, 3, D_MODEL), lambda i: (layer, i + 1, 1, 0, 0)),
        lyr((1, D_MODEL)),
        lyr((D_MODEL, P_SPLIT)),
        lyr((D_MODEL, P_COLS - P_SPLIT)),
        lyr((1, MLA_Q_LORA)),
        lyr((MLA_Q_LORA, HEAD_BLOCK + LANES)),
        lyr((1, MLA_KV_LORA)),
        lyr((MLA_KV_LORA, 2 * HEAD_BLOCK)),
        lyr((4, DIFF_DK)),
        lyr((HEAD_BLOCK, 1)),
        lyr((D_MODEL, D_MODEL)),
        tab, tab, tab,
        lyr((2, NAT_HEADS, NAT_WIN_ROWS, LANES)),
        cache(CTX_SEQ, MLA_KV_LORA), cache(MLA_ROPE, CTX_SEQ), cache(HEAD_BLOCK, CTX_SEQ), cache(HEAD_BLOCK, CTX_SEQ),
        cache(2 * HEAD_BLOCK, CTX_SEQ), cache(2 * HEAD_BLOCK, CTX_SEQ),
    ]
    scratch = [
        pltpu.VMEM((LAT_SEQ, HEAD_BLOCK + LANES), BF16),
        pltpu.VMEM((N_KEYS, MLA_KV_LORA), BF16),
        pltpu.VMEM((N_KEYS, HEAD_BLOCK + LANES), BF16),
        pltpu.VMEM((N_KEYS, HEAD_BLOCK), BF16),
        pltpu.VMEM((LAT_SEQ, HEAD_BLOCK), BF16),
        pltpu.VMEM((N_KEYS, HEAD_BLOCK), BF16),
        pltpu.VMEM((N_KEYS, HEAD_BLOCK), BF16),
        pltpu.VMEM((LAT_SEQ, 2 * HEAD_BLOCK), BF16),
        pltpu.VMEM((LAT_SEQ, 2 * HEAD_BLOCK), BF16),
        pltpu.VMEM((LAT_SEQ, 2 * HEAD_BLOCK), BF16),
        pltpu.VMEM((LAT_SEQ, D_MODEL), BF16),
        pltpu.VMEM((2, 2, HEAD_BLOCK, BIAS_LANES), F32),
        pltpu.VMEM((HEAD_BLOCK, N_KEYS), BF16),
        pltpu.VMEM((HEAD_BLOCK, N_KEYS), BF16),
        pltpu.VMEM((CTX_SEQ, 2 * HEAD_BLOCK), BF16),
    ]
    return pl.pallas_call(
        functools.partial(_mix_ii_kernel, lam_init=lam_init),
        grid=(Bd,),
        in_specs=in_specs,
        out_specs=pl.BlockSpec((LAT_SEQ, D_MODEL), lambda i: (i, 0)),
        out_shape=jax.ShapeDtypeStruct((T, D_MODEL), F32),
        scratch_shapes=scratch,
        compiler_params=pltpu.CompilerParams(
            dimension_semantics=("arbitrary",), vmem_limit_bytes=VMEM_LIMIT_BYTES),
        name="mix_lat",
    )(x, ada5, mix_norm, *win, qn, wuq, kvn, wukv, lamv, subln, wout, *rope_tabs, rpb_rows, *caches)


W_IN_SPLIT = MLA_Q_LORA + MLA_KV_LORA + MLA_ROPE


def _win_prep_kernel(wt_ref, a_ref, b_ref):
    tr = lambda r0: wt_ref[r0:r0 + MXU_DIM, :].T
    a_ref[:, 0:MXU_DIM] = tr(0).astype(BF16)
    live = _lane_mask(MXU_DIM, 0, W_IN_SPLIT - MXU_DIM)
    a_ref[:, MXU_DIM:P_SPLIT] = jnp.where(live, tr(MXU_DIM), 0.0).astype(BF16)
    for j in range((P_COLS - P_SPLIT) // MXU_DIM):
        b_ref[:, j * MXU_DIM:(j + 1) * MXU_DIM] = tr(W_IN_SPLIT + j * MXU_DIM).astype(BF16)


def _win_prep_call(w_in):
    L, D, n = w_in.shape
    assert n - W_IN_SPLIT == P_COLS - P_SPLIT and P_SPLIT == 2 * MXU_DIM
    return pl.pallas_call(
        _win_prep_kernel,
        grid=(L,),
        in_specs=[pl.BlockSpec((None, n, D), lambda l: (l, 0, 0))],
        out_specs=[pl.BlockSpec((None, D, P_SPLIT), lambda l: (l, 0, 0)),
                   pl.BlockSpec((None, D, P_COLS - P_SPLIT), lambda l: (l, 0, 0))],
        out_shape=[jax.ShapeDtypeStruct((L, D, P_SPLIT), BF16), jax.ShapeDtypeStruct((L, D, P_COLS - P_SPLIT), BF16)],
        compiler_params=pltpu.CompilerParams(
            dimension_semantics=("arbitrary",), vmem_limit_bytes=VMEM_LIMIT_BYTES),
        name="win_prep",
    )(jnp.swapaxes(w_in, 1, 2))


def _rpb_rows(rpb):
    L = rpb.shape[0]
    n_dr, n_dc = 2 * NAT_WIN_ROWS - 1, 2 * NAT_WIN_COLS - 1
    padded = jnp.pad(rpb.astype(F32), ((0, 0), (0, 0), (0, 2 * NAT_WIN_ROWS + 1 - n_dr), (0, GRID_W - n_dc)))
    even = padded[:, :, 0:2 * NAT_WIN_ROWS].reshape(L, NAT_HEADS, NAT_WIN_ROWS, LANES)
    odd = padded[:, :, 1:2 * NAT_WIN_ROWS + 1].reshape(L, NAT_HEADS, NAT_WIN_ROWS, LANES)
    return jnp.stack([even, odd], axis=1)


def _feature_major(cache):
    B, L, H, S, d = cache.shape
    return jnp.swapaxes(cache, -1, -2).reshape(B, L, H * d, S)


def kernel(x_prompt, x_sample, cache_mla_ckv, cache_mla_krope, cache_diff_k, cache_diff_v, cache_nat_k, cache_nat_v, c, c_ctx, w_ada, b_ada, ffn1_norm, ffn1_w_gate, ffn1_w_up, ffn1_w_down, mix_norm, w_in, mla_q_norm, mla_w_uq, mla_kv_norm, mla_w_ukv, diff_lambda_q1, diff_lambda_k1, diff_lambda_q2, diff_lambda_k2, diff_subln, nat_rpb, w_out, ffn2_norm, ffn2_w_gate, ffn2_w_up, ffn2_w_down, final_norm):
    L = w_ada.shape[0]
    B, S_ctx, _ = x_prompt.shape
    Bd, S_lat, _ = x_sample.shape
    assert S_ctx == CTX_SEQ and S_lat == LAT_SEQ and 1 + Bd <= ADA_ROWS
    assert cache_mla_ckv.shape[2] == CTX_SEQ and (B * CTX_SEQ) % FFN_TILE == 0

    cvec = jnp.concatenate([c_ctx[None, :], c, jnp.zeros((ADA_ROWS - 1 - Bd, D_MODEL), F32)], axis=0)
    ada5 = _ada_call(cvec, w_ada, b_ada).reshape(L, ADA_ROWS, 3, 3, D_MODEL)

    bf = lambda w: w.astype(BF16)
    win = _win_prep_call(w_in)
    wuq4 = mla_w_uq.reshape(L, MLA_Q_LORA, MLA_HEADS, MLA_NOPE + MLA_ROPE)
    wuq = bf(jnp.concatenate([wuq4[..., :MLA_NOPE].reshape(L, MLA_Q_LORA, -1),
                              wuq4[..., MLA_NOPE:].reshape(L, MLA_Q_LORA, -1)], axis=2))
    wukv4 = mla_w_ukv.reshape(L, MLA_KV_LORA, MLA_HEADS, MLA_NOPE + MLA_V)
    wukv = bf(jnp.concatenate([wukv4[..., :MLA_NOPE].reshape(L, MLA_KV_LORA, -1),
                               wukv4[..., MLA_NOPE:].reshape(L, MLA_KV_LORA, -1)], axis=2))
    wout = bf(w_out)
    lamv = jnp.stack([diff_lambda_q1, diff_lambda_k1, diff_lambda_q2, diff_lambda_k2], axis=1)
    subln = jnp.tile(diff_subln, (1, DIFF_HEADS)).reshape(L, HEAD_BLOCK, 1)
    r3 = lambda a: a.reshape(L, 1, a.shape[-1])
    n1, n2, nm, qn, kvn = r3(ffn1_norm), r3(ffn2_norm), r3(mix_norm), r3(mla_q_norm), r3(mla_kv_norm)
    fnorm = final_norm.reshape(1, D_MODEL)

    caches = (bf(cache_mla_ckv), jnp.swapaxes(cache_mla_krope, -1, -2),
              _feature_major(cache_diff_k), _feature_major(cache_diff_v),
              _feature_major(cache_nat_k), _feature_major(cache_nat_v))
    rope_tabs = (jnp.asarray(_ROPE_COS), jnp.asarray(_ROPE_SNEXT), jnp.asarray(_ROPE_SPREV))
    rpb_rows = _rpb_rows(nat_rpb)

    xi = x_prompt.reshape(B * CTX_SEQ, D_MODEL)
    xs = x_sample.reshape(Bd * LAT_SEQ, D_MODEL)
    nb = FFN_TILE // CTX_SEQ
    states = None
    for l in range(L):
        lam_init = 0.8 - 0.6 * math.exp(-0.3 * l)
        last = l == L - 1
        xi, xs, *st = _ffn_call(xi, xs, ada5, l, 0, n1, ffn1_w_gate, ffn1_w_up, ffn1_w_down, None, "ffn1",
                                fill_layers=L if l == 0 else 0)
        states = tuple(st) if l == 0 else states
        xi, states = _mix_i_call(xi, ada5, l, L, nm, win, qn, wuq, kvn, wukv, lamv, subln, wout, states, nb, lam_init)
        xs = _mix_ii_call(xs, ada5, l, nm, win, qn, wuq, kvn, wukv, lamv, subln, wout, rope_tabs,
                          rpb_rows, caches, lam_init)
        xi, xs = _ffn_call(xi, xs, ada5, l, 2, n2, ffn2_w_gate, ffn2_w_up, ffn2_w_down,
                           fnorm if last else None, "ffn2")

    return (xi.reshape(B, CTX_SEQ, D_MODEL), xs.reshape(Bd, LAT_SEQ, D_MODEL)) + tuple(states)
```

```python
import functools
import math

import numpy as np
import jax
import jax.numpy as jnp
from jax import lax
from jax.experimental import pallas as pl
from jax.experimental.pallas import tpu as pltpu

F32 = jnp.float32
BF16 = jnp.bfloat16

D_MODEL = 1024
FFN_DIM = 2816
NORM_EPS = 1e-6
ROPE_THETA = 10000.0
GRID_W = 64
N_ADA = 9
NEG_INF = -1e30

MLA_HEADS = 4
MLA_Q_LORA = 256
MLA_KV_LORA = 128
MLA_NOPE = 64
MLA_ROPE = 32
MLA_V = 64
DIFF_HEADS = 4
DIFF_DK = 32
DIFF_DV = 64
NAT_HEADS = 8
NAT_HD = 64
NAT_WIN_ROWS = 8
NAT_WIN_COLS = 16

LOG2E = math.log2(math.e)
MLA_SCALE = (MLA_NOPE + MLA_ROPE) ** -0.5 * LOG2E
DIFF_SCALE = DIFF_DK ** -0.5 * LOG2E
NAT_SCALE = NAT_HD ** -0.5 * LOG2E

CTX_SEQ = 256
LAT_SEQ = 1024
LAT_ROWS = LAT_SEQ // GRID_W

LANES = 128
MXU_DIM = 256
VMEM_LIMIT_BYTES = 58 * 1024 * 1024

P_CQ = 0
P_CKV = 256
P_KR = 384
P_DQ = 512
P_DK = 768
P_DV = 1024
P_NQ = 1280
P_NK = 1792
P_NV = 2304
P_COLS = 2816
P_SPLIT = 512
HEAD_BLOCK = 256
ADA_ROWS = 8

FFN_CHUNKS = ((0, 512), (512, 1024), (1024, 1536), (1536, 2048), (2048, 2560), (2560, 2816))
FFN_TILE = 512
N_STATES = 6


def _rope_tables():
    t = np.arange(LAT_SEQ)
    pos = np.stack([t // GRID_W, t % GRID_W], axis=0).astype(np.float64)
    lane = np.arange(LANES)
    p = lane % 32
    axis = (p >= 16).astype(np.int64)
    freqs = ROPE_THETA ** (-(p % 8).astype(np.float64) / 8.0)
    ang = pos[axis, :].T * freqs[None, :]
    first = (p % 16) < 8
    cos = np.cos(ang)
    sin = np.sin(ang)
    s_next = np.where(first[None, :], -sin, 0.0)
    s_prev = np.where(first[None, :], 0.0, sin)
    return cos.astype(np.float32), s_next.astype(np.float32), s_prev.astype(np.float32)


_ROPE_COS, _ROPE_SNEXT, _ROPE_SPREV = _rope_tables()


def _rmsnorm(x, g):
    ms = jnp.mean(x * x, axis=-1, keepdims=True)
    return x * lax.rsqrt(ms + NORM_EPS) * g


def _silu(x):
    return x / (1.0 + jnp.exp(-x))


def _dot(a, b):
    return jnp.dot(a, b, preferred_element_type=F32)


def _dot_t(a, b):
    return lax.dot_general(a, b, (((1,), (1,)), ((), ())), preferred_element_type=F32)


class _Proj:
    def __init__(self, a, b):
        self.a, self.b = a, b

    def __getitem__(self, idx):
        lo, hi = idx[1].start, idx[1].stop
        return self.a[:, lo:hi] if hi <= P_SPLIT else self.b[:, lo - P_SPLIT:hi - P_SPLIT]


def _lane_mask(width, lo, hi):
    lane = lax.broadcasted_iota(jnp.int32, (1, width), 1)
    return (lane >= lo) & (lane < hi)


def _head_masks(width=HEAD_BLOCK, group=64, n=4):
    return [_lane_mask(width, h * group, (h + 1) * group) for h in range(n)]


def _mla_qmasks():
    lane = lax.broadcasted_iota(jnp.int32, (1, HEAD_BLOCK + LANES), 1)
    out = []
    for h in range(MLA_HEADS):
        nope = (lane >= h * MLA_NOPE) & (lane < (h + 1) * MLA_NOPE)
        rope = (lane >= HEAD_BLOCK + h * MLA_ROPE) & (lane < HEAD_BLOCK + (h + 1) * MLA_ROPE)
        out.append(nope | rope)
    return out


def _tile32(blk):
    return blk + pltpu.roll(blk, 32, 1) + pltpu.roll(blk, 64, 1) + pltpu.roll(blk, 96, 1)


def _rope(x, cos, s_next, s_prev):
    return x * cos + pltpu.roll(x, LANES - 8, 1) * s_next + pltpu.roll(x, 8, 1) * s_prev


def _stack_masked(q, masks):
    zero = jnp.zeros_like(q)
    return jnp.concatenate([jnp.where(m, q, zero) for m in masks], axis=0)


def _scores_t(q, k_bf, kmasks, stacked):
    sk = k_bf.shape[0]
    if stacked:
        s = _dot_t(_stack_masked(k_bf, kmasks), q)
        return [s[h * sk:(h + 1) * sk] for h in range(len(kmasks))]
    zero = jnp.zeros_like(k_bf)
    return [_dot_t(jnp.where(m, k_bf, zero), q) for m in kmasks]


def _pv_t(vt_bf, p_bf, h, short_keys):
    if short_keys:
        return _dot(vt_bf, p_bf)[h * 64:(h + 1) * 64]
    return _dot(vt_bf[h * 64:(h + 1) * 64], p_bf)


def _exp_sum_t(s):
    e = jnp.exp2(s - jnp.max(s, axis=0, keepdims=True))
    return e, jnp.sum(e, axis=0, keepdims=True)


def _softmax_heads_t(q, k_bf, vt_bf, kmasks, stacked):
    parts = []
    for h, s in enumerate(_scores_t(q, k_bf, kmasks, stacked)):
        e, l = _exp_sum_t(s)
        parts.append(_pv_t(vt_bf, e.astype(BF16), h, stacked) / l)
    return jnp.concatenate(parts, axis=0)


def _diff_heads_t(q, k_bf, vt_bf, lam, subln_col, lam_init, stacked):
    n = DIFF_HEADS
    m1 = [_lane_mask(HEAD_BLOCK, h * 2 * DIFF_DK, h * 2 * DIFF_DK + DIFF_DK) for h in range(n)]
    m2 = [_lane_mask(HEAD_BLOCK, h * 2 * DIFF_DK + DIFF_DK, (h + 1) * 2 * DIFF_DK) for h in range(n)]
    s = _scores_t(q, k_bf, m1 + m2, stacked)
    parts = []
    for h in range(n):
        e1, l1 = _exp_sum_t(s[h])
        e2, l2 = _exp_sum_t(s[n + h])
        p = e1 * (1.0 / l1) - e2 * (lam / l2)
        oh = _pv_t(vt_bf, p.astype(BF16), h, stacked)
        ms = jnp.mean(oh * oh, axis=0, keepdims=True)
        parts.append(oh * lax.rsqrt(ms + NORM_EPS))
    return jnp.concatenate(parts, axis=0) * subln_col * (1.0 - lam_init)


def _lambda(lamv, lam_init):
    a = jnp.sum(lamv[0:1] * lamv[1:2], axis=-1, keepdims=True)
    b = jnp.sum(lamv[2:3] * lamv[3:4], axis=-1, keepdims=True)
    return jnp.exp(a) - jnp.exp(b) + lam_init


def _ada_kernel(c_ref, w_ref, b_ref, o_ref):
    s = _silu(c_ref[...]).astype(BF16)
    o_ref[...] = _dot(s, w_ref[...].astype(BF16)) + b_ref[...]


def _ada_call(cvec, w_ada, b_ada):
    L, _, n = w_ada.shape
    tn = 1024
    return pl.pallas_call(
        _ada_kernel,
        grid=(L, n // tn),
        in_specs=[
            pl.BlockSpec((ADA_ROWS, D_MODEL), lambda l, j: (0, 0)),
            pl.BlockSpec((None, D_MODEL, tn), lambda l, j: (l, 0, j)),
            pl.BlockSpec((None, 1, tn), lambda l, j: (l, 0, j)),
        ],
        out_specs=pl.BlockSpec((None, ADA_ROWS, tn), lambda l, j: (l, 0, j)),
        out_shape=jax.ShapeDtypeStruct((L, ADA_ROWS, n), F32),
        compiler_params=pltpu.CompilerParams(
            dimension_semantics=("arbitrary", "arbitrary"), vmem_limit_bytes=VMEM_LIMIT_BYTES),
        name="ada",
    )(cvec, w_ada, b_ada.reshape(L, 1, n))


def _ffn_tile(x_ref, o_ref, mod, g_ref, wg_ref, wu_ref, wd_ref, fn_ref):
    x = x_ref[...]
    h = (_rmsnorm(x, g_ref[...]) * (1.0 + mod[1:2]) + mod[0:1]).astype(BF16)
    acc = None
    for lo, hi in FFN_CHUNKS:
        g = _dot(h, wg_ref[:, lo:hi].astype(BF16))
        u = _dot(h, wu_ref[:, lo:hi].astype(BF16))
        part = _dot((_silu(g) * u).astype(BF16), wd_ref[lo:hi, :].astype(BF16))
        acc = part if acc is None else acc + part
    y = x + (0.5 * mod[2:3]) * acc
    if fn_ref is not None:
        y = _rmsnorm(y, fn_ref[...])
    o_ref[...] = y


def _state_fill_copies(t, state_refs, zero_refs, sem, n_layers):
    per_tile = FFN_TILE // CTX_SEQ
    copies = []
    for j in range(per_tile):
        for l in range(n_layers):
            for st, z in zip(state_refs, zero_refs):
                dst = st.at[t * per_tile + j, l]
                if dst.shape == z.shape:
                    copies.append(pltpu.make_async_copy(z, dst, sem))
                else:
                    nh = z.shape[0]
                    copies += [pltpu.make_async_copy(z, dst.at[h:h + nh], sem) for h in range(0, dst.shape[0], nh)]
    return copies


def _ffn_kernel(xc_ref, xl_ref, mod_ref, g_ref, wg_ref, wu_ref, wd_ref, *rest, n_ctx_tiles, final, fill_layers):
    fn_ref = rest[0] if final else None
    oc_ref, ol_ref = rest[final:final + 2]
    mod = mod_ref[...]
    t = pl.program_id(0)
    if fill_layers:
        state_refs = rest[final + 2:final + 2 + N_STATES]
        z_ckv, z_kr, z_heads, zsem = rest[final + 2 + N_STATES:]

        @pl.when(t == 0)
        def _():
            for z in (z_ckv, z_kr, z_heads):
                z[...] = jnp.zeros(z.shape, z.dtype)

    @pl.when(t < n_ctx_tiles)
    def _():
        fill = []
        if fill_layers:
            fill = _state_fill_copies(t, state_refs, (z_ckv, z_kr, z_heads, z_heads, z_heads, z_heads), zsem,
                                      fill_layers)
        for c in fill:
            c.start()
        _ffn_tile(xc_ref, oc_ref, mod, g_ref, wg_ref, wu_ref, wd_ref, fn_ref)
        for c in fill:
            c.wait()

    @pl.when(t >= n_ctx_tiles)
    def _():
        _ffn_tile(xl_ref, ol_ref, mod, g_ref, wg_ref, wu_ref, wd_ref, fn_ref)


def _resident(shape, index_map):
    return pl.BlockSpec(shape, index_map, pipeline_mode=pl.Buffered(1))


def _state_shapes(B, L):
    return [jax.ShapeDtypeStruct(s, F32) for s in (
        (B, L, CTX_SEQ, MLA_KV_LORA), (B, L, MLA_ROPE, CTX_SEQ),
        (B, L, DIFF_HEADS, 2 * DIFF_DK, CTX_SEQ), (B, L, DIFF_HEADS, DIFF_DV, CTX_SEQ),
        (B, L, NAT_HEADS, NAT_HD, CTX_SEQ), (B, L, NAT_HEADS, NAT_HD, CTX_SEQ))]


def _ffn_call(xc, xl, ada5, layer, group, norm, wg, wu, wd, final_norm, name, fill_layers=0):
    tm = FFN_TILE
    n_ctx = xc.shape[0] // tm
    n_lat = xl.shape[0] // tm
    per_lat = LAT_SEQ // tm
    ctx_blk = lambda t: (jnp.minimum(t, n_ctx - 1), 0)
    lat_blk = lambda t: (jnp.maximum(t - n_ctx, 0), 0)
    ada_row = lambda t: jnp.maximum(t - n_ctx, -per_lat) // per_lat + 1
    in_specs = [
        pl.BlockSpec((tm, D_MODEL), ctx_blk),
        pl.BlockSpec((tm, D_MODEL), lat_blk),
        pl.BlockSpec((None, None, None, 3, D_MODEL), lambda t: (layer, ada_row(t), group, 0, 0)),
        _resident((None, 1, D_MODEL), lambda t: (layer, 0, 0)),
        _resident((None, D_MODEL, FFN_DIM), lambda t: (layer, 0, 0)),
        _resident((None, D_MODEL, FFN_DIM), lambda t: (layer, 0, 0)),
        _resident((None, FFN_DIM, D_MODEL), lambda t: (layer, 0, 0)),
    ]
    args = [xc, xl, ada5, norm, wg, wu, wd]
    if final_norm is not None:
        in_specs.append(_resident((1, D_MODEL), lambda t: (0, 0)))
        args.append(final_norm)
    out_specs = [pl.BlockSpec((tm, D_MODEL), ctx_blk), pl.BlockSpec((tm, D_MODEL), lat_blk)]
    out_shape = [jax.ShapeDtypeStruct(xc.shape, F32), jax.ShapeDtypeStruct(xl.shape, F32)]
    scratch = []
    if fill_layers:
        states = _state_shapes(xc.shape[0] // CTX_SEQ, fill_layers)
        out_specs += [pl.BlockSpec(memory_space=pl.ANY)] * N_STATES
        out_shape += states
        scratch += [pltpu.VMEM(states[k].shape[2:], F32) for k in (0, 1, 2)] + [pltpu.SemaphoreType.DMA(())]
    return pl.pallas_call(
        functools.partial(_ffn_kernel, n_ctx_tiles=n_ctx, final=final_norm is not None, fill_layers=fill_layers),
        grid=(n_ctx + n_lat,),
        in_specs=in_specs,
        out_specs=out_specs,
        out_shape=out_shape,
        scratch_shapes=scratch,
        compiler_params=pltpu.CompilerParams(
            dimension_semantics=("arbitrary",), vmem_limit_bytes=VMEM_LIMIT_BYTES),
        name=name,
    )(*args)


def _mix_i_kernel(x_ref, mod_ref, g_ref, wina_ref, winb_ref, qn_ref, wuq_ref, kvn_ref, wukv_ref, lamv_ref, subln_ref,
                  wout_ref, *refs, nb, lam_init):
    y_ref, ckv_ref, kr_ref, dk_ref, dv_ref, nk_ref, nv_ref = refs[-7:]
    S = CTX_SEQ
    x = x_ref[...]
    mod = mod_ref[...]
    h = (_rmsnorm(x, g_ref[...]) * (1.0 + mod[1:2]) + mod[0:1]).astype(BF16)
    proj = _Proj(_dot(h, wina_ref[...]), _dot(h, winb_ref[...]))
    q_cat = _dot(_rmsnorm(proj[:, P_CQ:P_CQ + MLA_Q_LORA], qn_ref[...]).astype(BF16), wuq_ref[...])
    ckv = _rmsnorm(proj[:, P_CKV:P_CKV + MLA_KV_LORA], kvn_ref[...])
    kv = _dot(ckv.astype(BF16), wukv_ref[...])
    kr_blk = proj[:, P_KR:P_KR + LANES]
    k_cat = jnp.concatenate([kv[:, 0:HEAD_BLOCK], _tile32(kr_blk)], axis=1).astype(BF16)
    v_mla = kv[:, HEAD_BLOCK:2 * HEAD_BLOCK]
    q_cat = (q_cat * MLA_SCALE).astype(BF16)
    dq = (proj[:, P_DQ:P_DQ + HEAD_BLOCK] * DIFF_SCALE).astype(BF16)
    dk = proj[:, P_DK:P_DK + HEAD_BLOCK]
    dv = proj[:, P_DV:P_DV + HEAD_BLOCK]
    nq = (proj[:, P_NQ:P_NQ + 2 * HEAD_BLOCK] * NAT_SCALE).astype(BF16)
    nk = proj[:, P_NK:P_NK + 2 * HEAD_BLOCK]
    nv = proj[:, P_NV:P_NV + 2 * HEAD_BLOCK]
    dk_bf, nk_bf = dk.astype(BF16), nk.astype(BF16)
    lam = _lambda(lamv_ref[...], lam_init)
    hmasks = _head_masks()
    mla_qm = _mla_qmasks()

    outs = []
    for j in range(nb):
        r0, r1 = j * S, (j + 1) * S
        ckv_ref[j] = ckv[r0:r1]
        kr_ref[j] = kr_blk[r0:r1].T[0:MLA_ROPE]
        dk_t, dv_t = dk[r0:r1].T, dv[r0:r1].T
        nk_t = [nk[r0:r1, b * HEAD_BLOCK:(b + 1) * HEAD_BLOCK].T for b in range(2)]
        nv_t = [nv[r0:r1, b * HEAD_BLOCK:(b + 1) * HEAD_BLOCK].T for b in range(2)]
        for hh in range(DIFF_HEADS):
            dk_ref[j, hh] = dk_t[hh * 64:(hh + 1) * 64]
            dv_ref[j, hh] = dv_t[hh * 64:(hh + 1) * 64]
        for hh in range(NAT_HEADS):
            nk_ref[j, hh] = nk_t[hh // 4][(hh % 4) * 64:(hh % 4 + 1) * 64]
            nv_ref[j, hh] = nv_t[hh // 4][(hh % 4) * 64:(hh % 4 + 1) * 64]
        ot_mla = _softmax_heads_t(q_cat[r0:r1], k_cat[r0:r1], v_mla[r0:r1].T.astype(BF16), mla_qm, True)
        ot_diff = _diff_heads_t(dq[r0:r1], dk_bf[r0:r1], dv_t.astype(BF16), lam, subln_ref[...], lam_init, True)
        ot_nat = [
            _softmax_heads_t(nq[r0:r1, b * HEAD_BLOCK:(b + 1) * HEAD_BLOCK], nk_bf[r0:r1, b * HEAD_BLOCK:(b + 1) * HEAD_BLOCK],
                             nv_t[b].astype(BF16), hmasks, True)
            for b in range(2)
        ]
        outs.append(jnp.concatenate([ot_mla, ot_diff] + ot_nat, axis=0).T.astype(BF16))
    o = jnp.concatenate(outs, axis=0) if nb > 1 else outs[0]
    y_ref[...] = x + mod[2:3] * _dot(o, wout_ref[...])


def _mix_i_call(x, ada5, layer, n_layers, mix_norm, win, qn, wuq, kvn, wukv, lamv, subln, wout, states, nb, lam_init):
    T = x.shape[0]
    B = T // CTX_SEQ
    tm = nb * CTX_SEQ
    lyr = lambda shape: _resident((None,) + shape, lambda i: (layer,) + (0,) * len(shape))
    any_spec = pl.BlockSpec(memory_space=pl.ANY)
    in_specs = [
        pl.BlockSpec((tm, D_MODEL), lambda i: (i, 0)),
        _resident((None, None, None, 3, D_MODEL), lambda i: (layer, 0, 1, 0, 0)),
        lyr((1, D_MODEL)),
        lyr((D_MODEL, P_SPLIT)),
        lyr((D_MODEL, P_COLS - P_SPLIT)),
        lyr((1, MLA_Q_LORA)),
        lyr((MLA_Q_LORA, HEAD_BLOCK + LANES)),
        lyr((1, MLA_KV_LORA)),
        lyr((MLA_KV_LORA, 2 * HEAD_BLOCK)),
        lyr((4, DIFF_DK)),
        lyr((HEAD_BLOCK, 1)),
        lyr((D_MODEL, D_MODEL)),
    ]
    n_fixed = len(in_specs)
    in_specs += [any_spec] * len(states)
    out_specs = [
        pl.BlockSpec((tm, D_MODEL), lambda i: (i, 0)),
        pl.BlockSpec((nb, None, CTX_SEQ, MLA_KV_LORA), lambda i: (i, layer, 0, 0)),
        pl.BlockSpec((nb, None, MLA_ROPE, CTX_SEQ), lambda i: (i, layer, 0, 0)),
        pl.BlockSpec((nb, None, DIFF_HEADS, 2 * DIFF_DK, CTX_SEQ), lambda i: (i, layer, 0, 0, 0)),
        pl.BlockSpec((nb, None, DIFF_HEADS, DIFF_DV, CTX_SEQ), lambda i: (i, layer, 0, 0, 0)),
        pl.BlockSpec((nb, None, NAT_HEADS, NAT_HD, CTX_SEQ), lambda i: (i, layer, 0, 0, 0)),
        pl.BlockSpec((nb, None, NAT_HEADS, NAT_HD, CTX_SEQ), lambda i: (i, layer, 0, 0, 0)),
    ]
    out_shape = [jax.ShapeDtypeStruct((T, D_MODEL), F32)] + _state_shapes(B, n_layers)
    res = pl.pallas_call(
        functools.partial(_mix_i_kernel, nb=nb, lam_init=lam_init),
        grid=(B // nb,),
        in_specs=in_specs,
        out_specs=out_specs,
        out_shape=out_shape,
        input_output_aliases={n_fixed + k: 1 + k for k in range(len(states))},
        compiler_params=pltpu.CompilerParams(
            dimension_semantics=("arbitrary",), vmem_limit_bytes=VMEM_LIMIT_BYTES),
        name="mix_ctx",
    )(x, ada5, mix_norm, *win, qn, wuq, kvn, wukv, lamv, subln, wout, *states)
    return res[0], tuple(res[1:])


PROJ_ROWS = 256
DENSE_QROWS = MXU_DIM
N_KEYS = CTX_SEQ + LAT_SEQ
BIAS_LANES = NAT_WIN_ROWS * LANES


def _mix_ii_kernel(x_ref, mod_ref, g_ref, wina_ref, winb_ref, qn_ref, wuq_ref, kvn_ref, wukv_ref, lamv_ref, subln_ref,
                   wout_ref, cos_ref, snext_ref, sprev_ref, rpb_ref,
                   cckv_ref, ckr_ref, cdk_ref, cdv_ref, cnk_ref, cnv_ref,
                   y_ref,
                   qcat_s, ckv_s, kcat_s, vmla_s, dq_s, dk_s, dv_s, nq_s, nk_s, nv_s, o_s, bias_ref,
                   vmlat_s, dvt_s, cnv_s, *, lam_init):
    S = LAT_SEQ
    C = CTX_SEQ
    mod = mod_ref[...]
    lam = _lambda(lamv_ref[...], lam_init)
    hmasks = _head_masks()
    mla_qm = _mla_qmasks()

    @pl.when(pl.program_id(0) == 0)
    def _():
        qc = lax.broadcasted_iota(jnp.int32, (GRID_W, LANES), 0)
        kc = lax.broadcasted_iota(jnp.int32, (GRID_W, LANES), 1) % GRID_W
        c_start = jnp.clip(qc - NAT_WIN_COLS // 2, 0, GRID_W - NAT_WIN_COLS)
        in_win = (kc >= c_start) & (kc < c_start + NAT_WIN_COLS)
        for par in range(2):
            for h in range(NAT_HEADS):
                for p in range(NAT_WIN_ROWS):
                    row = jnp.broadcast_to(rpb_ref[par, h, p:p + 1, :], (GRID_W, LANES))
                    tile = pltpu.roll(row, LANES - (NAT_WIN_COLS - 1), 1, stride=1, stride_axis=0)
                    bias_ref[par, h // 4, (h % 4) * GRID_W:(h % 4 + 1) * GRID_W, p * LANES:(p + 1) * LANES] = (
                        jnp.where(in_win, tile * LOG2E, NEG_INF))

    ckv_s[0:C, :] = cckv_ref[...]
    kr_t = ckr_ref[...]
    kcat_s[0:C, HEAD_BLOCK:] = jnp.concatenate([kr_t] * (LANES // MLA_ROPE), axis=0).T.astype(BF16)
    dk_s[0:C, :] = cdk_ref[...].T.astype(BF16)
    dvt_s[:, 0:C] = cdv_ref[...].astype(BF16)
    cnv_s[...] = cnv_ref[...].T.astype(BF16)

    def proj_body(i, carry):
        r = pl.multiple_of(i * PROJ_ROWS, PROJ_ROWS)
        rows = pl.ds(r, PROJ_ROWS)
        krows = pl.ds(C + r, PROJ_ROWS)
        cos, s_next, s_prev = cos_ref[rows, :], snext_ref[rows, :], sprev_ref[rows, :]
        rope = lambda v: _rope(v, cos, s_next, s_prev)
        x = x_ref[rows, :]
        h = (_rmsnorm(x, g_ref[...]) * (1.0 + mod[1:2]) + mod[0:1]).astype(BF16)
        proj = _Proj(_dot(h, wina_ref[...]), _dot(h, winb_ref[...]))
        q_cat = _dot(_rmsnorm(proj[:, P_CQ:P_CQ + MLA_Q_LORA], qn_ref[...]).astype(BF16), wuq_ref[...])
        qcat_s[rows, 0:HEAD_BLOCK] = (q_cat[:, 0:HEAD_BLOCK] * MLA_SCALE).astype(BF16)
        qcat_s[rows, HEAD_BLOCK:] = (rope(q_cat[:, HEAD_BLOCK:]) * MLA_SCALE).astype(BF16)
        ckv_s[krows, :] = _rmsnorm(proj[:, P_CKV:P_CKV + MLA_KV_LORA], kvn_ref[...]).astype(BF16)
        kcat_s[krows, HEAD_BLOCK:] = _tile32(rope(proj[:, P_KR:P_KR + LANES])).astype(BF16)
        for b in range(2):
            c0 = b * LANES
            dq_s[rows, c0:c0 + LANES] = (rope(proj[:, P_DQ + c0:P_DQ + c0 + LANES]) * DIFF_SCALE).astype(BF16)
            dk_s[krows, c0:c0 + LANES] = rope(proj[:, P_DK + c0:P_DK + c0 + LANES]).astype(BF16)
        dv_s[krows, :] = proj[:, P_DV:P_DV + HEAD_BLOCK].astype(BF16)
        nq_s[rows, :] = (proj[:, P_NQ:P_NQ + 2 * HEAD_BLOCK] * NAT_SCALE).astype(BF16)
        nk_s[rows, :] = proj[:, P_NK:P_NK + 2 * HEAD_BLOCK].astype(BF16)
        nv_s[rows, :] = proj[:, P_NV:P_NV + 2 * HEAD_BLOCK].astype(BF16)
        return carry

    lax.fori_loop(0, S // PROJ_ROWS, proj_body, 0)

    def kv_body(i, carry):
        rows = pl.ds(pl.multiple_of(i * PROJ_ROWS, PROJ_ROWS), PROJ_ROWS)
        kv = _dot(ckv_s[rows, :], wukv_ref[...])
        kcat_s[rows, 0:HEAD_BLOCK] = kv[:, 0:HEAD_BLOCK].astype(BF16)
        vmla_s[rows, :] = kv[:, HEAD_BLOCK:].astype(BF16)
        return carry

    lax.fori_loop(0, N_KEYS // PROJ_ROWS, kv_body, 0)

    for j in range(N_KEYS // PROJ_ROWS):
        c0, c1 = j * PROJ_ROWS, (j + 1) * PROJ_ROWS
        vmlat_s[:, c0:c1] = vmla_s[c0:c1, :].astype(F32).T.astype(BF16)
        if c0 >= C:
            dvt_s[:, c0:c1] = dv_s[c0:c1, :].astype(F32).T.astype(BF16)

    def dense_body(i, carry):
        rows = pl.ds(pl.multiple_of(i * DENSE_QROWS, DENSE_QROWS), DENSE_QROWS)
        ot_mla = _softmax_heads_t(qcat_s[rows, :], kcat_s[...], vmlat_s[...], mla_qm, False)
        o_s[rows, 0:HEAD_BLOCK] = ot_mla.T.astype(BF16)
        ot_diff = _diff_heads_t(dq_s[rows, :], dk_s[...], dvt_s[...], lam, subln_ref[...], lam_init, False)
        o_s[rows, HEAD_BLOCK:2 * HEAD_BLOCK] = ot_diff.T.astype(BF16)
        return carry

    lax.fori_loop(0, S // DENSE_QROWS, dense_body, 0)

    for r in range(LAT_ROWS):
        rs = min(max(r - NAT_WIN_ROWS // 2, 0), LAT_ROWS - NAT_WIN_ROWS)
        dr0 = rs - r + (NAT_WIN_ROWS - 1)
        par = dr0 % 2
        off = GRID_W * (dr0 - par)
        q0, q1 = r * GRID_W, (r + 1) * GRID_W
        k0, k1 = rs * GRID_W, (rs + NAT_WIN_ROWS) * GRID_W
        for b in range(2):
            c0, c1 = b * HEAD_BLOCK, (b + 1) * HEAD_BLOCK
            q = nq_s[q0:q1, c0:c1]
            zero = jnp.zeros_like(q)
            qs = jnp.concatenate([jnp.where(m, q, zero) for m in hmasks], axis=0)
            sw = _dot_t(qs, nk_s[k0:k1, c0:c1]) + bias_ref[par, b, :, off:off + NAT_WIN_ROWS * GRID_W]
            sc = _dot(qs, cnk_ref[c0:c1, :].astype(BF16))
            m = jnp.maximum(jnp.max(sw, axis=-1, keepdims=True), jnp.max(sc, axis=-1, keepdims=True))
            ew = jnp.exp2(sw - m)
            ec = jnp.exp2(sc - m)
            l = jnp.sum(ew, axis=-1, keepdims=True) + jnp.sum(ec, axis=-1, keepdims=True)
            o = (_dot(ew.astype(BF16), nv_s[k0:k1, c0:c1]) + _dot(ec.astype(BF16), cnv_s[:, c0:c1])) / l
            of = None
            for hh, hm in enumerate(hmasks):
                part = jnp.where(hm, o[hh * GRID_W:(hh + 1) * GRID_W], 0.0)
                of = part if of is None else of + part
            o_s[q0:q1, 2 * HEAD_BLOCK + c0:2 * HEAD_BLOCK + c1] = of.astype(BF16)

    def out_body(i, carry):
        rows = pl.ds(pl.multiple_of(i * PROJ_ROWS, PROJ_ROWS), PROJ_ROWS)
        y_ref[rows, :] = x_ref[rows, :] + mod[2:3] * _dot(o_s[rows, :], wout_ref[...])
        return carry

    lax.fori_loop(0, S // PROJ_ROWS, out_body, 0)


def _mix_ii_call(x, ada5, layer, mix_norm, win, qn, wuq, kvn, wukv, lamv, subln, wout, rope_tabs, rpb_rows,
                 caches, lam_init):
    T = x.shape[0]
    Bd = T // LAT_SEQ
    cckv, ckr, cdk, cdv, cnk, cnv = caches
    lyr = lambda shape: _resident((None,) + shape, lambda i: (layer,) + (0,) * len(shape))
    cache = lambda rows, width: pl.BlockSpec((None, None, rows, width), lambda i: (i, layer, 0, 0),
                                             pipeline_mode=pl.Buffered(1))
    tab = _resident((LAT_SEQ, LANES), lambda i: (0, 0))
    in_specs = [
        pl.BlockSpec((LAT_SEQ, D_MODEL), lambda i: (i, 0), pipeline_mode=pl.Buffered(1)),
        pl.BlockSpec((None, None, None, 3, D_MODEL), lambda i: (layer, i + 1, 1, 0, 0)),
        lyr((1, D_MODEL)),
        lyr((D_MODEL, P_SPLIT)),
        lyr((D_MODEL, P_COLS - P_SPLIT)),
        lyr((1, MLA_Q_LORA)),
        lyr((MLA_Q_LORA, HEAD_BLOCK + LANES)),
        lyr((1, MLA_KV_LORA)),
        lyr((MLA_KV_LORA, 2 * HEAD_BLOCK)),
        lyr((4, DIFF_DK)),
        lyr((HEAD_BLOCK, 1)),
        lyr((D_MODEL, D_MODEL)),
        tab, tab, tab,
        lyr((2, NAT_HEADS, NAT_WIN_ROWS, LANES)),
        cache(CTX_SEQ, MLA_KV_LORA), cache(MLA_ROPE, CTX_SEQ), cache(HEAD_BLOCK, CTX_SEQ), cache(HEAD_BLOCK, CTX_SEQ),
        cache(2 * HEAD_BLOCK, CTX_SEQ), cache(2 * HEAD_BLOCK, CTX_SEQ),
    ]
    scratch = [
        pltpu.VMEM((LAT_SEQ, HEAD_BLOCK + LANES), BF16),
        pltpu.VMEM((N_KEYS, MLA_KV_LORA), BF16),
        pltpu.VMEM((N_KEYS, HEAD_BLOCK + LANES), BF16),
        pltpu.VMEM((N_KEYS, HEAD_BLOCK), BF16),
        pltpu.VMEM((LAT_SEQ, HEAD_BLOCK), BF16),
        pltpu.VMEM((N_KEYS, HEAD_BLOCK), BF16),
        pltpu.VMEM((N_KEYS, HEAD_BLOCK), BF16),
        pltpu.VMEM((LAT_SEQ, 2 * HEAD_BLOCK), BF16),
        pltpu.VMEM((LAT_SEQ, 2 * HEAD_BLOCK), BF16),
        pltpu.VMEM((LAT_SEQ, 2 * HEAD_BLOCK), BF16),
        pltpu.VMEM((LAT_SEQ, D_MODEL), BF16),
        pltpu.VMEM((2, 2, HEAD_BLOCK, BIAS_LANES), F32),
        pltpu.VMEM((HEAD_BLOCK, N_KEYS), BF16),
        pltpu.VMEM((HEAD_BLOCK, N_KEYS), BF16),
        pltpu.VMEM((CTX_SEQ, 2 * HEAD_BLOCK), BF16),
    ]
    return pl.pallas_call(
        functools.partial(_mix_ii_kernel, lam_init=lam_init),
        grid=(Bd,),
        in_specs=in_specs,
        out_specs=pl.BlockSpec((LAT_SEQ, D_MODEL), lambda i: (i, 0)),
        out_shape=jax.ShapeDtypeStruct((T, D_MODEL), F32),
        scratch_shapes=scratch,
        compiler_params=pltpu.CompilerParams(
            dimension_semantics=("arbitrary",), vmem_limit_bytes=VMEM_LIMIT_BYTES),
        name="mix_lat",
    )(x, ada5, mix_norm, *win, qn, wuq, kvn, wukv, lamv, subln, wout, *rope_tabs, rpb_rows, *caches)


W_IN_SPLIT = MLA_Q_LORA + MLA_KV_LORA + MLA_ROPE


def _win_prep_kernel(wt_ref, a_ref, b_ref):
    tr = lambda r0: wt_ref[r0:r0 + MXU_DIM, :].T
    a_ref[:, 0:MXU_DIM] = tr(0).astype(BF16)
    live = _lane_mask(MXU_DIM, 0, W_IN_SPLIT - MXU_DIM)
    a_ref[:, MXU_DIM:P_SPLIT] = jnp.where(live, tr(MXU_DIM), 0.0).astype(BF16)
    for j in range((P_COLS - P_SPLIT) // MXU_DIM):
        b_ref[:, j * MXU_DIM:(j + 1) * MXU_DIM] = tr(W_IN_SPLIT + j * MXU_DIM).astype(BF16)


def _win_prep_call(w_in):
    L, D, n = w_in.shape
    assert n - W_IN_SPLIT == P_COLS - P_SPLIT and P_SPLIT == 2 * MXU_DIM
    return pl.pallas_call(
        _win_prep_kernel,
        grid=(L,),
        in_specs=[pl.BlockSpec((None, n, D), lambda l: (l, 0, 0))],
        out_specs=[pl.BlockSpec((None, D, P_SPLIT), lambda l: (l, 0, 0)),
                   pl.BlockSpec((None, D, P_COLS - P_SPLIT), lambda l: (l, 0, 0))],
        out_shape=[jax.ShapeDtypeStruct((L, D, P_SPLIT), BF16), jax.ShapeDtypeStruct((L, D, P_COLS - P_SPLIT), BF16)],
        compiler_params=pltpu.CompilerParams(
            dimension_semantics=("arbitrary",), vmem_limit_bytes=VMEM_LIMIT_BYTES),
        name="win_prep",
    )(jnp.swapaxes(w_in, 1, 2))


def _rpb_rows(rpb):
    L = rpb.shape[0]
    n_dr, n_dc = 2 * NAT_WIN_ROWS - 1, 2 * NAT_WIN_COLS - 1
    padded = jnp.pad(rpb.astype(F32), ((0, 0), (0, 0), (0, 2 * NAT_WIN_ROWS + 1 - n_dr), (0, GRID_W - n_dc)))
    even = padded[:, :, 0:2 * NAT_WIN_ROWS].reshape(L, NAT_HEADS, NAT_WIN_ROWS, LANES)
    odd = padded[:, :, 1:2 * NAT_WIN_ROWS + 1].reshape(L, NAT_HEADS, NAT_WIN_ROWS, LANES)
    return jnp.stack([even, odd], axis=1)


def _feature_major(cache):
    B, L, H, S, d = cache.shape
    return jnp.swapaxes(cache, -1, -2).reshape(B, L, H * d, S)


def kernel(x_prompt, x_sample, cache_mla_ckv, cache_mla_krope, cache_diff_k, cache_diff_v, cache_nat_k, cache_nat_v, c, c_ctx, w_ada, b_ada, ffn1_norm, ffn1_w_gate, ffn1_w_up, ffn1_w_down, mix_norm, w_in, mla_q_norm, mla_w_uq, mla_kv_norm, mla_w_ukv, diff_lambda_q1, diff_lambda_k1, diff_lambda_q2, diff_lambda_k2, diff_subln, nat_rpb, w_out, ffn2_norm, ffn2_w_gate, ffn2_w_up, ffn2_w_down, final_norm):
    L = w_ada.shape[0]
    B, S_ctx, _ = x_prompt.shape
    Bd, S_lat, _ = x_sample.shape
    assert S_ctx == CTX_SEQ and S_lat == LAT_SEQ and 1 + Bd <= ADA_ROWS
    assert cache_mla_ckv.shape[2] == CTX_SEQ and (B * CTX_SEQ) % FFN_TILE == 0

    cvec = jnp.concatenate([c_ctx[None, :], c, jnp.zeros((ADA_ROWS - 1 - Bd, D_MODEL), F32)], axis=0)
    ada5 = _ada_call(cvec, w_ada, b_ada).reshape(L, ADA_ROWS, 3, 3, D_MODEL)

    bf = lambda w: w.astype(BF16)
    win = _win_prep_call(w_in)
    wuq4 = mla_w_uq.reshape(L, MLA_Q_LORA, MLA_HEADS, MLA_NOPE + MLA_ROPE)
    wuq = bf(jnp.concatenate([wuq4[..., :MLA_NOPE].reshape(L, MLA_Q_LORA, -1),
                              wuq4[..., MLA_NOPE:].reshape(L, MLA_Q_LORA, -1)], axis=2))
    wukv4 = mla_w_ukv.reshape(L, MLA_KV_LORA, MLA_HEADS, MLA_NOPE + MLA_V)
    wukv = bf(jnp.concatenate([wukv4[..., :MLA_NOPE].reshape(L, MLA_KV_LORA, -1),
                               wukv4[..., MLA_NOPE:].reshape(L, MLA_KV_LORA, -1)], axis=2))
    wout = bf(w_out)
    lamv = jnp.stack([diff_lambda_q1, diff_lambda_k1, diff_lambda_q2, diff_lambda_k2], axis=1)
    subln = jnp.tile(diff_subln, (1, DIFF_HEADS)).reshape(L, HEAD_BLOCK, 1)
    r3 = lambda a: a.reshape(L, 1, a.shape[-1])
    n1, n2, nm, qn, kvn = r3(ffn1_norm), r3(ffn2_norm), r3(mix_norm), r3(mla_q_norm), r3(mla_kv_norm)
    fnorm = final_norm.reshape(1, D_MODEL)

    caches = (bf(cache_mla_ckv), jnp.swapaxes(cache_mla_krope, -1, -2),
              _feature_major(cache_diff_k), _feature_major(cache_diff_v),
              _feature_major(cache_nat_k), _feature_major(cache_nat_v))
    rope_tabs = (jnp.asarray(_ROPE_COS), jnp.asarray(_ROPE_SNEXT), jnp.asarray(_ROPE_SPREV))
    rpb_rows = _rpb_rows(nat_rpb)

    xi = x_prompt.reshape(B * CTX_SEQ, D_MODEL)
    xs = x_sample.reshape(Bd * LAT_SEQ, D_MODEL)
    nb = FFN_TILE // CTX_SEQ
    states = None
    for l in range(L):
        lam_init = 0.8 - 0.6 * math.exp(-0.3 * l)
        last = l == L - 1
        xi, xs, *st = _ffn_call(xi, xs, ada5, l, 0, n1, ffn1_w_gate, ffn1_w_up, ffn1_w_down, None, "ffn1",
                                fill_layers=L if l == 0 else 0)
        states = tuple(st) if l == 0 else states
        xi, states = _mix_i_call(xi, ada5, l, L, nm, win, qn, wuq, kvn, wukv, lamv, subln, wout, states, nb, lam_init)
        xs = _mix_ii_call(xs, ada5, l, nm, win, qn, wuq, kvn, wukv, lamv, subln, wout, rope_tabs,
                          rpb_rows, caches, lam_init)
        xi, xs = _ffn_call(xi, xs, ada5, l, 2, n2, ffn2_w_gate, ffn2_w_up, ffn2_w_down,
                           fnorm if last else None, "ffn2")

    states = (states[0],) + tuple(jnp.swapaxes(s, -1, -2) for s in states[1:])
    return (xi.reshape(B, CTX_SEQ, D_MODEL), xs.reshape(Bd, LAT_SEQ, D_MODEL)) + states
```

```python
import functools
import math

import numpy as np
import jax
import jax.numpy as jnp
from jax import lax
from jax.experimental import pallas as pl
from jax.experimental.pallas import tpu as pltpu

F32 = jnp.float32
BF16 = jnp.bfloat16

D_MODEL = 1024
FFN_DIM = 2816
NORM_EPS = 1e-6
ROPE_THETA = 10000.0
GRID_W = 64
N_ADA = 9
NEG_INF = -1e30

MLA_HEADS = 4
MLA_Q_LORA = 256
MLA_KV_LORA = 128
MLA_NOPE = 64
MLA_ROPE = 32
MLA_V = 64
DIFF_HEADS = 4
DIFF_DK = 32
DIFF_DV = 64
NAT_HEADS = 8
NAT_HD = 64
NAT_WIN_ROWS = 8
NAT_WIN_COLS = 16

LOG2E = math.log2(math.e)
MLA_SCALE = (MLA_NOPE + MLA_ROPE) ** -0.5 * LOG2E
DIFF_SCALE = DIFF_DK ** -0.5 * LOG2E
NAT_SCALE = NAT_HD ** -0.5 * LOG2E

CTX_SEQ = 256
LAT_SEQ = 1024
LAT_ROWS = LAT_SEQ // GRID_W

LANES = 128
MXU_DIM = 256
VMEM_LIMIT_BYTES = 58 * 1024 * 1024

P_CQ = 0
P_CKV = 256
P_KR = 384
P_DQ = 512
P_DK = 768
P_DV = 1024
P_NQ = 1280
P_NK = 1792
P_NV = 2304
P_COLS = 2816
P_SPLIT = 512
HEAD_BLOCK = 256
ADA_ROWS = 8

FFN_CHUNKS = ((0, 512), (512, 1024), (1024, 1536), (1536, 2048), (2048, 2560), (2560, 2816))
FFN_TILE = 512
N_STATES = 6
MIX_CTX_BATCH = 4


def _rope_tables():
    t = np.arange(LAT_SEQ)
    pos = np.stack([t // GRID_W, t % GRID_W], axis=0).astype(np.float64)
    lane = np.arange(LANES)
    p = lane % 32
    axis = (p >= 16).astype(np.int64)
    freqs = ROPE_THETA ** (-(p % 8).astype(np.float64) / 8.0)
    ang = pos[axis, :].T * freqs[None, :]
    first = (p % 16) < 8
    cos = np.cos(ang)
    sin = np.sin(ang)
    s_next = np.where(first[None, :], -sin, 0.0)
    s_prev = np.where(first[None, :], 0.0, sin)
    return cos.astype(np.float32), s_next.astype(np.float32), s_prev.astype(np.float32)


_ROPE_COS, _ROPE_SNEXT, _ROPE_SPREV = _rope_tables()


def _rmsnorm(x, g):
    ms = jnp.mean(x * x, axis=-1, keepdims=True)
    return x * lax.rsqrt(ms + NORM_EPS) * g


def _silu(x):
    return x / (1.0 + jnp.exp(-x))


def _dot(a, b):
    return jnp.dot(a, b, preferred_element_type=F32)


def _dot_t(a, b):
    return lax.dot_general(a, b, (((1,), (1,)), ((), ())), preferred_element_type=F32)


class _Proj:
    def __init__(self, a, b):
        self.a, self.b = a, b

    def __getitem__(self, idx):
        lo, hi = idx[1].start, idx[1].stop
        return self.a[:, lo:hi] if hi <= P_SPLIT else self.b[:, lo - P_SPLIT:hi - P_SPLIT]


def _lane_mask(width, lo, hi):
    lane = lax.broadcasted_iota(jnp.int32, (1, width), 1)
    return (lane >= lo) & (lane < hi)


def _head_masks(width=HEAD_BLOCK, group=64, n=4):
    return [_lane_mask(width, h * group, (h + 1) * group) for h in range(n)]


def _mla_qmasks():
    lane = lax.broadcasted_iota(jnp.int32, (1, HEAD_BLOCK + LANES), 1)
    out = []
    for h in range(MLA_HEADS):
        nope = (lane >= h * MLA_NOPE) & (lane < (h + 1) * MLA_NOPE)
        rope = (lane >= HEAD_BLOCK + h * MLA_ROPE) & (lane < HEAD_BLOCK + (h + 1) * MLA_ROPE)
        out.append(nope | rope)
    return out


def _tile32(blk):
    return blk + pltpu.roll(blk, 32, 1) + pltpu.roll(blk, 64, 1) + pltpu.roll(blk, 96, 1)


def _rope(x, cos, s_next, s_prev):
    return x * cos + pltpu.roll(x, LANES - 8, 1) * s_next + pltpu.roll(x, 8, 1) * s_prev


def _stack_masked(q, masks):
    zero = jnp.zeros_like(q)
    return jnp.concatenate([jnp.where(m, q, zero) for m in masks], axis=0)


def _scores_t(q, k_bf, kmasks, stacked):
    sk = k_bf.shape[0]
    if stacked:
        s = _dot_t(_stack_masked(k_bf, kmasks), q)
        return [s[h * sk:(h + 1) * sk] for h in range(len(kmasks))]
    zero = jnp.zeros_like(k_bf)
    return [_dot_t(jnp.where(m, k_bf, zero), q) for m in kmasks]


def _pv_t(vt_bf, p_bf, h, short_keys):
    if short_keys:
        return _dot(vt_bf, p_bf)[h * 64:(h + 1) * 64]
    return _dot(vt_bf[h * 64:(h + 1) * 64], p_bf)


def _exp_sum_t(s):
    e = jnp.exp2(s - jnp.max(s, axis=0, keepdims=True))
    return e, jnp.sum(e, axis=0, keepdims=True)


def _softmax_heads_t(q, k_bf, vt_bf, kmasks, stacked):
    parts = []
    for h, s in enumerate(_scores_t(q, k_bf, kmasks, stacked)):
        e, l = _exp_sum_t(s)
        parts.append(_pv_t(vt_bf, e.astype(BF16), h, stacked) / l)
    return jnp.concatenate(parts, axis=0)


def _diff_heads_t(q, k_bf, vt_bf, lam, subln_col, lam_init, stacked):
    n = DIFF_HEADS
    m1 = [_lane_mask(HEAD_BLOCK, h * 2 * DIFF_DK, h * 2 * DIFF_DK + DIFF_DK) for h in range(n)]
    m2 = [_lane_mask(HEAD_BLOCK, h * 2 * DIFF_DK + DIFF_DK, (h + 1) * 2 * DIFF_DK) for h in range(n)]
    s = _scores_t(q, k_bf, m1 + m2, stacked)
    parts = []
    for h in range(n):
        e1, l1 = _exp_sum_t(s[h])
        e2, l2 = _exp_sum_t(s[n + h])
        p = e1 * (1.0 / l1) - e2 * (lam / l2)
        oh = _pv_t(vt_bf, p.astype(BF16), h, stacked)
        ms = jnp.mean(oh * oh, axis=0, keepdims=True)
        parts.append(oh * lax.rsqrt(ms + NORM_EPS))
    return jnp.concatenate(parts, axis=0) * subln_col * (1.0 - lam_init)


def _lambda(lamv, lam_init):
    a = jnp.sum(lamv[0:1] * lamv[1:2], axis=-1, keepdims=True)
    b = jnp.sum(lamv[2:3] * lamv[3:4], axis=-1, keepdims=True)
    return jnp.exp(a) - jnp.exp(b) + lam_init


def _ada_block(c_ref, w_ref, b_ref):
    return _dot(_silu(c_ref[...]).astype(BF16), w_ref[...].astype(BF16)) + b_ref[...]


def _ada_kernel(c_ref, w_ref, b_ref, o_ref):
    o_ref[...] = _ada_block(c_ref, w_ref, b_ref)


def _ada_call(cvec, w_ada, b_ada3, layer):
    n = w_ada.shape[2]
    tn = 1024
    return pl.pallas_call(
        _ada_kernel,
        grid=(n // tn,),
        in_specs=[
            pl.BlockSpec((ADA_ROWS, D_MODEL), lambda j: (0, 0)),
            pl.BlockSpec((None, D_MODEL, tn), lambda j: (layer, 0, j)),
            pl.BlockSpec((None, 1, tn), lambda j: (layer, 0, j)),
        ],
        out_specs=pl.BlockSpec((ADA_ROWS, tn), lambda j: (0, j)),
        out_shape=jax.ShapeDtypeStruct((ADA_ROWS, n), F32),
        compiler_params=pltpu.CompilerParams(
            dimension_semantics=("arbitrary",), vmem_limit_bytes=VMEM_LIMIT_BYTES),
        name="ada",
    )(cvec, w_ada, b_ada3)


def _ffn_tile(x_ref, o_ref, mod, g_ref, wg_ref, wu_ref, wd_ref, fn_ref):
    x = x_ref[...]
    h = (_rmsnorm(x, g_ref[...]) * (1.0 + mod[1:2]) + mod[0:1]).astype(BF16)
    acc = None
    for lo, hi in FFN_CHUNKS:
        g = _dot(h, wg_ref[:, lo:hi].astype(BF16))
        u = _dot(h, wu_ref[:, lo:hi].astype(BF16))
        part = _dot((_silu(g) * u).astype(BF16), wd_ref[lo:hi, :].astype(BF16))
        acc = part if acc is None else acc + part
    y = x + (0.5 * mod[2:3]) * acc
    if fn_ref is not None:
        y = _rmsnorm(y, fn_ref[...])
    o_ref[...] = y


def _state_fill_copies(t, state_refs, zero_refs, sem, n_layers):
    per_tile = FFN_TILE // CTX_SEQ
    copies = []
    for j in range(per_tile):
        for l in range(n_layers):
            for st, z in zip(state_refs, zero_refs):
                dst = st.at[t * per_tile + j, l]
                if dst.shape == z.shape:
                    copies.append(pltpu.make_async_copy(z, dst, sem))
                else:
                    nh = z.shape[0]
                    copies += [pltpu.make_async_copy(z, dst.at[h:h + nh], sem) for h in range(0, dst.shape[0], nh)]
    return copies


def _ffn_kernel(xc_ref, xl_ref, mod_ref, g_ref, wg_ref, wu_ref, wd_ref, *rest,
                n_ctx_tiles, final, fill_layers, next_ada):
    rest = list(rest)
    fn_ref = rest.pop(0) if final else None
    ada_in = [rest.pop(0) for _ in range(3)] if next_ada else None
    oc_ref, ol_ref = rest.pop(0), rest.pop(0)
    state_refs = [rest.pop(0) for _ in range(N_STATES)] if fill_layers else None
    if next_ada:
        rest.pop(0)[...] = _ada_block(*ada_in)
    mod = mod_ref[...]
    t = pl.program_id(0)
    if fill_layers:
        z_ckv, z_kr, z_heads, zsem = rest

        @pl.when(t == 0)
        def _():
            for z in (z_ckv, z_kr, z_heads):
                z[...] = jnp.zeros(z.shape, z.dtype)

    @pl.when(t < n_ctx_tiles)
    def _():
        fill = []
        if fill_layers:
            fill = _state_fill_copies(t, state_refs, (z_ckv, z_kr, z_heads, z_heads, z_heads, z_heads), zsem,
                                      fill_layers)
        for c in fill:
            c.start()
        _ffn_tile(xc_ref, oc_ref, mod, g_ref, wg_ref, wu_ref, wd_ref, fn_ref)
        for c in fill:
            c.wait()

    @pl.when(t >= n_ctx_tiles)
    def _():
        _ffn_tile(xl_ref, ol_ref, mod, g_ref, wg_ref, wu_ref, wd_ref, fn_ref)


def _resident(shape, index_map):
    return pl.BlockSpec(shape, index_map, pipeline_mode=pl.Buffered(1))


def _state_shapes(B, L):
    return [jax.ShapeDtypeStruct(s, F32) for s in (
        (B, L, CTX_SEQ, MLA_KV_LORA), (B, L, MLA_ROPE, CTX_SEQ),
        (B, L, DIFF_HEADS, 2 * DIFF_DK, CTX_SEQ), (B, L, DIFF_HEADS, DIFF_DV, CTX_SEQ),
        (B, L, NAT_HEADS, NAT_HD, CTX_SEQ), (B, L, NAT_HEADS, NAT_HD, CTX_SEQ))]


def _ffn_call(xc, xl, ada4, layer, group, norm, wg, wu, wd, final_norm, name, fill_layers=0, next_ada=None):
    tm = FFN_TILE
    n_ctx = xc.shape[0] // tm
    n_lat = xl.shape[0] // tm
    per_lat = LAT_SEQ // tm
    ctx_blk = lambda t: (jnp.minimum(t, n_ctx - 1), 0)
    lat_blk = lambda t: (jnp.maximum(t - n_ctx, 0), 0)
    ada_row = lambda t: jnp.maximum(t - n_ctx, -per_lat) // per_lat + 1
    in_specs = [
        pl.BlockSpec((tm, D_MODEL), ctx_blk),
        pl.BlockSpec((tm, D_MODEL), lat_blk),
        pl.BlockSpec((None, None, 3, D_MODEL), lambda t: (ada_row(t), group, 0, 0)),
        _resident((None, 1, D_MODEL), lambda t: (layer, 0, 0)),
        _resident((None, D_MODEL, FFN_DIM), lambda t: (layer, 0, 0)),
        _resident((None, D_MODEL, FFN_DIM), lambda t: (layer, 0, 0)),
        _resident((None, FFN_DIM, D_MODEL), lambda t: (layer, 0, 0)),
    ]
    args = [xc, xl, ada4, norm, wg, wu, wd]
    if final_norm is not None:
        in_specs.append(_resident((1, D_MODEL), lambda t: (0, 0)))
        args.append(final_norm)
    if next_ada is not None:
        n_ada = next_ada[1].shape[2]
        tn = n_ada // (n_ctx + n_lat)
        assert tn * (n_ctx + n_lat) == n_ada and tn % LANES == 0
        in_specs += [_resident((ADA_ROWS, D_MODEL), lambda t: (0, 0)),
                     pl.BlockSpec((None, D_MODEL, tn), lambda t: (layer + 1, 0, t)),
                     pl.BlockSpec((None, 1, tn), lambda t: (layer + 1, 0, t))]
        args += list(next_ada)
    out_specs = [pl.BlockSpec((tm, D_MODEL), ctx_blk), pl.BlockSpec((tm, D_MODEL), lat_blk)]
    out_shape = [jax.ShapeDtypeStruct(xc.shape, F32), jax.ShapeDtypeStruct(xl.shape, F32)]
    scratch = []
    if fill_layers:
        states = _state_shapes(xc.shape[0] // CTX_SEQ, fill_layers)
        out_specs += [pl.BlockSpec(memory_space=pl.ANY)] * N_STATES
        out_shape += states
        scratch += [pltpu.VMEM(states[k].shape[2:], F32) for k in (0, 1, 2)] + [pltpu.SemaphoreType.DMA(())]
    if next_ada is not None:
        out_specs.append(pl.BlockSpec((ADA_ROWS, tn), lambda t: (0, t)))
        out_shape.append(jax.ShapeDtypeStruct((ADA_ROWS, n_ada), F32))
    return pl.pallas_call(
        functools.partial(_ffn_kernel, n_ctx_tiles=n_ctx, final=final_norm is not None, fill_layers=fill_layers,
                          next_ada=next_ada is not None),
        grid=(n_ctx + n_lat,),
        in_specs=in_specs,
        out_specs=out_specs,
        out_shape=out_shape,
        scratch_shapes=scratch,
        compiler_params=pltpu.CompilerParams(
            dimension_semantics=("arbitrary",), vmem_limit_bytes=VMEM_LIMIT_BYTES),
        name=name,
    )(*args)


def _mix_i_kernel(x_ref, mod_ref, g_ref, wina_ref, winb_ref, qn_ref, wuq_ref, kvn_ref, wukv_ref, lamv_ref, subln_ref,
                  wout_ref, *refs, nb, lam_init):
    y_ref, ckv_ref, kr_ref, dk_ref, dv_ref, nk_ref, nv_ref = refs[-7:]
    S = CTX_SEQ
    x = x_ref[...]
    mod = mod_ref[...]
    h = (_rmsnorm(x, g_ref[...]) * (1.0 + mod[1:2]) + mod[0:1]).astype(BF16)
    proj = _Proj(_dot(h, wina_ref[...]), _dot(h, winb_ref[...]))
    q_cat = _dot(_rmsnorm(proj[:, P_CQ:P_CQ + MLA_Q_LORA], qn_ref[...]).astype(BF16), wuq_ref[...])
    ckv = _rmsnorm(proj[:, P_CKV:P_CKV + MLA_KV_LORA], kvn_ref[...])
    kv = _dot(ckv.astype(BF16), wukv_ref[...])
    kr_blk = proj[:, P_KR:P_KR + LANES]
    k_cat = jnp.concatenate([kv[:, 0:HEAD_BLOCK], _tile32(kr_blk)], axis=1).astype(BF16)
    v_mla = kv[:, HEAD_BLOCK:2 * HEAD_BLOCK]
    q_cat = (q_cat * MLA_SCALE).astype(BF16)
    dq = (proj[:, P_DQ:P_DQ + HEAD_BLOCK] * DIFF_SCALE).astype(BF16)
    dk = proj[:, P_DK:P_DK + HEAD_BLOCK]
    dv = proj[:, P_DV:P_DV + HEAD_BLOCK]
    nq = (proj[:, P_NQ:P_NQ + 2 * HEAD_BLOCK] * NAT_SCALE).astype(BF16)
    nk = proj[:, P_NK:P_NK + 2 * HEAD_BLOCK]
    nv = proj[:, P_NV:P_NV + 2 * HEAD_BLOCK]
    dk_bf, nk_bf = dk.astype(BF16), nk.astype(BF16)
    lam = _lambda(lamv_ref[...], lam_init)
    hmasks = _head_masks()
    mla_qm = _mla_qmasks()

    outs = []
    for j in range(nb):
        r0, r1 = j * S, (j + 1) * S
        ckv_ref[j] = ckv[r0:r1]
        kr_ref[j] = kr_blk[r0:r1].T[0:MLA_ROPE]
        dk_t, dv_t = dk[r0:r1].T, dv[r0:r1].T
        nk_t = [nk[r0:r1, b * HEAD_BLOCK:(b + 1) * HEAD_BLOCK].T for b in range(2)]
        nv_t = [nv[r0:r1, b * HEAD_BLOCK:(b + 1) * HEAD_BLOCK].T for b in range(2)]
        for hh in range(DIFF_HEADS):
            dk_ref[j, hh] = dk_t[hh * 64:(hh + 1) * 64]
            dv_ref[j, hh] = dv_t[hh * 64:(hh + 1) * 64]
        for hh in range(NAT_HEADS):
            nk_ref[j, hh] = nk_t[hh // 4][(hh % 4) * 64:(hh % 4 + 1) * 64]
            nv_ref[j, hh] = nv_t[hh // 4][(hh % 4) * 64:(hh % 4 + 1) * 64]
        ot_mla = _softmax_heads_t(q_cat[r0:r1], k_cat[r0:r1], v_mla[r0:r1].T.astype(BF16), mla_qm, True)
        ot_diff = _diff_heads_t(dq[r0:r1], dk_bf[r0:r1], dv_t.astype(BF16), lam, subln_ref[...], lam_init, True)
        ot_nat = [
            _softmax_heads_t(nq[r0:r1, b * HEAD_BLOCK:(b + 1) * HEAD_BLOCK], nk_bf[r0:r1, b * HEAD_BLOCK:(b + 1) * HEAD_BLOCK],
                             nv_t[b].astype(BF16), hmasks, True)
            for b in range(2)
        ]
        outs.append(jnp.concatenate([ot_mla, ot_diff] + ot_nat, axis=0).T.astype(BF16))
    o = jnp.concatenate(outs, axis=0) if nb > 1 else outs[0]
    y_ref[...] = x + mod[2:3] * _dot(o, wout_ref[...])


def _mix_i_call(x, ada4, layer, n_layers, mix_norm, win, qn, wuq, kvn, wukv, lamv, subln, wout, states, nb, lam_init):
    T = x.shape[0]
    B = T // CTX_SEQ
    tm = nb * CTX_SEQ
    lyr = lambda shape: _resident((None,) + shape, lambda i: (layer,) + (0,) * len(shape))
    any_spec = pl.BlockSpec(memory_space=pl.ANY)
    in_specs = [
        pl.BlockSpec((tm, D_MODEL), lambda i: (i, 0)),
        _resident((None, None, 3, D_MODEL), lambda i: (0, 1, 0, 0)),
        lyr((1, D_MODEL)),
        lyr((D_MODEL, P_SPLIT)),
        lyr((D_MODEL, P_COLS - P_SPLIT)),
        lyr((1, MLA_Q_LORA)),
        lyr((MLA_Q_LORA, HEAD_BLOCK + LANES)),
        lyr((1, MLA_KV_LORA)),
        lyr((MLA_KV_LORA, 2 * HEAD_BLOCK)),
        lyr((4, DIFF_DK)),
        lyr((HEAD_BLOCK, 1)),
        lyr((D_MODEL, D_MODEL)),
    ]
    n_fixed = len(in_specs)
    in_specs += [any_spec] * len(states)
    out_specs = [
        pl.BlockSpec((tm, D_MODEL), lambda i: (i, 0)),
        pl.BlockSpec((nb, None, CTX_SEQ, MLA_KV_LORA), lambda i: (i, layer, 0, 0)),
        pl.BlockSpec((nb, None, MLA_ROPE, CTX_SEQ), lambda i: (i, layer, 0, 0)),
        pl.BlockSpec((nb, None, DIFF_HEADS, 2 * DIFF_DK, CTX_SEQ), lambda i: (i, layer, 0, 0, 0)),
        pl.BlockSpec((nb, None, DIFF_HEADS, DIFF_DV, CTX_SEQ), lambda i: (i, layer, 0, 0, 0)),
        pl.BlockSpec((nb, None, NAT_HEADS, NAT_HD, CTX_SEQ), lambda i: (i, layer, 0, 0, 0)),
        pl.BlockSpec((nb, None, NAT_HEADS, NAT_HD, CTX_SEQ), lambda i: (i, layer, 0, 0, 0)),
    ]
    out_shape = [jax.ShapeDtypeStruct((T, D_MODEL), F32)] + _state_shapes(B, n_layers)
    res = pl.pallas_call(
        functools.partial(_mix_i_kernel, nb=nb, lam_init=lam_init),
        grid=(B // nb,),
        in_specs=in_specs,
        out_specs=out_specs,
        out_shape=out_shape,
        input_output_aliases={n_fixed + k: 1 + k for k in range(len(states))},
        compiler_params=pltpu.CompilerParams(
            dimension_semantics=("arbitrary",), vmem_limit_bytes=VMEM_LIMIT_BYTES),
        name="mix_ctx",
    )(x, ada4, mix_norm, *win, qn, wuq, kvn, wukv, lamv, subln, wout, *states)
    return res[0], tuple(res[1:])


PROJ_ROWS = 256
DENSE_QROWS = MXU_DIM
N_KEYS = CTX_SEQ + LAT_SEQ
BIAS_LANES = NAT_WIN_ROWS * LANES


def _mix_ii_kernel(x_ref, mod_ref, g_ref, wina_ref, winb_ref, qn_ref, wuq_ref, kvn_ref, wukv_ref, lamv_ref, subln_ref,
                   wout_ref, cos_ref, snext_ref, sprev_ref, rpb_ref,
                   cckv_ref, ckr_ref, cdk_ref, cdv_ref, cnk_ref, cnv_ref,
                   y_ref,
                   qcat_s, ckv_s, kcat_s, vmla_s, dq_s, dk_s, dv_s, nq_s, nk_s, nv_s, o_s, bias_ref,
                   vmlat_s, dvt_s, cnv_s, *, lam_init):
    S = LAT_SEQ
    C = CTX_SEQ
    mod = mod_ref[...]
    lam = _lambda(lamv_ref[...], lam_init)
    hmasks = _head_masks()
    mla_qm = _mla_qmasks()

    @pl.when(pl.program_id(0) == 0)
    def _():
        qc = lax.broadcasted_iota(jnp.int32, (GRID_W, LANES), 0)
        kc = lax.broadcasted_iota(jnp.int32, (GRID_W, LANES), 1) % GRID_W
        c_start = jnp.clip(qc - NAT_WIN_COLS // 2, 0, GRID_W - NAT_WIN_COLS)
        in_win = (kc >= c_start) & (kc < c_start + NAT_WIN_COLS)
        for par in range(2):
            for h in range(NAT_HEADS):
                for p in range(NAT_WIN_ROWS):
                    row = jnp.broadcast_to(rpb_ref[par, h, p:p + 1, :], (GRID_W, LANES))
                    tile = pltpu.roll(row, LANES - (NAT_WIN_COLS - 1), 1, stride=1, stride_axis=0)
                    bias_ref[par, h // 4, (h % 4) * GRID_W:(h % 4 + 1) * GRID_W, p * LANES:(p + 1) * LANES] = (
                        jnp.where(in_win, tile * LOG2E, NEG_INF))

    ckv_s[0:C, :] = cckv_ref[...]
    kr_t = ckr_ref[...]
    kcat_s[0:C, HEAD_BLOCK:] = jnp.concatenate([kr_t] * (LANES // MLA_ROPE), axis=0).T.astype(BF16)
    dk_s[0:C, :] = cdk_ref[...].T.astype(BF16)
    dvt_s[:, 0:C] = cdv_ref[...].astype(BF16)
    cnv_s[...] = cnv_ref[...].T.astype(BF16)

    def proj_body(i, carry):
        r = pl.multiple_of(i * PROJ_ROWS, PROJ_ROWS)
        rows = pl.ds(r, PROJ_ROWS)
        krows = pl.ds(C + r, PROJ_ROWS)
        cos, s_next, s_prev = cos_ref[rows, :], snext_ref[rows, :], sprev_ref[rows, :]
        rope = lambda v: _rope(v, cos, s_next, s_prev)
        x = x_ref[rows, :]
        h = (_rmsnorm(x, g_ref[...]) * (1.0 + mod[1:2]) + mod[0:1]).astype(BF16)
        proj = _Proj(_dot(h, wina_ref[...]), _dot(h, winb_ref[...]))
        q_cat = _dot(_rmsnorm(proj[:, P_CQ:P_CQ + MLA_Q_LORA], qn_ref[...]).astype(BF16), wuq_ref[...])
        qcat_s[rows, 0:HEAD_BLOCK] = (q_cat[:, 0:HEAD_BLOCK] * MLA_SCALE).astype(BF16)
        qcat_s[rows, HEAD_BLOCK:] = (rope(q_cat[:, HEAD_BLOCK:]) * MLA_SCALE).astype(BF16)
        ckv_s[krows, :] = _rmsnorm(proj[:, P_CKV:P_CKV + MLA_KV_LORA], kvn_ref[...]).astype(BF16)
        kcat_s[krows, HEAD_BLOCK:] = _tile32(rope(proj[:, P_KR:P_KR + LANES])).astype(BF16)
        for b in range(2):
            c0 = b * LANES
            dq_s[rows, c0:c0 + LANES] = (rope(proj[:, P_DQ + c0:P_DQ + c0 + LANES]) * DIFF_SCALE).astype(BF16)
            dk_s[krows, c0:c0 + LANES] = rope(proj[:, P_DK + c0:P_DK + c0 + LANES]).astype(BF16)
        dv_s[krows, :] = proj[:, P_DV:P_DV + HEAD_BLOCK].astype(BF16)
        nq_s[rows, :] = (proj[:, P_NQ:P_NQ + 2 * HEAD_BLOCK] * NAT_SCALE).astype(BF16)
        nk_s[rows, :] = proj[:, P_NK:P_NK + 2 * HEAD_BLOCK].astype(BF16)
        nv_s[rows, :] = proj[:, P_NV:P_NV + 2 * HEAD_BLOCK].astype(BF16)
        return carry

    lax.fori_loop(0, S // PROJ_ROWS, proj_body, 0)

    def kv_body(i, carry):
        rows = pl.ds(pl.multiple_of(i * PROJ_ROWS, PROJ_ROWS), PROJ_ROWS)
        kv = _dot(ckv_s[rows, :], wukv_ref[...])
        kcat_s[rows, 0:HEAD_BLOCK] = kv[:, 0:HEAD_BLOCK].astype(BF16)
        vmla_s[rows, :] = kv[:, HEAD_BLOCK:].astype(BF16)
        return carry

    lax.fori_loop(0, N_KEYS // PROJ_ROWS, kv_body, 0)

    for j in range(N_KEYS // PROJ_ROWS):
        c0, c1 = j * PROJ_ROWS, (j + 1) * PROJ_ROWS
        vmlat_s[:, c0:c1] = vmla_s[c0:c1, :].astype(F32).T.astype(BF16)
        if c0 >= C:
            dvt_s[:, c0:c1] = dv_s[c0:c1, :].astype(F32).T.astype(BF16)

    def dense_body(i, carry):
        rows = pl.ds(pl.multiple_of(i * DENSE_QROWS, DENSE_QROWS), DENSE_QROWS)
        ot_mla = _softmax_heads_t(qcat_s[rows, :], kcat_s[...], vmlat_s[...], mla_qm, False)
        o_s[rows, 0:HEAD_BLOCK] = ot_mla.T.astype(BF16)
        ot_diff = _diff_heads_t(dq_s[rows, :], dk_s[...], dvt_s[...], lam, subln_ref[...], lam_init, False)
        o_s[rows, HEAD_BLOCK:2 * HEAD_BLOCK] = ot_diff.T.astype(BF16)
        return carry

    lax.fori_loop(0, S // DENSE_QROWS, dense_body, 0)

    for r in range(LAT_ROWS):
        rs = min(max(r - NAT_WIN_ROWS // 2, 0), LAT_ROWS - NAT_WIN_ROWS)
        dr0 = rs - r + (NAT_WIN_ROWS - 1)
        par = dr0 % 2
        off = GRID_W * (dr0 - par)
        q0, q1 = r * GRID_W, (r + 1) * GRID_W
        k0, k1 = rs * GRID_W, (rs + NAT_WIN_ROWS) * GRID_W
        for b in range(2):
            c0, c1 = b * HEAD_BLOCK, (b + 1) * HEAD_BLOCK
            q = nq_s[q0:q1, c0:c1]
            zero = jnp.zeros_like(q)
            qs = jnp.concatenate([jnp.where(m, q, zero) for m in hmasks], axis=0)
            sw = _dot_t(qs, nk_s[k0:k1, c0:c1]) + bias_ref[par, b, :, off:off + NAT_WIN_ROWS * GRID_W]
            sc = _dot(qs, cnk_ref[c0:c1, :].astype(BF16))
            m = jnp.maximum(jnp.max(sw, axis=-1, keepdims=True), jnp.max(sc, axis=-1, keepdims=True))
            ew = jnp.exp2(sw - m)
            ec = jnp.exp2(sc - m)
            l = jnp.sum(ew, axis=-1, keepdims=True) + jnp.sum(ec, axis=-1, keepdims=True)
            o = (_dot(ew.astype(BF16), nv_s[k0:k1, c0:c1]) + _dot(ec.astype(BF16), cnv_s[:, c0:c1])) / l
            of = None
            for hh, hm in enumerate(hmasks):
                part = jnp.where(hm, o[hh * GRID_W:(hh + 1) * GRID_W], 0.0)
                of = part if of is None else of + part
            o_s[q0:q1, 2 * HEAD_BLOCK + c0:2 * HEAD_BLOCK + c1] = of.astype(BF16)

    def out_body(i, carry):
        rows = pl.ds(pl.multiple_of(i * PROJ_ROWS, PROJ_ROWS), PROJ_ROWS)
        y_ref[rows, :] = x_ref[rows, :] + mod[2:3] * _dot(o_s[rows, :], wout_ref[...])
        return carry

    lax.fori_loop(0, S // PROJ_ROWS, out_body, 0)


def _mix_ii_call(x, ada4, layer, mix_norm, win, qn, wuq, kvn, wukv, lamv, subln, wout, rope_tabs, rpb_rows,
                 caches, lam_init):
    T = x.shape[0]
    Bd = T // LAT_SEQ
    cckv, ckr, cdk, cdv, cnk, cnv = caches
    lyr = lambda shape: _resident((None,) + shape, lambda i: (layer,) + (0,) * len(shape))
    cache = lambda rows, width: pl.BlockSpec((None, None, rows, width), lambda i: (i, layer, 0, 0),
                                             pipeline_mode=pl.Buffered(1))
    tab = _resident((LAT_SEQ, LANES), lambda i: (0, 0))
    in_specs = [
        pl.BlockSpec((LAT_SEQ, D_MODEL), lambda i: (i, 0), pipeline_mode=pl.Buffered(1)),
        pl.BlockSpec((None, None, 3, D_MODEL), lambda i: (i + 1, 1, 0, 0)),
        lyr((1, D_MODEL)),
        lyr((D_MODEL, P_SPLIT)),
        lyr((D_MODEL, P_COLS - P_SPLIT)),
        lyr((1, MLA_Q_LORA)),
        lyr((MLA_Q_LORA, HEAD_BLOCK + LANES)),
        lyr((1, MLA_KV_LORA)),
        lyr((MLA_KV_LORA, 2 * HEAD_BLOCK)),
        lyr((4, DIFF_DK)),
        lyr((HEAD_BLOCK, 1)),
        lyr((D_MODEL, D_MODEL)),
        tab, tab, tab,
        lyr((2, NAT_HEADS, NAT_WIN_ROWS, LANES)),
        cache(CTX_SEQ, MLA_KV_LORA), cache(MLA_ROPE, CTX_SEQ), cache(HEAD_BLOCK, CTX_SEQ), cache(HEAD_BLOCK, CTX_SEQ),
        cache(2 * HEAD_BLOCK, CTX_SEQ), cache(2 * HEAD_BLOCK, CTX_SEQ),
    ]
    scratch = [
        pltpu.VMEM((LAT_SEQ, HEAD_BLOCK + LANES), BF16),
        pltpu.VMEM((N_KEYS, MLA_KV_LORA), BF16),
        pltpu.VMEM((N_KEYS, HEAD_BLOCK + LANES), BF16),
        pltpu.VMEM((N_KEYS, HEAD_BLOCK), BF16),
        pltpu.VMEM((LAT_SEQ, HEAD_BLOCK), BF16),
        pltpu.VMEM((N_KEYS, HEAD_BLOCK), BF16),
        pltpu.VMEM((N_KEYS, HEAD_BLOCK), BF16),
        pltpu.VMEM((LAT_SEQ, 2 * HEAD_BLOCK), BF16),
        pltpu.VMEM((LAT_SEQ, 2 * HEAD_BLOCK), BF16),
        pltpu.VMEM((LAT_SEQ, 2 * HEAD_BLOCK), BF16),
        pltpu.VMEM((LAT_SEQ, D_MODEL), BF16),
        pltpu.VMEM((2, 2, HEAD_BLOCK, BIAS_LANES), F32),
        pltpu.VMEM((HEAD_BLOCK, N_KEYS), BF16),
        pltpu.VMEM((HEAD_BLOCK, N_KEYS), BF16),
        pltpu.VMEM((CTX_SEQ, 2 * HEAD_BLOCK), BF16),
    ]
    return pl.pallas_call(
        functools.partial(_mix_ii_kernel, lam_init=lam_init),
        grid=(Bd,),
        in_specs=in_specs,
        out_specs=pl.BlockSpec((LAT_SEQ, D_MODEL), lambda i: (i, 0)),
        out_shape=jax.ShapeDtypeStruct((T, D_MODEL), F32),
        scratch_shapes=scratch,
        compiler_params=pltpu.CompilerParams(
            dimension_semantics=("arbitrary",), vmem_limit_bytes=VMEM_LIMIT_BYTES),
        name="mix_lat",
    )(x, ada4, mix_norm, *win, qn, wuq, kvn, wukv, lamv, subln, wout, *rope_tabs, rpb_rows, *caches)


W_IN_SPLIT = MLA_Q_LORA + MLA_KV_LORA + MLA_ROPE


def _win_prep_kernel(wt_ref, a_ref, b_ref):
    tr = lambda r0: wt_ref[r0:r0 + MXU_DIM, :].T
    a_ref[:, 0:MXU_DIM] = tr(0).astype(BF16)
    live = _lane_mask(MXU_DIM, 0, W_IN_SPLIT - MXU_DIM)
    a_ref[:, MXU_DIM:P_SPLIT] = jnp.where(live, tr(MXU_DIM), 0.0).astype(BF16)
    for j in range((P_COLS - P_SPLIT) // MXU_DIM):
        b_ref[:, j * MXU_DIM:(j + 1) * MXU_DIM] = tr(W_IN_SPLIT + j * MXU_DIM).astype(BF16)


def _win_prep_call(w_in):
    L, D, n = w_in.shape
    assert n - W_IN_SPLIT == P_COLS - P_SPLIT and P_SPLIT == 2 * MXU_DIM
    return pl.pallas_call(
        _win_prep_kernel,
        grid=(L,),
        in_specs=[pl.BlockSpec((None, n, D), lambda l: (l, 0, 0))],
        out_specs=[pl.BlockSpec((None, D, P_SPLIT), lambda l: (l, 0, 0)),
                   pl.BlockSpec((None, D, P_COLS - P_SPLIT), lambda l: (l, 0, 0))],
        out_shape=[jax.ShapeDtypeStruct((L, D, P_SPLIT), BF16), jax.ShapeDtypeStruct((L, D, P_COLS - P_SPLIT), BF16)],
        compiler_params=pltpu.CompilerParams(
            dimension_semantics=("arbitrary",), vmem_limit_bytes=VMEM_LIMIT_BYTES),
        name="win_prep",
    )(jnp.swapaxes(w_in, 1, 2))


def _rpb_rows(rpb):
    L = rpb.shape[0]
    n_dr, n_dc = 2 * NAT_WIN_ROWS - 1, 2 * NAT_WIN_COLS - 1
    padded = jnp.pad(rpb.astype(F32), ((0, 0), (0, 0), (0, 2 * NAT_WIN_ROWS + 1 - n_dr), (0, GRID_W - n_dc)))
    even = padded[:, :, 0:2 * NAT_WIN_ROWS].reshape(L, NAT_HEADS, NAT_WIN_ROWS, LANES)
    odd = padded[:, :, 1:2 * NAT_WIN_ROWS + 1].reshape(L, NAT_HEADS, NAT_WIN_ROWS, LANES)
    return jnp.stack([even, odd], axis=1)


def _feature_major(cache):
    B, L, H, S, d = cache.shape
    return jnp.swapaxes(cache, -1, -2).reshape(B, L, H * d, S)


def kernel(x_prompt, x_sample, cache_mla_ckv, cache_mla_krope, cache_diff_k, cache_diff_v, cache_nat_k, cache_nat_v, c, c_ctx, w_ada, b_ada, ffn1_norm, ffn1_w_gate, ffn1_w_up, ffn1_w_down, mix_norm, w_in, mla_q_norm, mla_w_uq, mla_kv_norm, mla_w_ukv, diff_lambda_q1, diff_lambda_k1, diff_lambda_q2, diff_lambda_k2, diff_subln, nat_rpb, w_out, ffn2_norm, ffn2_w_gate, ffn2_w_up, ffn2_w_down, final_norm):
    L = w_ada.shape[0]
    B, S_ctx, _ = x_prompt.shape
    Bd, S_lat, _ = x_sample.shape
    assert S_ctx == CTX_SEQ and S_lat == LAT_SEQ and 1 + Bd <= ADA_ROWS
    assert cache_mla_ckv.shape[2] == CTX_SEQ and (B * CTX_SEQ) % FFN_TILE == 0

    cvec = jnp.concatenate([c_ctx[None, :], c, jnp.zeros((ADA_ROWS - 1 - Bd, D_MODEL), F32)], axis=0)
    b_ada3 = b_ada.reshape(L, 1, N_ADA * D_MODEL)
    ada = _ada_call(cvec, w_ada, b_ada3, 0)

    bf = lambda w: w.astype(BF16)
    win = _win_prep_call(w_in)
    wuq4 = mla_w_uq.reshape(L, MLA_Q_LORA, MLA_HEADS, MLA_NOPE + MLA_ROPE)
    wuq = bf(jnp.concatenate([wuq4[..., :MLA_NOPE].reshape(L, MLA_Q_LORA, -1),
                              wuq4[..., MLA_NOPE:].reshape(L, MLA_Q_LORA, -1)], axis=2))
    wukv4 = mla_w_ukv.reshape(L, MLA_KV_LORA, MLA_HEADS, MLA_NOPE + MLA_V)
    wukv = bf(jnp.concatenate([wukv4[..., :MLA_NOPE].reshape(L, MLA_KV_LORA, -1),
                               wukv4[..., MLA_NOPE:].reshape(L, MLA_KV_LORA, -1)], axis=2))
    wout = bf(w_out)
    lamv = jnp.stack([diff_lambda_q1, diff_lambda_k1, diff_lambda_q2, diff_lambda_k2], axis=1)
    subln = jnp.tile(diff_subln, (1, DIFF_HEADS)).reshape(L, HEAD_BLOCK, 1)
    r3 = lambda a: a.reshape(L, 1, a.shape[-1])
    n1, n2, nm, qn, kvn = r3(ffn1_norm), r3(ffn2_norm), r3(mix_norm), r3(mla_q_norm), r3(mla_kv_norm)
    fnorm = final_norm.reshape(1, D_MODEL)

    caches = (bf(cache_mla_ckv), jnp.swapaxes(cache_mla_krope, -1, -2),
              _feature_major(cache_diff_k), _feature_major(cache_diff_v),
              _feature_major(cache_nat_k), _feature_major(cache_nat_v))
    rope_tabs = (jnp.asarray(_ROPE_COS), jnp.asarray(_ROPE_SNEXT), jnp.asarray(_ROPE_SPREV))
    rpb_rows = _rpb_rows(nat_rpb)

    xi = x_prompt.reshape(B * CTX_SEQ, D_MODEL)
    xs = x_sample.reshape(Bd * LAT_SEQ, D_MODEL)
    nb = MIX_CTX_BATCH if B % MIX_CTX_BATCH == 0 else FFN_TILE // CTX_SEQ
    states = None
    for l in range(L):
        lam_init = 0.8 - 0.6 * math.exp(-0.3 * l)
        last = l == L - 1
        ada4 = ada.reshape(ADA_ROWS, 3, 3, D_MODEL)
        xi, xs, *st = _ffn_call(xi, xs, ada4, l, 0, n1, ffn1_w_gate, ffn1_w_up, ffn1_w_down, None, "ffn1",
                                fill_layers=L if l == 0 else 0)
        states = tuple(st) if l == 0 else states
        xi, states = _mix_i_call(xi, ada4, l, L, nm, win, qn, wuq, kvn, wukv, lamv, subln, wout, states, nb, lam_init)
        xs = _mix_ii_call(xs, ada4, l, nm, win, qn, wuq, kvn, wukv, lamv, subln, wout, rope_tabs,
                          rpb_rows, caches, lam_init)
        xi, xs, *nxt = _ffn_call(xi, xs, ada4, l, 2, n2, ffn2_w_gate, ffn2_w_up, ffn2_w_down,
                                 fnorm if last else None, "ffn2",
                                 next_ada=None if last else (cvec, w_ada, b_ada3))
        ada = None if last else nxt[0]

    states = (states[0],) + tuple(jnp.swapaxes(s, -1, -2) for s in states[1:])
    return (xi.reshape(B, CTX_SEQ, D_MODEL), xs.reshape(Bd, LAT_SEQ, D_MODEL)) + states
```

```python
import functools
import math

import numpy as np
import jax
import jax.numpy as jnp
from jax import lax
from jax.experimental import pallas as pl
from jax.experimental.pallas import tpu as pltpu

F32 = jnp.float32
BF16 = jnp.bfloat16

D_MODEL = 1024
FFN_DIM = 2816
NORM_EPS = 1e-6
ROPE_THETA = 10000.0
GRID_W = 64
N_ADA = 9
NEG_INF = -1e30

MLA_HEADS = 4
MLA_Q_LORA = 256
MLA_KV_LORA = 128
MLA_NOPE = 64
MLA_ROPE = 32
MLA_V = 64
DIFF_HEADS = 4
DIFF_DK = 32
DIFF_DV = 64
NAT_HEADS = 8
NAT_HD = 64
NAT_WIN_ROWS = 8
NAT_WIN_COLS = 16

LOG2E = math.log2(math.e)
MLA_SCALE = (MLA_NOPE + MLA_ROPE) ** -0.5 * LOG2E
DIFF_SCALE = DIFF_DK ** -0.5 * LOG2E
NAT_SCALE = NAT_HD ** -0.5 * LOG2E

CTX_SEQ = 256
LAT_SEQ = 1024
LAT_ROWS = LAT_SEQ // GRID_W

LANES = 128
MXU_DIM = 256
VMEM_LIMIT_BYTES = 58 * 1024 * 1024

P_CQ = 0
P_CKV = 256
P_KR = 384
P_DQ = 512
P_DK = 768
P_DV = 1024
P_NQ = 1280
P_NK = 1792
P_NV = 2304
P_COLS = 2816
P_SPLIT = 512
HEAD_BLOCK = 256
ADA_ROWS = 8

FFN_CHUNKS = ((0, 512), (512, 1024), (1024, 1536), (1536, 2048), (2048, 2560), (2560, 2816))
FFN_TILE = 512
N_STATES = 6
MIX_CTX_BATCH = 2


def _rope_tables():
    t = np.arange(LAT_SEQ)
    pos = np.stack([t // GRID_W, t % GRID_W], axis=0).astype(np.float64)
    lane = np.arange(LANES)
    p = lane % 32
    axis = (p >= 16).astype(np.int64)
    freqs = ROPE_THETA ** (-(p % 8).astype(np.float64) / 8.0)
    ang = pos[axis, :].T * freqs[None, :]
    first = (p % 16) < 8
    cos = np.cos(ang)
    sin = np.sin(ang)
    s_next = np.where(first[None, :], -sin, 0.0)
    s_prev = np.where(first[None, :], 0.0, sin)
    return cos.astype(np.float32), s_next.astype(np.float32), s_prev.astype(np.float32)


_ROPE_COS, _ROPE_SNEXT, _ROPE_SPREV = _rope_tables()


def _rmsnorm(x, g):
    ms = jnp.mean(x * x, axis=-1, keepdims=True)
    return x * lax.rsqrt(ms + NORM_EPS) * g


def _silu(x):
    return x / (1.0 + jnp.exp(-x))


def _dot(a, b):
    return jnp.dot(a, b, preferred_element_type=F32)


def _dot_t(a, b):
    return lax.dot_general(a, b, (((1,), (1,)), ((), ())), preferred_element_type=F32)


class _Proj:
    def __init__(self, a, b):
        self.a, self.b = a, b

    def __getitem__(self, idx):
        lo, hi = idx[1].start, idx[1].stop
        return self.a[:, lo:hi] if hi <= P_SPLIT else self.b[:, lo - P_SPLIT:hi - P_SPLIT]


def _lane_mask(width, lo, hi):
    lane = lax.broadcasted_iota(jnp.int32, (1, width), 1)
    return (lane >= lo) & (lane < hi)


def _head_masks(width=HEAD_BLOCK, group=64, n=4):
    return [_lane_mask(width, h * group, (h + 1) * group) for h in range(n)]


def _mla_qmasks():
    lane = lax.broadcasted_iota(jnp.int32, (1, HEAD_BLOCK + LANES), 1)
    out = []
    for h in range(MLA_HEADS):
        nope = (lane >= h * MLA_NOPE) & (lane < (h + 1) * MLA_NOPE)
        rope = (lane >= HEAD_BLOCK + h * MLA_ROPE) & (lane < HEAD_BLOCK + (h + 1) * MLA_ROPE)
        out.append(nope | rope)
    return out


def _tile32(blk):
    return blk + pltpu.roll(blk, 32, 1) + pltpu.roll(blk, 64, 1) + pltpu.roll(blk, 96, 1)


def _rope(x, cos, s_next, s_prev):
    return x * cos + pltpu.roll(x, LANES - 8, 1) * s_next + pltpu.roll(x, 8, 1) * s_prev


def _stack_masked(q, masks):
    zero = jnp.zeros_like(q)
    return jnp.concatenate([jnp.where(m, q, zero) for m in masks], axis=0)


def _scores_t(q, k_bf, kmasks, stacked):
    sk = k_bf.shape[0]
    if stacked:
        s = _dot_t(_stack_masked(k_bf, kmasks), q)
        return [s[h * sk:(h + 1) * sk] for h in range(len(kmasks))]
    zero = jnp.zeros_like(k_bf)
    return [_dot_t(jnp.where(m, k_bf, zero), q) for m in kmasks]


def _pv_t(vt_bf, p_bf, h, short_keys):
    if short_keys:
        return _dot(vt_bf, p_bf)[h * 64:(h + 1) * 64]
    return _dot(vt_bf[h * 64:(h + 1) * 64], p_bf)


def _exp_sum_t(s):
    e = jnp.exp2(s - jnp.max(s, axis=0, keepdims=True))
    return e, jnp.sum(e, axis=0, keepdims=True)


def _softmax_heads_t(q, k_bf, vt_bf, kmasks, stacked):
    parts = []
    for h, s in enumerate(_scores_t(q, k_bf, kmasks, stacked)):
        e, l = _exp_sum_t(s)
        parts.append(_pv_t(vt_bf, e.astype(BF16), h, stacked) / l)
    return jnp.concatenate(parts, axis=0)


def _diff_heads_t(q, k_bf, vt_bf, lam, subln_col, lam_init, stacked):
    n = DIFF_HEADS
    m1 = [_lane_mask(HEAD_BLOCK, h * 2 * DIFF_DK, h * 2 * DIFF_DK + DIFF_DK) for h in range(n)]
    m2 = [_lane_mask(HEAD_BLOCK, h * 2 * DIFF_DK + DIFF_DK, (h + 1) * 2 * DIFF_DK) for h in range(n)]
    s = _scores_t(q, k_bf, m1 + m2, stacked)
    parts = []
    for h in range(n):
        e1, l1 = _exp_sum_t(s[h])
        e2, l2 = _exp_sum_t(s[n + h])
        p = e1 * (1.0 / l1) - e2 * (lam / l2)
        oh = _pv_t(vt_bf, p.astype(BF16), h, stacked)
        ms = jnp.mean(oh * oh, axis=0, keepdims=True)
        parts.append(oh * lax.rsqrt(ms + NORM_EPS))
    return jnp.concatenate(parts, axis=0) * subln_col * (1.0 - lam_init)


def _lambda(lamv, lam_init):
    a = jnp.sum(lamv[0:1] * lamv[1:2], axis=-1, keepdims=True)
    b = jnp.sum(lamv[2:3] * lamv[3:4], axis=-1, keepdims=True)
    return jnp.exp(a) - jnp.exp(b) + lam_init


def _ada_block(c_ref, w_ref, b_ref):
    return _dot(_silu(c_ref[...]).astype(BF16), w_ref[...].astype(BF16)) + b_ref[...]


def _ada_kernel(c_ref, w_ref, b_ref, o_ref):
    o_ref[...] = _ada_block(c_ref, w_ref, b_ref)


def _ada_call(cvec, w_ada, b_ada3, layer):
    n = w_ada.shape[2]
    tn = 1024
    return pl.pallas_call(
        _ada_kernel,
        grid=(n // tn,),
        in_specs=[
            pl.BlockSpec((ADA_ROWS, D_MODEL), lambda j: (0, 0)),
            pl.BlockSpec((None, D_MODEL, tn), lambda j: (layer, 0, j)),
            pl.BlockSpec((None, 1, tn), lambda j: (layer, 0, j)),
        ],
        out_specs=pl.BlockSpec((ADA_ROWS, tn), lambda j: (0, j)),
        out_shape=jax.ShapeDtypeStruct((ADA_ROWS, n), F32),
        compiler_params=pltpu.CompilerParams(
            dimension_semantics=("arbitrary",), vmem_limit_bytes=VMEM_LIMIT_BYTES),
        name="ada",
    )(cvec, w_ada, b_ada3)


def _ffn_tile(x_ref, o_ref, mod, g_ref, wg_ref, wu_ref, wd_ref, fn_ref):
    x = x_ref[...]
    h = (_rmsnorm(x, g_ref[...]) * (1.0 + mod[1:2]) + mod[0:1]).astype(BF16)
    acc = None
    for lo, hi in FFN_CHUNKS:
        g = _dot(h, wg_ref[:, lo:hi].astype(BF16))
        u = _dot(h, wu_ref[:, lo:hi].astype(BF16))
        part = _dot((_silu(g) * u).astype(BF16), wd_ref[lo:hi, :].astype(BF16))
        acc = part if acc is None else acc + part
    y = x + (0.5 * mod[2:3]) * acc
    if fn_ref is not None:
        y = _rmsnorm(y, fn_ref[...])
    o_ref[...] = y


def _state_fill_copies(t, state_refs, zero_refs, sem, n_layers):
    per_tile = FFN_TILE // CTX_SEQ
    copies = []
    for j in range(per_tile):
        for l in range(n_layers):
            for st, z in zip(state_refs, zero_refs):
                dst = st.at[t * per_tile + j, l]
                if dst.shape == z.shape:
                    copies.append(pltpu.make_async_copy(z, dst, sem))
                else:
                    nh = z.shape[0]
                    copies += [pltpu.make_async_copy(z, dst.at[h:h + nh], sem) for h in range(0, dst.shape[0], nh)]
    return copies


def _ffn_kernel(xc_ref, xl_ref, mod_ref, g_ref, wg_ref, wu_ref, wd_ref, *rest,
                n_ctx_tiles, final, fill_layers, next_ada):
    rest = list(rest)
    fn_ref = rest.pop(0) if final else None
    ada_in = [rest.pop(0) for _ in range(3)] if next_ada else None
    oc_ref, ol_ref = rest.pop(0), rest.pop(0)
    state_refs = [rest.pop(0) for _ in range(N_STATES)] if fill_layers else None
    if next_ada:
        rest.pop(0)[...] = _ada_block(*ada_in)
    mod = mod_ref[...]
    t = pl.program_id(0)
    if fill_layers:
        z_ckv, z_kr, z_heads, zsem = rest

        @pl.when(t == 0)
        def _():
            for z in (z_ckv, z_kr, z_heads):
                z[...] = jnp.zeros(z.shape, z.dtype)

    @pl.when(t < n_ctx_tiles)
    def _():
        fill = []
        if fill_layers:
            fill = _state_fill_copies(t, state_refs, (z_ckv, z_kr, z_heads, z_heads, z_heads, z_heads), zsem,
                                      fill_layers)
        for c in fill:
            c.start()
        _ffn_tile(xc_ref, oc_ref, mod, g_ref, wg_ref, wu_ref, wd_ref, fn_ref)
        for c in fill:
            c.wait()

    @pl.when(t >= n_ctx_tiles)
    def _():
        _ffn_tile(xl_ref, ol_ref, mod, g_ref, wg_ref, wu_ref, wd_ref, fn_ref)


def _resident(shape, index_map):
    return pl.BlockSpec(shape, index_map, pipeline_mode=pl.Buffered(1))


def _state_shapes(B, L):
    return [jax.ShapeDtypeStruct(s, F32) for s in (
        (B, L, CTX_SEQ, MLA_KV_LORA), (B, L, MLA_ROPE, CTX_SEQ),
        (B, L, DIFF_HEADS, 2 * DIFF_DK, CTX_SEQ), (B, L, DIFF_HEADS, DIFF_DV, CTX_SEQ),
        (B, L, NAT_HEADS, NAT_HD, CTX_SEQ), (B, L, NAT_HEADS, NAT_HD, CTX_SEQ))]


def _ffn_call(xc, xl, ada4, layer, group, norm, wg, wu, wd, final_norm, name, fill_layers=0, next_ada=None):
    tm = FFN_TILE
    n_ctx = xc.shape[0] // tm
    n_lat = xl.shape[0] // tm
    per_lat = LAT_SEQ // tm
    ctx_blk = lambda t: (jnp.minimum(t, n_ctx - 1), 0)
    lat_blk = lambda t: (jnp.maximum(t - n_ctx, 0), 0)
    ada_row = lambda t: jnp.maximum(t - n_ctx, -per_lat) // per_lat + 1
    in_specs = [
        pl.BlockSpec((tm, D_MODEL), ctx_blk),
        pl.BlockSpec((tm, D_MODEL), lat_blk),
        pl.BlockSpec((None, None, 3, D_MODEL), lambda t: (ada_row(t), group, 0, 0)),
        _resident((None, 1, D_MODEL), lambda t: (layer, 0, 0)),
        _resident((None, D_MODEL, FFN_DIM), lambda t: (layer, 0, 0)),
        _resident((None, D_MODEL, FFN_DIM), lambda t: (layer, 0, 0)),
        _resident((None, FFN_DIM, D_MODEL), lambda t: (layer, 0, 0)),
    ]
    args = [xc, xl, ada4, norm, wg, wu, wd]
    if final_norm is not None:
        in_specs.append(_resident((1, D_MODEL), lambda t: (0, 0)))
        args.append(final_norm)
    if next_ada is not None:
        n_ada = next_ada[1].shape[2]
        tn = n_ada // (n_ctx + n_lat)
        assert tn * (n_ctx + n_lat) == n_ada and tn % LANES == 0
        in_specs += [_resident((ADA_ROWS, D_MODEL), lambda t: (0, 0)),
                     pl.BlockSpec((None, D_MODEL, tn), lambda t: (layer + 1, 0, t)),
                     pl.BlockSpec((None, 1, tn), lambda t: (layer + 1, 0, t))]
        args += list(next_ada)
    out_specs = [pl.BlockSpec((tm, D_MODEL), ctx_blk), pl.BlockSpec((tm, D_MODEL), lat_blk)]
    out_shape = [jax.ShapeDtypeStruct(xc.shape, F32), jax.ShapeDtypeStruct(xl.shape, F32)]
    scratch = []
    if fill_layers:
        states = _state_shapes(xc.shape[0] // CTX_SEQ, fill_layers)
        out_specs += [pl.BlockSpec(memory_space=pl.ANY)] * N_STATES
        out_shape += states
        scratch += [pltpu.VMEM(states[k].shape[2:], F32) for k in (0, 1, 2)] + [pltpu.SemaphoreType.DMA(())]
    if next_ada is not None:
        out_specs.append(pl.BlockSpec((ADA_ROWS, tn), lambda t: (0, t)))
        out_shape.append(jax.ShapeDtypeStruct((ADA_ROWS, n_ada), F32))
    return pl.pallas_call(
        functools.partial(_ffn_kernel, n_ctx_tiles=n_ctx, final=final_norm is not None, fill_layers=fill_layers,
                          next_ada=next_ada is not None),
        grid=(n_ctx + n_lat,),
        in_specs=in_specs,
        out_specs=out_specs,
        out_shape=out_shape,
        scratch_shapes=scratch,
        compiler_params=pltpu.CompilerParams(
            dimension_semantics=("arbitrary",), vmem_limit_bytes=VMEM_LIMIT_BYTES),
        name=name,
    )(*args)


def _mix_i_kernel(x_ref, mod_ref, g_ref, wina_ref, winb_ref, qn_ref, wuq_ref, kvn_ref, wukv_ref, lamv_ref, subln_ref,
                  wout_ref, *refs, nb, lam_init):
    y_ref, ckv_ref, kr_ref, dk_ref, dv_ref, nk_ref, nv_ref = refs[-7:]
    S = CTX_SEQ
    x = x_ref[...]
    mod = mod_ref[...]
    h = (_rmsnorm(x, g_ref[...]) * (1.0 + mod[1:2]) + mod[0:1]).astype(BF16)
    proj = _Proj(_dot(h, wina_ref[...]), _dot(h, winb_ref[...]))
    q_cat = _dot(_rmsnorm(proj[:, P_CQ:P_CQ + MLA_Q_LORA], qn_ref[...]).astype(BF16), wuq_ref[...])
    ckv = _rmsnorm(proj[:, P_CKV:P_CKV + MLA_KV_LORA], kvn_ref[...])
    kv = _dot(ckv.astype(BF16), wukv_ref[...])
    kr_blk = proj[:, P_KR:P_KR + LANES]
    k_cat = jnp.concatenate([kv[:, 0:HEAD_BLOCK], _tile32(kr_blk)], axis=1).astype(BF16)
    v_mla = kv[:, HEAD_BLOCK:2 * HEAD_BLOCK]
    q_cat = (q_cat * MLA_SCALE).astype(BF16)
    dq = (proj[:, P_DQ:P_DQ + HEAD_BLOCK] * DIFF_SCALE).astype(BF16)
    dk = proj[:, P_DK:P_DK + HEAD_BLOCK]
    dv = proj[:, P_DV:P_DV + HEAD_BLOCK]
    nq = (proj[:, P_NQ:P_NQ + 2 * HEAD_BLOCK] * NAT_SCALE).astype(BF16)
    nk = proj[:, P_NK:P_NK + 2 * HEAD_BLOCK]
    nv = proj[:, P_NV:P_NV + 2 * HEAD_BLOCK]
    dk_bf, nk_bf = dk.astype(BF16), nk.astype(BF16)
    lam = _lambda(lamv_ref[...], lam_init)
    hmasks = _head_masks()
    mla_qm = _mla_qmasks()

    outs = []
    for j in range(nb):
        r0, r1 = j * S, (j + 1) * S
        ckv_ref[j] = ckv[r0:r1]
        kr_ref[j] = kr_blk[r0:r1].T[0:MLA_ROPE]
        dk_t, dv_t = dk[r0:r1].T, dv[r0:r1].T
        nk_t = [nk[r0:r1, b * HEAD_BLOCK:(b + 1) * HEAD_BLOCK].T for b in range(2)]
        nv_t = [nv[r0:r1, b * HEAD_BLOCK:(b + 1) * HEAD_BLOCK].T for b in range(2)]
        for hh in range(DIFF_HEADS):
            dk_ref[j, hh] = dk_t[hh * 64:(hh + 1) * 64]
            dv_ref[j, hh] = dv_t[hh * 64:(hh + 1) * 64]
        for hh in range(NAT_HEADS):
            nk_ref[j, hh] = nk_t[hh // 4][(hh % 4) * 64:(hh % 4 + 1) * 64]
            nv_ref[j, hh] = nv_t[hh // 4][(hh % 4) * 64:(hh % 4 + 1) * 64]
        ot_mla = _softmax_heads_t(q_cat[r0:r1], k_cat[r0:r1], v_mla[r0:r1].T.astype(BF16), mla_qm, True)
        ot_diff = _diff_heads_t(dq[r0:r1], dk_bf[r0:r1], dv_t.astype(BF16), lam, subln_ref[...], lam_init, True)
        ot_nat = [
            _softmax_heads_t(nq[r0:r1, b * HEAD_BLOCK:(b + 1) * HEAD_BLOCK], nk_bf[r0:r1, b * HEAD_BLOCK:(b + 1) * HEAD_BLOCK],
                             nv_t[b].astype(BF16), hmasks, True)
            for b in range(2)
        ]
        outs.append(jnp.concatenate([ot_mla, ot_diff] + ot_nat, axis=0).T.astype(BF16))
    o = jnp.concatenate(outs, axis=0) if nb > 1 else outs[0]
    y_ref[...] = x + mod[2:3] * _dot(o, wout_ref[...])


def _mix_i_call(x, ada4, layer, n_layers, mix_norm, win, qn, wuq, kvn, wukv, lamv, subln, wout, states, nb, lam_init):
    T = x.shape[0]
    B = T // CTX_SEQ
    tm = nb * CTX_SEQ
    lyr = lambda shape: _resident((None,) + shape, lambda i: (layer,) + (0,) * len(shape))
    any_spec = pl.BlockSpec(memory_space=pl.ANY)
    in_specs = [
        pl.BlockSpec((tm, D_MODEL), lambda i: (i, 0)),
        _resident((None, None, 3, D_MODEL), lambda i: (0, 1, 0, 0)),
        lyr((1, D_MODEL)),
        lyr((D_MODEL, P_SPLIT)),
        lyr((D_MODEL, P_COLS - P_SPLIT)),
        lyr((1, MLA_Q_LORA)),
        lyr((MLA_Q_LORA, HEAD_BLOCK + LANES)),
        lyr((1, MLA_KV_LORA)),
        lyr((MLA_KV_LORA, 2 * HEAD_BLOCK)),
        lyr((4, DIFF_DK)),
        lyr((HEAD_BLOCK, 1)),
        lyr((D_MODEL, D_MODEL)),
    ]
    n_fixed = len(in_specs)
    in_specs += [any_spec] * len(states)
    out_specs = [
        pl.BlockSpec((tm, D_MODEL), lambda i: (i, 0)),
        pl.BlockSpec((nb, None, CTX_SEQ, MLA_KV_LORA), lambda i: (i, layer, 0, 0)),
        pl.BlockSpec((nb, None, MLA_ROPE, CTX_SEQ), lambda i: (i, layer, 0, 0)),
        pl.BlockSpec((nb, None, DIFF_HEADS, 2 * DIFF_DK, CTX_SEQ), lambda i: (i, layer, 0, 0, 0)),
        pl.BlockSpec((nb, None, DIFF_HEADS, DIFF_DV, CTX_SEQ), lambda i: (i, layer, 0, 0, 0)),
        pl.BlockSpec((nb, None, NAT_HEADS, NAT_HD, CTX_SEQ), lambda i: (i, layer, 0, 0, 0)),
        pl.BlockSpec((nb, None, NAT_HEADS, NAT_HD, CTX_SEQ), lambda i: (i, layer, 0, 0, 0)),
    ]
    out_shape = [jax.ShapeDtypeStruct((T, D_MODEL), F32)] + _state_shapes(B, n_layers)
    res = pl.pallas_call(
        functools.partial(_mix_i_kernel, nb=nb, lam_init=lam_init),
        grid=(B // nb,),
        in_specs=in_specs,
        out_specs=out_specs,
        out_shape=out_shape,
        input_output_aliases={n_fixed + k: 1 + k for k in range(len(states))},
        compiler_params=pltpu.CompilerParams(
            dimension_semantics=("arbitrary",), vmem_limit_bytes=VMEM_LIMIT_BYTES),
        name="mix_ctx",
    )(x, ada4, mix_norm, *win, qn, wuq, kvn, wukv, lamv, subln, wout, *states)
    return res[0], tuple(res[1:])


PROJ_ROWS = 256
DENSE_QROWS = MXU_DIM
N_KEYS = CTX_SEQ + LAT_SEQ
BIAS_LANES = NAT_WIN_ROWS * LANES


def _mix_ii_kernel(x_ref, mod_ref, g_ref, wina_ref, winb_ref, qn_ref, wuq_ref, kvn_ref, wukv_ref, lamv_ref, subln_ref,
                   wout_ref, cos_ref, snext_ref, sprev_ref, rpb_ref,
                   cckv_ref, ckr_ref, cdk_ref, cdv_ref, cnk_ref, cnv_ref,
                   y_ref,
                   qcat_s, ckv_s, kcat_s, vmla_s, dq_s, dk_s, dv_s, nq_s, nk_s, nv_s, o_s, bias_ref,
                   vmlat_s, dvt_s, cnv_s, *, lam_init):
    S = LAT_SEQ
    C = CTX_SEQ
    mod = mod_ref[...]
    lam = _lambda(lamv_ref[...], lam_init)
    hmasks = _head_masks()
    mla_qm = _mla_qmasks()

    @pl.when(pl.program_id(0) == 0)
    def _():
        qc = lax.broadcasted_iota(jnp.int32, (GRID_W, LANES), 0)
        kc = lax.broadcasted_iota(jnp.int32, (GRID_W, LANES), 1) % GRID_W
        c_start = jnp.clip(qc - NAT_WIN_COLS // 2, 0, GRID_W - NAT_WIN_COLS)
        in_win = (kc >= c_start) & (kc < c_start + NAT_WIN_COLS)
        for par in range(2):
            for h in range(NAT_HEADS):
                for p in range(NAT_WIN_ROWS):
                    row = jnp.broadcast_to(rpb_ref[par, h, p:p + 1, :], (GRID_W, LANES))
                    tile = pltpu.roll(row, LANES - (NAT_WIN_COLS - 1), 1, stride=1, stride_axis=0)
                    bias_ref[par, h // 4, (h % 4) * GRID_W:(h % 4 + 1) * GRID_W, p * LANES:(p + 1) * LANES] = (
                        jnp.where(in_win, tile * LOG2E, NEG_INF))

    ckv_s[0:C, :] = cckv_ref[...]
    kr_t = ckr_ref[...]
    kcat_s[0:C, HEAD_BLOCK:] = jnp.concatenate([kr_t] * (LANES // MLA_ROPE), axis=0).T.astype(BF16)
    dk_s[0:C, :] = cdk_ref[...].T.astype(BF16)
    dvt_s[:, 0:C] = cdv_ref[...].astype(BF16)
    cnv_s[...] = cnv_ref[...].T.astype(BF16)

    def proj_body(i, carry):
        r = pl.multiple_of(i * PROJ_ROWS, PROJ_ROWS)
        rows = pl.ds(r, PROJ_ROWS)
        krows = pl.ds(C + r, PROJ_ROWS)
        cos, s_next, s_prev = cos_ref[rows, :], snext_ref[rows, :], sprev_ref[rows, :]
        rope = lambda v: _rope(v, cos, s_next, s_prev)
        x = x_ref[rows, :]
        h = (_rmsnorm(x, g_ref[...]) * (1.0 + mod[1:2]) + mod[0:1]).astype(BF16)
        proj = _Proj(_dot(h, wina_ref[...]), _dot(h, winb_ref[...]))
        q_cat = _dot(_rmsnorm(proj[:, P_CQ:P_CQ + MLA_Q_LORA], qn_ref[...]).astype(BF16), wuq_ref[...])
        qcat_s[rows, 0:HEAD_BLOCK] = (q_cat[:, 0:HEAD_BLOCK] * MLA_SCALE).astype(BF16)
        qcat_s[rows, HEAD_BLOCK:] = (rope(q_cat[:, HEAD_BLOCK:]) * MLA_SCALE).astype(BF16)
        ckv_s[krows, :] = _rmsnorm(proj[:, P_CKV:P_CKV + MLA_KV_LORA], kvn_ref[...]).astype(BF16)
        kcat_s[krows, HEAD_BLOCK:] = _tile32(rope(proj[:, P_KR:P_KR + LANES])).astype(BF16)
        for b in range(2):
            c0 = b * LANES
            dq_s[rows, c0:c0 + LANES] = (rope(proj[:, P_DQ + c0:P_DQ + c0 + LANES]) * DIFF_SCALE).astype(BF16)
            dk_s[krows, c0:c0 + LANES] = rope(proj[:, P_DK + c0:P_DK + c0 + LANES]).astype(BF16)
        dv_s[krows, :] = proj[:, P_DV:P_DV + HEAD_BLOCK].astype(BF16)
        nq_s[rows, :] = (proj[:, P_NQ:P_NQ + 2 * HEAD_BLOCK] * NAT_SCALE).astype(BF16)
        nk_s[rows, :] = proj[:, P_NK:P_NK + 2 * HEAD_BLOCK].astype(BF16)
        nv_s[rows, :] = proj[:, P_NV:P_NV + 2 * HEAD_BLOCK].astype(BF16)
        return carry

    lax.fori_loop(0, S // PROJ_ROWS, proj_body, 0)

    def kv_body(i, carry):
        rows = pl.ds(pl.multiple_of(i * PROJ_ROWS, PROJ_ROWS), PROJ_ROWS)
        kv = _dot(ckv_s[rows, :], wukv_ref[...])
        kcat_s[rows, 0:HEAD_BLOCK] = kv[:, 0:HEAD_BLOCK].astype(BF16)
        vmla_s[rows, :] = kv[:, HEAD_BLOCK:].astype(BF16)
        return carry

    lax.fori_loop(0, N_KEYS // PROJ_ROWS, kv_body, 0)

    for j in range(N_KEYS // PROJ_ROWS):
        c0, c1 = j * PROJ_ROWS, (j + 1) * PROJ_ROWS
        vmlat_s[:, c0:c1] = vmla_s[c0:c1, :].astype(F32).T.astype(BF16)
        if c0 >= C:
            dvt_s[:, c0:c1] = dv_s[c0:c1, :].astype(F32).T.astype(BF16)

    def dense_body(i, carry):
        rows = pl.ds(pl.multiple_of(i * DENSE_QROWS, DENSE_QROWS), DENSE_QROWS)
        ot_mla = _softmax_heads_t(qcat_s[rows, :], kcat_s[...], vmlat_s[...], mla_qm, False)
        o_s[rows, 0:HEAD_BLOCK] = ot_mla.T.astype(BF16)
        ot_diff = _diff_heads_t(dq_s[rows, :], dk_s[...], dvt_s[...], lam, subln_ref[...], lam_init, False)
        o_s[rows, HEAD_BLOCK:2 * HEAD_BLOCK] = ot_diff.T.astype(BF16)
        return carry

    lax.fori_loop(0, S // DENSE_QROWS, dense_body, 0)

    for r in range(LAT_ROWS):
        rs = min(max(r - NAT_WIN_ROWS // 2, 0), LAT_ROWS - NAT_WIN_ROWS)
        dr0 = rs - r + (NAT_WIN_ROWS - 1)
        par = dr0 % 2
        off = GRID_W * (dr0 - par)
        q0, q1 = r * GRID_W, (r + 1) * GRID_W
        k0, k1 = rs * GRID_W, (rs + NAT_WIN_ROWS) * GRID_W
        for b in range(2):
            c0, c1 = b * HEAD_BLOCK, (b + 1) * HEAD_BLOCK
            q = nq_s[q0:q1, c0:c1]
            zero = jnp.zeros_like(q)
            qs = jnp.concatenate([jnp.where(m, q, zero) for m in hmasks], axis=0)
            sw = _dot_t(qs, nk_s[k0:k1, c0:c1]) + bias_ref[par, b, :, off:off + NAT_WIN_ROWS * GRID_W]
            sc = _dot(qs, cnk_ref[c0:c1, :].astype(BF16))
            m = jnp.maximum(jnp.max(sw, axis=-1, keepdims=True), jnp.max(sc, axis=-1, keepdims=True))
            ew = jnp.exp2(sw - m)
            ec = jnp.exp2(sc - m)
            l = jnp.sum(ew, axis=-1, keepdims=True) + jnp.sum(ec, axis=-1, keepdims=True)
            o = (_dot(ew.astype(BF16), nv_s[k0:k1, c0:c1]) + _dot(ec.astype(BF16), cnv_s[:, c0:c1])) / l
            of = None
            for hh, hm in enumerate(hmasks):
                part = jnp.where(hm, o[hh * GRID_W:(hh + 1) * GRID_W], 0.0)
                of = part if of is None else of + part
            o_s[q0:q1, 2 * HEAD_BLOCK + c0:2 * HEAD_BLOCK + c1] = of.astype(BF16)

    def out_body(i, carry):
        rows = pl.ds(pl.multiple_of(i * PROJ_ROWS, PROJ_ROWS), PROJ_ROWS)
        y_ref[rows, :] = x_ref[rows, :] + mod[2:3] * _dot(o_s[rows, :], wout_ref[...])
        return carry

    lax.fori_loop(0, S // PROJ_ROWS, out_body, 0)


def _mix_ii_call(x, ada4, layer, mix_norm, win, qn, wuq, kvn, wukv, lamv, subln, wout, rope_tabs, rpb_rows,
                 caches, lam_init):
    T = x.shape[0]
    Bd = T // LAT_SEQ
    cckv, ckr, cdk, cdv, cnk, cnv = caches
    lyr = lambda shape: _resident((None,) + shape, lambda i: (layer,) + (0,) * len(shape))
    cache = lambda rows, width: pl.BlockSpec((None, None, rows, width), lambda i: (i, layer, 0, 0),
                                             pipeline_mode=pl.Buffered(1))
    tab = _resident((LAT_SEQ, LANES), lambda i: (0, 0))
    in_specs = [
        pl.BlockSpec((LAT_SEQ, D_MODEL), lambda i: (i, 0), pipeline_mode=pl.Buffered(1)),
        pl.BlockSpec((None, None, 3, D_MODEL), lambda i: (i + 1, 1, 0, 0)),
        lyr((1, D_MODEL)),
        lyr((D_MODEL, P_SPLIT)),
        lyr((D_MODEL, P_COLS - P_SPLIT)),
        lyr((1, MLA_Q_LORA)),
        lyr((MLA_Q_LORA, HEAD_BLOCK + LANES)),
        lyr((1, MLA_KV_LORA)),
        lyr((MLA_KV_LORA, 2 * HEAD_BLOCK)),
        lyr((4, DIFF_DK)),
        lyr((HEAD_BLOCK, 1)),
        lyr((D_MODEL, D_MODEL)),
        tab, tab, tab,
        lyr((2, NAT_HEADS, NAT_WIN_ROWS, LANES)),
        cache(CTX_SEQ, MLA_KV_LORA), cache(MLA_ROPE, CTX_SEQ), cache(HEAD_BLOCK, CTX_SEQ), cache(HEAD_BLOCK, CTX_SEQ),
        cache(2 * HEAD_BLOCK, CTX_SEQ), cache(2 * HEAD_BLOCK, CTX_SEQ),
    ]
    scratch = [
        pltpu.VMEM((LAT_SEQ, HEAD_BLOCK + LANES), BF16),
        pltpu.VMEM((N_KEYS, MLA_KV_LORA), BF16),
        pltpu.VMEM((N_KEYS, HEAD_BLOCK + LANES), BF16),
        pltpu.VMEM((N_KEYS, HEAD_BLOCK), BF16),
        pltpu.VMEM((LAT_SEQ, HEAD_BLOCK), BF16),
        pltpu.VMEM((N_KEYS, HEAD_BLOCK), BF16),
        pltpu.VMEM((N_KEYS, HEAD_BLOCK), BF16),
        pltpu.VMEM((LAT_SEQ, 2 * HEAD_BLOCK), BF16),
        pltpu.VMEM((LAT_SEQ, 2 * HEAD_BLOCK), BF16),
        pltpu.VMEM((LAT_SEQ, 2 * HEAD_BLOCK), BF16),
        pltpu.VMEM((LAT_SEQ, D_MODEL), BF16),
        pltpu.VMEM((2, 2, HEAD_BLOCK, BIAS_LANES), F32),
        pltpu.VMEM((HEAD_BLOCK, N_KEYS), BF16),
        pltpu.VMEM((HEAD_BLOCK, N_KEYS), BF16),
        pltpu.VMEM((CTX_SEQ, 2 * HEAD_BLOCK), BF16),
    ]
    return pl.pallas_call(
        functools.partial(_mix_ii_kernel, lam_init=lam_init),
        grid=(Bd,),
        in_specs=in_specs,
        out_specs=pl.BlockSpec((LAT_SEQ, D_MODEL), lambda i: (i, 0)),
        out_shape=jax.ShapeDtypeStruct((T, D_MODEL), F32),
        scratch_shapes=scratch,
        compiler_params=pltpu.CompilerParams(
            dimension_semantics=("arbitrary",), vmem_limit_bytes=VMEM_LIMIT_BYTES),
        name="mix_lat",
    )(x, ada4, mix_norm, *win, qn, wuq, kvn, wukv, lamv, subln, wout, *rope_tabs, rpb_rows, *caches)


W_IN_SPLIT = MLA_Q_LORA + MLA_KV_LORA + MLA_ROPE


def _win_prep_kernel(wt_ref, a_ref, b_ref):
    tr = lambda r0: wt_ref[r0:r0 + MXU_DIM, :].T
    a_ref[:, 0:MXU_DIM] = tr(0).astype(BF16)
    live = _lane_mask(MXU_DIM, 0, W_IN_SPLIT - MXU_DIM)
    a_ref[:, MXU_DIM:P_SPLIT] = jnp.where(live, tr(MXU_DIM), 0.0).astype(BF16)
    for j in range((P_COLS - P_SPLIT) // MXU_DIM):
        b_ref[:, j * MXU_DIM:(j + 1) * MXU_DIM] = tr(W_IN_SPLIT + j * MXU_DIM).astype(BF16)


def _win_prep_call(w_in):
    L, D, n = w_in.shape
    assert n - W_IN_SPLIT == P_COLS - P_SPLIT and P_SPLIT == 2 * MXU_DIM
    return pl.pallas_call(
        _win_prep_kernel,
        grid=(L,),
        in_specs=[pl.BlockSpec((None, n, D), lambda l: (l, 0, 0))],
        out_specs=[pl.BlockSpec((None, D, P_SPLIT), lambda l: (l, 0, 0)),
                   pl.BlockSpec((None, D, P_COLS - P_SPLIT), lambda l: (l, 0, 0))],
        out_shape=[jax.ShapeDtypeStruct((L, D, P_SPLIT), BF16), jax.ShapeDtypeStruct((L, D, P_COLS - P_SPLIT), BF16)],
        compiler_params=pltpu.CompilerParams(
            dimension_semantics=("arbitrary",), vmem_limit_bytes=VMEM_LIMIT_BYTES),
        name="win_prep",
    )(jnp.swapaxes(w_in, 1, 2))


def _rpb_rows(rpb):
    L = rpb.shape[0]
    n_dr, n_dc = 2 * NAT_WIN_ROWS - 1, 2 * NAT_WIN_COLS - 1
    padded = jnp.pad(rpb.astype(F32), ((0, 0), (0, 0), (0, 2 * NAT_WIN_ROWS + 1 - n_dr), (0, GRID_W - n_dc)))
    even = padded[:, :, 0:2 * NAT_WIN_ROWS].reshape(L, NAT_HEADS, NAT_WIN_ROWS, LANES)
    odd = padded[:, :, 1:2 * NAT_WIN_ROWS + 1].reshape(L, NAT_HEADS, NAT_WIN_ROWS, LANES)
    return jnp.stack([even, odd], axis=1)


def _feature_major(cache):
    B, L, H, S, d = cache.shape
    return jnp.swapaxes(cache, -1, -2).reshape(B, L, H * d, S)


def kernel(x_prompt, x_sample, cache_mla_ckv, cache_mla_krope, cache_diff_k, cache_diff_v, cache_nat_k, cache_nat_v, c, c_ctx, w_ada, b_ada, ffn1_norm, ffn1_w_gate, ffn1_w_up, ffn1_w_down, mix_norm, w_in, mla_q_norm, mla_w_uq, mla_kv_norm, mla_w_ukv, diff_lambda_q1, diff_lambda_k1, diff_lambda_q2, diff_lambda_k2, diff_subln, nat_rpb, w_out, ffn2_norm, ffn2_w_gate, ffn2_w_up, ffn2_w_down, final_norm):
    L = w_ada.shape[0]
    B, S_ctx, _ = x_prompt.shape
    Bd, S_lat, _ = x_sample.shape
    assert S_ctx == CTX_SEQ and S_lat == LAT_SEQ and 1 + Bd <= ADA_ROWS
    assert cache_mla_ckv.shape[2] == CTX_SEQ and (B * CTX_SEQ) % FFN_TILE == 0

    cvec = jnp.concatenate([c_ctx[None, :], c, jnp.zeros((ADA_ROWS - 1 - Bd, D_MODEL), F32)], axis=0)
    b_ada3 = b_ada.reshape(L, 1, N_ADA * D_MODEL)
    ada = _ada_call(cvec, w_ada, b_ada3, 0)

    bf = lambda w: w.astype(BF16)
    win = _win_prep_call(w_in)
    wuq4 = mla_w_uq.reshape(L, MLA_Q_LORA, MLA_HEADS, MLA_NOPE + MLA_ROPE)
    wuq = bf(jnp.concatenate([wuq4[..., :MLA_NOPE].reshape(L, MLA_Q_LORA, -1),
                              wuq4[..., MLA_NOPE:].reshape(L, MLA_Q_LORA, -1)], axis=2))
    wukv4 = mla_w_ukv.reshape(L, MLA_KV_LORA, MLA_HEADS, MLA_NOPE + MLA_V)
    wukv = bf(jnp.concatenate([wukv4[..., :MLA_NOPE].reshape(L, MLA_KV_LORA, -1),
                               wukv4[..., MLA_NOPE:].reshape(L, MLA_KV_LORA, -1)], axis=2))
    wout = bf(w_out)
    lamv = jnp.stack([diff_lambda_q1, diff_lambda_k1, diff_lambda_q2, diff_lambda_k2], axis=1)
    subln = jnp.tile(diff_subln, (1, DIFF_HEADS)).reshape(L, HEAD_BLOCK, 1)
    r3 = lambda a: a.reshape(L, 1, a.shape[-1])
    n1, n2, nm, qn, kvn = r3(ffn1_norm), r3(ffn2_norm), r3(mix_norm), r3(mla_q_norm), r3(mla_kv_norm)
    fnorm = final_norm.reshape(1, D_MODEL)

    caches = (bf(cache_mla_ckv), jnp.swapaxes(cache_mla_krope, -1, -2),
              _feature_major(cache_diff_k), _feature_major(cache_diff_v),
              _feature_major(cache_nat_k), _feature_major(cache_nat_v))
    rope_tabs = (jnp.asarray(_ROPE_COS), jnp.asarray(_ROPE_SNEXT), jnp.asarray(_ROPE_SPREV))
    rpb_rows = _rpb_rows(nat_rpb)

    xi = x_prompt.reshape(B * CTX_SEQ, D_MODEL)
    xs = x_sample.reshape(Bd * LAT_SEQ, D_MODEL)
    nb = MIX_CTX_BATCH if B % MIX_CTX_BATCH == 0 else FFN_TILE // CTX_SEQ
    states = None
    for l in range(L):
        lam_init = 0.8 - 0.6 * math.exp(-0.3 * l)
        last = l == L - 1
        ada4 = ada.reshape(ADA_ROWS, 3, 3, D_MODEL)
        xi, xs, *st = _ffn_call(xi, xs, ada4, l, 0, n1, ffn1_w_gate, ffn1_w_up, ffn1_w_down, None, "ffn1",
                                fill_layers=L if l == 0 else 0)
        states = tuple(st) if l == 0 else states
        xi, states = _mix_i_call(xi, ada4, l, L, nm, win, qn, wuq, kvn, wukv, lamv, subln, wout, states, nb, lam_init)
        xs = _mix_ii_call(xs, ada4, l, nm, win, qn, wuq, kvn, wukv, lamv, subln, wout, rope_tabs,
                          rpb_rows, caches, lam_init)
        xi, xs, *nxt = _ffn_call(xi, xs, ada4, l, 2, n2, ffn2_w_gate, ffn2_w_up, ffn2_w_down,
                                 fnorm if last else None, "ffn2",
                                 next_ada=None if last else (cvec, w_ada, b_ada3))
        ada = None if last else nxt[0]

    states = (states[0],) + tuple(jnp.swapaxes(s, -1, -2) for s in states[1:])
    return (xi.reshape(B, CTX_SEQ, D_MODEL), xs.reshape(Bd, LAT_SEQ, D_MODEL)) + states
```

```python
import functools
import math

import numpy as np
import jax
import jax.numpy as jnp
from jax import lax
from jax.experimental import pallas as pl
from jax.experimental.pallas import tpu as pltpu

F32 = jnp.float32
BF16 = jnp.bfloat16

D_MODEL = 1024
FFN_DIM = 2816
NORM_EPS = 1e-6
ROPE_THETA = 10000.0
GRID_W = 64
N_ADA = 9
NEG_INF = -1e30

MLA_HEADS = 4
MLA_Q_LORA = 256
MLA_KV_LORA = 128
MLA_NOPE = 64
MLA_ROPE = 32
MLA_V = 64
DIFF_HEADS = 4
DIFF_DK = 32
DIFF_DV = 64
NAT_HEADS = 8
NAT_HD = 64
NAT_WIN_ROWS = 8
NAT_WIN_COLS = 16

LOG2E = math.log2(math.e)
MLA_SCALE = (MLA_NOPE + MLA_ROPE) ** -0.5 * LOG2E
DIFF_SCALE = DIFF_DK ** -0.5 * LOG2E
NAT_SCALE = NAT_HD ** -0.5 * LOG2E

CTX_SEQ = 256
LAT_SEQ = 1024
LAT_ROWS = LAT_SEQ // GRID_W

LANES = 128
MXU_DIM = 256
VMEM_LIMIT_BYTES = 58 * 1024 * 1024

P_CQ = 0
P_CKV = 256
P_KR = 384
P_DQ = 512
P_DK = 768
P_DV = 1024
P_NQ = 1280
P_NK = 1792
P_NV = 2304
P_COLS = 2816
P_SPLIT = 512
HEAD_BLOCK = 256
ADA_ROWS = 8

FFN_CHUNKS = ((0, 512), (512, 1024), (1024, 1536), (1536, 2048), (2048, 2560), (2560, 2816))
FFN_TILE = 512
N_STATES = 6
MIX_CTX_BATCH = 2


def _rope_tables():
    t = np.arange(LAT_SEQ)
    pos = np.stack([t // GRID_W, t % GRID_W], axis=0).astype(np.float64)
    lane = np.arange(LANES)
    p = lane % 32
    axis = (p >= 16).astype(np.int64)
    freqs = ROPE_THETA ** (-(p % 8).astype(np.float64) / 8.0)
    ang = pos[axis, :].T * freqs[None, :]
    first = (p % 16) < 8
    cos = np.cos(ang)
    sin = np.sin(ang)
    s_next = np.where(first[None, :], -sin, 0.0)
    s_prev = np.where(first[None, :], 0.0, sin)
    return cos.astype(np.float32), s_next.astype(np.float32), s_prev.astype(np.float32)


_ROPE_COS, _ROPE_SNEXT, _ROPE_SPREV = _rope_tables()


def _rmsnorm(x, g):
    ms = jnp.mean(x * x, axis=-1, keepdims=True)
    return x * lax.rsqrt(ms + NORM_EPS) * g


def _silu(x):
    return x / (1.0 + jnp.exp(-x))


def _dot(a, b):
    return jnp.dot(a, b, preferred_element_type=F32)


def _dot_t(a, b):
    return lax.dot_general(a, b, (((1,), (1,)), ((), ())), preferred_element_type=F32)


class _Proj:
    def __init__(self, a, b):
        self.a, self.b = a, b

    def __getitem__(self, idx):
        lo, hi = idx[1].start, idx[1].stop
        return self.a[:, lo:hi] if hi <= P_SPLIT else self.b[:, lo - P_SPLIT:hi - P_SPLIT]


def _lane_mask(width, lo, hi):
    lane = lax.broadcasted_iota(jnp.int32, (1, width), 1)
    return (lane >= lo) & (lane < hi)


def _head_masks(width=HEAD_BLOCK, group=64, n=4):
    return [_lane_mask(width, h * group, (h + 1) * group) for h in range(n)]


def _mla_qmasks():
    lane = lax.broadcasted_iota(jnp.int32, (1, HEAD_BLOCK + LANES), 1)
    out = []
    for h in range(MLA_HEADS):
        nope = (lane >= h * MLA_NOPE) & (lane < (h + 1) * MLA_NOPE)
        rope = (lane >= HEAD_BLOCK + h * MLA_ROPE) & (lane < HEAD_BLOCK + (h + 1) * MLA_ROPE)
        out.append(nope | rope)
    return out


def _tile32(blk):
    return blk + pltpu.roll(blk, 32, 1) + pltpu.roll(blk, 64, 1) + pltpu.roll(blk, 96, 1)


def _rope(x, cos, s_next, s_prev):
    return x * cos + pltpu.roll(x, LANES - 8, 1) * s_next + pltpu.roll(x, 8, 1) * s_prev


def _stack_masked(q, masks):
    zero = jnp.zeros_like(q)
    return jnp.concatenate([jnp.where(m, q, zero) for m in masks], axis=0)


def _scores_t(q, k_bf, kmasks, stacked):
    sk = k_bf.shape[0]
    if stacked:
        s = _dot_t(_stack_masked(k_bf, kmasks), q)
        return [s[h * sk:(h + 1) * sk] for h in range(len(kmasks))]
    zero = jnp.zeros_like(k_bf)
    return [_dot_t(jnp.where(m, k_bf, zero), q) for m in kmasks]


def _pv_t(vt_bf, p_bf, h, short_keys):
    if short_keys:
        return _dot(vt_bf, p_bf)[h * 64:(h + 1) * 64]
    return _dot(vt_bf[h * 64:(h + 1) * 64], p_bf)


def _exp_sum_t(s):
    e = jnp.exp2(s - jnp.max(s, axis=0, keepdims=True))
    return e, jnp.sum(e, axis=0, keepdims=True)


def _softmax_heads_t(q, k_bf, vt_bf, kmasks, stacked):
    parts = []
    for h, s in enumerate(_scores_t(q, k_bf, kmasks, stacked)):
        e, l = _exp_sum_t(s)
        parts.append(_pv_t(vt_bf, e.astype(BF16), h, stacked) / l)
    return jnp.concatenate(parts, axis=0)


def _diff_heads_t(q, k_bf, vt_bf, lam, subln_col, lam_init, stacked):
    n = DIFF_HEADS
    m1 = [_lane_mask(HEAD_BLOCK, h * 2 * DIFF_DK, h * 2 * DIFF_DK + DIFF_DK) for h in range(n)]
    m2 = [_lane_mask(HEAD_BLOCK, h * 2 * DIFF_DK + DIFF_DK, (h + 1) * 2 * DIFF_DK) for h in range(n)]
    s = _scores_t(q, k_bf, m1 + m2, stacked)
    parts = []
    for h in range(n):
        e1, l1 = _exp_sum_t(s[h])
        e2, l2 = _exp_sum_t(s[n + h])
        p = e1 * (1.0 / l1) - e2 * (lam / l2)
        oh = _pv_t(vt_bf, p.astype(BF16), h, stacked)
        ms = jnp.mean(oh * oh, axis=0, keepdims=True)
        parts.append(oh * lax.rsqrt(ms + NORM_EPS))
    return jnp.concatenate(parts, axis=0) * subln_col * (1.0 - lam_init)


def _lambda(lamv, lam_init):
    a = jnp.sum(lamv[0:1] * lamv[1:2], axis=-1, keepdims=True)
    b = jnp.sum(lamv[2:3] * lamv[3:4], axis=-1, keepdims=True)
    return jnp.exp(a) - jnp.exp(b) + lam_init


def _ada_block(c_ref, w_ref, b_ref):
    return _dot(_silu(c_ref[...]).astype(BF16), w_ref[...].astype(BF16)) + b_ref[...]


def _ada_kernel(c_ref, w_ref, b_ref, o_ref):
    o_ref[...] = _ada_block(c_ref, w_ref, b_ref)


def _ada_call(cvec, w_ada, b_ada3, layer):
    n = w_ada.shape[2]
    tn = 1024
    return pl.pallas_call(
        _ada_kernel,
        grid=(n // tn,),
        in_specs=[
            pl.BlockSpec((ADA_ROWS, D_MODEL), lambda j: (0, 0)),
            pl.BlockSpec((None, D_MODEL, tn), lambda j: (layer, 0, j)),
            pl.BlockSpec((None, 1, tn), lambda j: (layer, 0, j)),
        ],
        out_specs=pl.BlockSpec((ADA_ROWS, tn), lambda j: (0, j)),
        out_shape=jax.ShapeDtypeStruct((ADA_ROWS, n), F32),
        compiler_params=pltpu.CompilerParams(
            dimension_semantics=("arbitrary",), vmem_limit_bytes=VMEM_LIMIT_BYTES),
        name="ada",
    )(cvec, w_ada, b_ada3)


def _ffn_tile(x_ref, o_ref, mod, g_ref, wg_ref, wu_ref, wd_ref, fn_ref, ada_io=None):
    if ada_io is not None:
        ada_io[3][...] = _ada_block(*ada_io[:3])
    x = x_ref[...]
    h = (_rmsnorm(x, g_ref[...]) * (1.0 + mod[1:2]) + mod[0:1]).astype(BF16)
    acc = None
    for lo, hi in FFN_CHUNKS:
        g = _dot(h, wg_ref[:, lo:hi].astype(BF16))
        u = _dot(h, wu_ref[:, lo:hi].astype(BF16))
        part = _dot((_silu(g) * u).astype(BF16), wd_ref[lo:hi, :].astype(BF16))
        acc = part if acc is None else acc + part
    y = x + (0.5 * mod[2:3]) * acc
    if fn_ref is not None:
        y = _rmsnorm(y, fn_ref[...])
    o_ref[...] = y


def _state_fill_copies(t, state_refs, zero_refs, sem, n_layers):
    per_tile = FFN_TILE // CTX_SEQ
    copies = []
    for j in range(per_tile):
        for l in range(n_layers):
            for st, z in zip(state_refs, zero_refs):
                dst = st.at[t * per_tile + j, l]
                if dst.shape == z.shape:
                    copies.append(pltpu.make_async_copy(z, dst, sem))
                else:
                    nh = z.shape[0]
                    copies += [pltpu.make_async_copy(z, dst.at[h:h + nh], sem) for h in range(0, dst.shape[0], nh)]
    return copies


def _ffn_kernel(xc_ref, xl_ref, mod_ref, g_ref, wg_ref, wu_ref, wd_ref, *rest,
                n_ctx_tiles, final, fill_layers, next_ada):
    rest = list(rest)
    fn_ref = rest.pop(0) if final else None
    ada_in = [rest.pop(0) for _ in range(3)] if next_ada else None
    oc_ref, ol_ref = rest.pop(0), rest.pop(0)
    state_refs = [rest.pop(0) for _ in range(N_STATES)] if fill_layers else None
    ada_io = ada_in + [rest.pop(0)] if next_ada else None
    mod = mod_ref[...]
    t = pl.program_id(0)
    if fill_layers:
        z_ckv, z_kr, z_heads, zsem = rest

        @pl.when(t == 0)
        def _():
            for z in (z_ckv, z_kr, z_heads):
                z[...] = jnp.zeros(z.shape, z.dtype)

    @pl.when(t < n_ctx_tiles)
    def _():
        fill = []
        if fill_layers:
            fill = _state_fill_copies(t, state_refs, (z_ckv, z_kr, z_heads, z_heads, z_heads, z_heads), zsem,
                                      fill_layers)
        for c in fill:
            c.start()
        _ffn_tile(xc_ref, oc_ref, mod, g_ref, wg_ref, wu_ref, wd_ref, fn_ref, ada_io)
        for c in fill:
            c.wait()

    @pl.when(t >= n_ctx_tiles)
    def _():
        _ffn_tile(xl_ref, ol_ref, mod, g_ref, wg_ref, wu_ref, wd_ref, fn_ref, ada_io)


def _resident(shape, index_map):
    return pl.BlockSpec(shape, index_map, pipeline_mode=pl.Buffered(1))


def _state_shapes(B, L):
    return [jax.ShapeDtypeStruct(s, F32) for s in (
        (B, L, CTX_SEQ, MLA_KV_LORA), (B, L, MLA_ROPE, CTX_SEQ),
        (B, L, DIFF_HEADS, 2 * DIFF_DK, CTX_SEQ), (B, L, DIFF_HEADS, DIFF_DV, CTX_SEQ),
        (B, L, NAT_HEADS, NAT_HD, CTX_SEQ), (B, L, NAT_HEADS, NAT_HD, CTX_SEQ))]


def _ffn_call(xc, xl, ada4, layer, group, norm, wg, wu, wd, final_norm, name, fill_layers=0, next_ada=None):
    tm = FFN_TILE
    n_ctx = xc.shape[0] // tm
    n_lat = xl.shape[0] // tm
    per_lat = LAT_SEQ // tm
    ctx_blk = lambda t: (jnp.minimum(t, n_ctx - 1), 0)
    lat_blk = lambda t: (jnp.maximum(t - n_ctx, 0), 0)
    ada_row = lambda t: jnp.maximum(t - n_ctx, -per_lat) // per_lat + 1
    in_specs = [
        pl.BlockSpec((tm, D_MODEL), ctx_blk),
        pl.BlockSpec((tm, D_MODEL), lat_blk),
        pl.BlockSpec((None, None, 3, D_MODEL), lambda t: (ada_row(t), group, 0, 0)),
        _resident((None, 1, D_MODEL), lambda t: (layer, 0, 0)),
        _resident((None, D_MODEL, FFN_DIM), lambda t: (layer, 0, 0)),
        _resident((None, D_MODEL, FFN_DIM), lambda t: (layer, 0, 0)),
        _resident((None, FFN_DIM, D_MODEL), lambda t: (layer, 0, 0)),
    ]
    args = [xc, xl, ada4, norm, wg, wu, wd]
    if final_norm is not None:
        in_specs.append(_resident((1, D_MODEL), lambda t: (0, 0)))
        args.append(final_norm)
    if next_ada is not None:
        n_ada = next_ada[1].shape[2]
        tn = n_ada // (n_ctx + n_lat)
        assert tn * (n_ctx + n_lat) == n_ada and tn % LANES == 0
        in_specs += [_resident((ADA_ROWS, D_MODEL), lambda t: (0, 0)),
                     pl.BlockSpec((None, D_MODEL, tn), lambda t: (layer + 1, 0, t)),
                     pl.BlockSpec((None, 1, tn), lambda t: (layer + 1, 0, t))]
        args += list(next_ada)
    out_specs = [pl.BlockSpec((tm, D_MODEL), ctx_blk), pl.BlockSpec((tm, D_MODEL), lat_blk)]
    out_shape = [jax.ShapeDtypeStruct(xc.shape, F32), jax.ShapeDtypeStruct(xl.shape, F32)]
    scratch = []
    if fill_layers:
        states = _state_shapes(xc.shape[0] // CTX_SEQ, fill_layers)
        out_specs += [pl.BlockSpec(memory_space=pl.ANY)] * N_STATES
        out_shape += states
        scratch += [pltpu.VMEM(states[k].shape[2:], F32) for k in (0, 1, 2)] + [pltpu.SemaphoreType.DMA(())]
    if next_ada is not None:
        out_specs.append(pl.BlockSpec((ADA_ROWS, tn), lambda t: (0, t)))
        out_shape.append(jax.ShapeDtypeStruct((ADA_ROWS, n_ada), F32))
    return pl.pallas_call(
        functools.partial(_ffn_kernel, n_ctx_tiles=n_ctx, final=final_norm is not None, fill_layers=fill_layers,
                          next_ada=next_ada is not None),
        grid=(n_ctx + n_lat,),
        in_specs=in_specs,
        out_specs=out_specs,
        out_shape=out_shape,
        scratch_shapes=scratch,
        compiler_params=pltpu.CompilerParams(
            dimension_semantics=("arbitrary",), vmem_limit_bytes=VMEM_LIMIT_BYTES),
        name=name,
    )(*args)


def _mix_i_kernel(x_ref, mod_ref, g_ref, wina_ref, winb_ref, qn_ref, wuq_ref, kvn_ref, wukv_ref, lamv_ref, subln_ref,
                  wout_ref, *refs, nb, lam_init):
    y_ref, ckv_ref, kr_ref, dk_ref, dv_ref, nk_ref, nv_ref = refs[-7:]
    S = CTX_SEQ
    x = x_ref[...]
    mod = mod_ref[...]
    h = (_rmsnorm(x, g_ref[...]) * (1.0 + mod[1:2]) + mod[0:1]).astype(BF16)
    proj = _Proj(_dot(h, wina_ref[...]), _dot(h, winb_ref[...]))
    q_cat = _dot(_rmsnorm(proj[:, P_CQ:P_CQ + MLA_Q_LORA], qn_ref[...]).astype(BF16), wuq_ref[...])
    ckv = _rmsnorm(proj[:, P_CKV:P_CKV + MLA_KV_LORA], kvn_ref[...])
    kv = _dot(ckv.astype(BF16), wukv_ref[...])
    kr_blk = proj[:, P_KR:P_KR + LANES]
    k_cat = jnp.concatenate([kv[:, 0:HEAD_BLOCK], _tile32(kr_blk)], axis=1).astype(BF16)
    v_mla = kv[:, HEAD_BLOCK:2 * HEAD_BLOCK]
    q_cat = (q_cat * MLA_SCALE).astype(BF16)
    dq = (proj[:, P_DQ:P_DQ + HEAD_BLOCK] * DIFF_SCALE).astype(BF16)
    dk = proj[:, P_DK:P_DK + HEAD_BLOCK]
    dv = proj[:, P_DV:P_DV + HEAD_BLOCK]
    nq = (proj[:, P_NQ:P_NQ + 2 * HEAD_BLOCK] * NAT_SCALE).astype(BF16)
    nk = proj[:, P_NK:P_NK + 2 * HEAD_BLOCK]
    nv = proj[:, P_NV:P_NV + 2 * HEAD_BLOCK]
    dk_bf, nk_bf = dk.astype(BF16), nk.astype(BF16)
    lam = _lambda(lamv_ref[...], lam_init)
    hmasks = _head_masks()
    mla_qm = _mla_qmasks()

    outs = []
    for j in range(nb):
        r0, r1 = j * S, (j + 1) * S
        ckv_ref[j] = ckv[r0:r1]
        kr_ref[j] = kr_blk[r0:r1].T[0:MLA_ROPE]
        dk_t, dv_t = dk[r0:r1].T, dv[r0:r1].T
        nk_t = [nk[r0:r1, b * HEAD_BLOCK:(b + 1) * HEAD_BLOCK].T for b in range(2)]
        nv_t = [nv[r0:r1, b * HEAD_BLOCK:(b + 1) * HEAD_BLOCK].T for b in range(2)]
        for hh in range(DIFF_HEADS):
            dk_ref[j, hh] = dk_t[hh * 64:(hh + 1) * 64]
            dv_ref[j, hh] = dv_t[hh * 64:(hh + 1) * 64]
        for hh in range(NAT_HEADS):
            nk_ref[j, hh] = nk_t[hh // 4][(hh % 4) * 64:(hh % 4 + 1) * 64]
            nv_ref[j, hh] = nv_t[hh // 4][(hh % 4) * 64:(hh % 4 + 1) * 64]
        ot_mla = _softmax_heads_t(q_cat[r0:r1], k_cat[r0:r1], v_mla[r0:r1].T.astype(BF16), mla_qm, True)
        ot_diff = _diff_heads_t(dq[r0:r1], dk_bf[r0:r1], dv_t.astype(BF16), lam, subln_ref[...], lam_init, True)
        ot_nat = [
            _softmax_heads_t(nq[r0:r1, b * HEAD_BLOCK:(b + 1) * HEAD_BLOCK], nk_bf[r0:r1, b * HEAD_BLOCK:(b + 1) * HEAD_BLOCK],
                             nv_t[b].astype(BF16), hmasks, True)
            for b in range(2)
        ]
        outs.append(jnp.concatenate([ot_mla, ot_diff] + ot_nat, axis=0).T.astype(BF16))
    o = jnp.concatenate(outs, axis=0) if nb > 1 else outs[0]
    y_ref[...] = x + mod[2:3] * _dot(o, wout_ref[...])


def _mix_i_call(x, ada4, layer, n_layers, mix_norm, win, qn, wuq, kvn, wukv, lamv, subln, wout, states, nb, lam_init):
    T = x.shape[0]
    B = T // CTX_SEQ
    tm = nb * CTX_SEQ
    lyr = lambda shape: _resident((None,) + shape, lambda i: (layer,) + (0,) * len(shape))
    any_spec = pl.BlockSpec(memory_space=pl.ANY)
    in_specs = [
        pl.BlockSpec((tm, D_MODEL), lambda i: (i, 0)),
        _resident((None, None, 3, D_MODEL), lambda i: (0, 1, 0, 0)),
        lyr((1, D_MODEL)),
        lyr((D_MODEL, P_SPLIT)),
        lyr((D_MODEL, P_COLS - P_SPLIT)),
        lyr((1, MLA_Q_LORA)),
        lyr((MLA_Q_LORA, HEAD_BLOCK + LANES)),
        lyr((1, MLA_KV_LORA)),
        lyr((MLA_KV_LORA, 2 * HEAD_BLOCK)),
        lyr((4, DIFF_DK)),
        lyr((HEAD_BLOCK, 1)),
        lyr((D_MODEL, D_MODEL)),
    ]
    n_fixed = len(in_specs)
    in_specs += [any_spec] * len(states)
    out_specs = [
        pl.BlockSpec((tm, D_MODEL), lambda i: (i, 0)),
        pl.BlockSpec((nb, None, CTX_SEQ, MLA_KV_LORA), lambda i: (i, layer, 0, 0)),
        pl.BlockSpec((nb, None, MLA_ROPE, CTX_SEQ), lambda i: (i, layer, 0, 0)),
        pl.BlockSpec((nb, None, DIFF_HEADS, 2 * DIFF_DK, CTX_SEQ), lambda i: (i, layer, 0, 0, 0)),
        pl.BlockSpec((nb, None, DIFF_HEADS, DIFF_DV, CTX_SEQ), lambda i: (i, layer, 0, 0, 0)),
        pl.BlockSpec((nb, None, NAT_HEADS, NAT_HD, CTX_SEQ), lambda i: (i, layer, 0, 0, 0)),
        pl.BlockSpec((nb, None, NAT_HEADS, NAT_HD, CTX_SEQ), lambda i: (i, layer, 0, 0, 0)),
    ]
    out_shape = [jax.ShapeDtypeStruct((T, D_MODEL), F32)] + _state_shapes(B, n_layers)
    res = pl.pallas_call(
        functools.partial(_mix_i_kernel, nb=nb, lam_init=lam_init),
        grid=(B // nb,),
        in_specs=in_specs,
        out_specs=out_specs,
        out_shape=out_shape,
        input_output_aliases={n_fixed + k: 1 + k for k in range(len(states))},
        compiler_params=pltpu.CompilerParams(
            dimension_semantics=("arbitrary",), vmem_limit_bytes=VMEM_LIMIT_BYTES),
        name="mix_ctx",
    )(x, ada4, mix_norm, *win, qn, wuq, kvn, wukv, lamv, subln, wout, *states)
    return res[0], tuple(res[1:])


PROJ_ROWS = 256
DENSE_QROWS = MXU_DIM
N_KEYS = CTX_SEQ + LAT_SEQ
BIAS_LANES = NAT_WIN_ROWS * LANES


def _mix_ii_kernel(x_ref, mod_ref, g_ref, wina_ref, winb_ref, qn_ref, wuq_ref, kvn_ref, wukv_ref, lamv_ref, subln_ref,
                   wout_ref, cos_ref, snext_ref, sprev_ref, rpb_ref,
                   cckv_ref, ckr_ref, cdk_ref, cdv_ref, cnk_ref, cnv_ref,
                   y_ref,
                   qcat_s, ckv_s, kcat_s, vmla_s, dq_s, dk_s, dv_s, nq_s, nk_s, nv_s, o_s, bias_ref,
                   vmlat_s, dvt_s, cnv_s, *, lam_init):
    S = LAT_SEQ
    C = CTX_SEQ
    mod = mod_ref[...]
    lam = _lambda(lamv_ref[...], lam_init)
    hmasks = _head_masks()
    mla_qm = _mla_qmasks()

    @pl.when(pl.program_id(0) == 0)
    def _():
        qc = lax.broadcasted_iota(jnp.int32, (GRID_W, LANES), 0)
        kc = lax.broadcasted_iota(jnp.int32, (GRID_W, LANES), 1) % GRID_W
        c_start = jnp.clip(qc - NAT_WIN_COLS // 2, 0, GRID_W - NAT_WIN_COLS)
        in_win = (kc >= c_start) & (kc < c_start + NAT_WIN_COLS)
        for par in range(2):
            for h in range(NAT_HEADS):
                for p in range(NAT_WIN_ROWS):
                    row = jnp.broadcast_to(rpb_ref[par, h, p:p + 1, :], (GRID_W, LANES))
                    tile = pltpu.roll(row, LANES - (NAT_WIN_COLS - 1), 1, stride=1, stride_axis=0)
                    bias_ref[par, h // 4, (h % 4) * GRID_W:(h % 4 + 1) * GRID_W, p * LANES:(p + 1) * LANES] = (
                        jnp.where(in_win, tile * LOG2E, NEG_INF))

    ckv_s[0:C, :] = cckv_ref[...]
    kr_t = ckr_ref[...]
    kcat_s[0:C, HEAD_BLOCK:] = jnp.concatenate([kr_t] * (LANES // MLA_ROPE), axis=0).T.astype(BF16)
    dk_s[0:C, :] = cdk_ref[...].T.astype(BF16)
    dvt_s[:, 0:C] = cdv_ref[...].astype(BF16)
    cnv_s[...] = cnv_ref[...].T.astype(BF16)

    def proj_body(i, carry):
        r = pl.multiple_of(i * PROJ_ROWS, PROJ_ROWS)
        rows = pl.ds(r, PROJ_ROWS)
        krows = pl.ds(C + r, PROJ_ROWS)
        cos, s_next, s_prev = cos_ref[rows, :], snext_ref[rows, :], sprev_ref[rows, :]
        rope = lambda v: _rope(v, cos, s_next, s_prev)
        x = x_ref[rows, :]
        h = (_rmsnorm(x, g_ref[...]) * (1.0 + mod[1:2]) + mod[0:1]).astype(BF16)
        proj = _Proj(_dot(h, wina_ref[...]), _dot(h, winb_ref[...]))
        q_cat = _dot(_rmsnorm(proj[:, P_CQ:P_CQ + MLA_Q_LORA], qn_ref[...]).astype(BF16), wuq_ref[...])
        qcat_s[rows, 0:HEAD_BLOCK] = (q_cat[:, 0:HEAD_BLOCK] * MLA_SCALE).astype(BF16)
        qcat_s[rows, HEAD_BLOCK:] = (rope(q_cat[:, HEAD_BLOCK:]) * MLA_SCALE).astype(BF16)
        ckv_s[krows, :] = _rmsnorm(proj[:, P_CKV:P_CKV + MLA_KV_LORA], kvn_ref[...]).astype(BF16)
        kcat_s[krows, HEAD_BLOCK:] = _tile32(rope(proj[:, P_KR:P_KR + LANES])).astype(BF16)
        for b in range(2):
            c0 = b * LANES
            dq_s[rows, c0:c0 + LANES] = (rope(proj[:, P_DQ + c0:P_DQ + c0 + LANES]) * DIFF_SCALE).astype(BF16)
            dk_s[krows, c0:c0 + LANES] = rope(proj[:, P_DK + c0:P_DK + c0 + LANES]).astype(BF16)
        dv_s[krows, :] = proj[:, P_DV:P_DV + HEAD_BLOCK].astype(BF16)
        nq_s[rows, :] = (proj[:, P_NQ:P_NQ + 2 * HEAD_BLOCK] * NAT_SCALE).astype(BF16)
        nk_s[rows, :] = proj[:, P_NK:P_NK + 2 * HEAD_BLOCK].astype(BF16)
        nv_s[rows, :] = proj[:, P_NV:P_NV + 2 * HEAD_BLOCK].astype(BF16)
        return carry

    lax.fori_loop(0, S // PROJ_ROWS, proj_body, 0)

    def kv_body(i, carry):
        rows = pl.ds(pl.multiple_of(i * PROJ_ROWS, PROJ_ROWS), PROJ_ROWS)
        kv = _dot(ckv_s[rows, :], wukv_ref[...])
        kcat_s[rows, 0:HEAD_BLOCK] = kv[:, 0:HEAD_BLOCK].astype(BF16)
        vmla_s[rows, :] = kv[:, HEAD_BLOCK:].astype(BF16)
        return carry

    lax.fori_loop(0, N_KEYS // PROJ_ROWS, kv_body, 0)

    for j in range(N_KEYS // PROJ_ROWS):
        c0, c1 = j * PROJ_ROWS, (j + 1) * PROJ_ROWS
        vmlat_s[:, c0:c1] = vmla_s[c0:c1, :].astype(F32).T.astype(BF16)
        if c0 >= C:
            dvt_s[:, c0:c1] = dv_s[c0:c1, :].astype(F32).T.astype(BF16)

    def dense_body(i, carry):
        rows = pl.ds(pl.multiple_of(i * DENSE_QROWS, DENSE_QROWS), DENSE_QROWS)
        ot_mla = _softmax_heads_t(qcat_s[rows, :], kcat_s[...], vmlat_s[...], mla_qm, False)
        o_s[rows, 0:HEAD_BLOCK] = ot_mla.T.astype(BF16)
        ot_diff = _diff_heads_t(dq_s[rows, :], dk_s[...], dvt_s[...], lam, subln_ref[...], lam_init, False)
        o_s[rows, HEAD_BLOCK:2 * HEAD_BLOCK] = ot_diff.T.astype(BF16)
        return carry

    lax.fori_loop(0, S // DENSE_QROWS, dense_body, 0)

    for r in range(LAT_ROWS):
        rs = min(max(r - NAT_WIN_ROWS // 2, 0), LAT_ROWS - NAT_WIN_ROWS)
        dr0 = rs - r + (NAT_WIN_ROWS - 1)
        par = dr0 % 2
        off = GRID_W * (dr0 - par)
        q0, q1 = r * GRID_W, (r + 1) * GRID_W
        k0, k1 = rs * GRID_W, (rs + NAT_WIN_ROWS) * GRID_W
        for b in range(2):
            c0, c1 = b * HEAD_BLOCK, (b + 1) * HEAD_BLOCK
            q = nq_s[q0:q1, c0:c1]
            zero = jnp.zeros_like(q)
            qs = jnp.concatenate([jnp.where(m, q, zero) for m in hmasks], axis=0)
            sw = _dot_t(qs, nk_s[k0:k1, c0:c1]) + bias_ref[par, b, :, off:off + NAT_WIN_ROWS * GRID_W]
            sc = _dot(qs, cnk_ref[c0:c1, :].astype(BF16))
            m = jnp.maximum(jnp.max(sw, axis=-1, keepdims=True), jnp.max(sc, axis=-1, keepdims=True))
            ew = jnp.exp2(sw - m)
            ec = jnp.exp2(sc - m)
            l = jnp.sum(ew, axis=-1, keepdims=True) + jnp.sum(ec, axis=-1, keepdims=True)
            o = (_dot(ew.astype(BF16), nv_s[k0:k1, c0:c1]) + _dot(ec.astype(BF16), cnv_s[:, c0:c1])) / l
            of = None
            for hh, hm in enumerate(hmasks):
                part = jnp.where(hm, o[hh * GRID_W:(hh + 1) * GRID_W], 0.0)
                of = part if of is None else of + part
            o_s[q0:q1, 2 * HEAD_BLOCK + c0:2 * HEAD_BLOCK + c1] = of.astype(BF16)

    def out_body(i, carry):
        rows = pl.ds(pl.multiple_of(i * PROJ_ROWS, PROJ_ROWS), PROJ_ROWS)
        y_ref[rows, :] = x_ref[rows, :] + mod[2:3] * _dot(o_s[rows, :], wout_ref[...])
        return carry

    lax.fori_loop(0, S // PROJ_ROWS, out_body, 0)


def _mix_ii_call(x, ada4, layer, mix_norm, win, qn, wuq, kvn, wukv, lamv, subln, wout, rope_tabs, rpb_rows,
                 caches, lam_init):
    T = x.shape[0]
    Bd = T // LAT_SEQ
    cckv, ckr, cdk, cdv, cnk, cnv = caches
    lyr = lambda shape: _resident((None,) + shape, lambda i: (layer,) + (0,) * len(shape))
    cache = lambda rows, width: pl.BlockSpec((None, None, rows, width), lambda i: (i, layer, 0, 0),
                                             pipeline_mode=pl.Buffered(1))
    tab = _resident((LAT_SEQ, LANES), lambda i: (0, 0))
    in_specs = [
        pl.BlockSpec((LAT_SEQ, D_MODEL), lambda i: (i, 0), pipeline_mode=pl.Buffered(1)),
        pl.BlockSpec((None, None, 3, D_MODEL), lambda i: (i + 1, 1, 0, 0)),
        lyr((1, D_MODEL)),
        lyr((D_MODEL, P_SPLIT)),
        lyr((D_MODEL, P_COLS - P_SPLIT)),
        lyr((1, MLA_Q_LORA)),
        lyr((MLA_Q_LORA, HEAD_BLOCK + LANES)),
        lyr((1, MLA_KV_LORA)),
        lyr((MLA_KV_LORA, 2 * HEAD_BLOCK)),
        lyr((4, DIFF_DK)),
        lyr((HEAD_BLOCK, 1)),
        lyr((D_MODEL, D_MODEL)),
        tab, tab, tab,
        lyr((2, NAT_HEADS, NAT_WIN_ROWS, LANES)),
        cache(CTX_SEQ, MLA_KV_LORA), cache(MLA_ROPE, CTX_SEQ), cache(HEAD_BLOCK, CTX_SEQ), cache(HEAD_BLOCK, CTX_SEQ),
        cache(2 * HEAD_BLOCK, CTX_SEQ), cache(2 * HEAD_BLOCK, CTX_SEQ),
    ]
    scratch = [
        pltpu.VMEM((LAT_SEQ, HEAD_BLOCK + LANES), BF16),
        pltpu.VMEM((N_KEYS, MLA_KV_LORA), BF16),
        pltpu.VMEM((N_KEYS, HEAD_BLOCK + LANES), BF16),
        pltpu.VMEM((N_KEYS, HEAD_BLOCK), BF16),
        pltpu.VMEM((LAT_SEQ, HEAD_BLOCK), BF16),
        pltpu.VMEM((N_KEYS, HEAD_BLOCK), BF16),
        pltpu.VMEM((N_KEYS, HEAD_BLOCK), BF16),
        pltpu.VMEM((LAT_SEQ, 2 * HEAD_BLOCK), BF16),
        pltpu.VMEM((LAT_SEQ, 2 * HEAD_BLOCK), BF16),
        pltpu.VMEM((LAT_SEQ, 2 * HEAD_BLOCK), BF16),
        pltpu.VMEM((LAT_SEQ, D_MODEL), BF16),
        pltpu.VMEM((2, 2, HEAD_BLOCK, BIAS_LANES), F32),
        pltpu.VMEM((HEAD_BLOCK, N_KEYS), BF16),
        pltpu.VMEM((HEAD_BLOCK, N_KEYS), BF16),
        pltpu.VMEM((CTX_SEQ, 2 * HEAD_BLOCK), BF16),
    ]
    return pl.pallas_call(
        functools.partial(_mix_ii_kernel, lam_init=lam_init),
        grid=(Bd,),
        in_specs=in_specs,
        out_specs=pl.BlockSpec((LAT_SEQ, D_MODEL), lambda i: (i, 0)),
        out_shape=jax.ShapeDtypeStruct((T, D_MODEL), F32),
        scratch_shapes=scratch,
        compiler_params=pltpu.CompilerParams(
            dimension_semantics=("arbitrary",), vmem_limit_bytes=VMEM_LIMIT_BYTES),
        name="mix_lat",
    )(x, ada4, mix_norm, *win, qn, wuq, kvn, wukv, lamv, subln, wout, *rope_tabs, rpb_rows, *caches)


W_IN_SPLIT = MLA_Q_LORA + MLA_KV_LORA + MLA_ROPE


def _win_prep_kernel(wt_ref, a_ref, b_ref):
    tr = lambda r0: wt_ref[r0:r0 + MXU_DIM, :].T
    a_ref[:, 0:MXU_DIM] = tr(0).astype(BF16)
    live = _lane_mask(MXU_DIM, 0, W_IN_SPLIT - MXU_DIM)
    a_ref[:, MXU_DIM:P_SPLIT] = jnp.where(live, tr(MXU_DIM), 0.0).astype(BF16)
    for j in range((P_COLS - P_SPLIT) // MXU_DIM):
        b_ref[:, j * MXU_DIM:(j + 1) * MXU_DIM] = tr(W_IN_SPLIT + j * MXU_DIM).astype(BF16)


def _win_prep_call(w_in):
    L, D, n = w_in.shape
    assert n - W_IN_SPLIT == P_COLS - P_SPLIT and P_SPLIT == 2 * MXU_DIM
    return pl.pallas_call(
        _win_prep_kernel,
        grid=(L,),
        in_specs=[pl.BlockSpec((None, n, D), lambda l: (l, 0, 0))],
        out_specs=[pl.BlockSpec((None, D, P_SPLIT), lambda l: (l, 0, 0)),
                   pl.BlockSpec((None, D, P_COLS - P_SPLIT), lambda l: (l, 0, 0))],
        out_shape=[jax.ShapeDtypeStruct((L, D, P_SPLIT), BF16), jax.ShapeDtypeStruct((L, D, P_COLS - P_SPLIT), BF16)],
        compiler_params=pltpu.CompilerParams(
            dimension_semantics=("arbitrary",), vmem_limit_bytes=VMEM_LIMIT_BYTES),
        name="win_prep",
    )(jnp.swapaxes(w_in, 1, 2))


def _rpb_rows(rpb):
    L = rpb.shape[0]
    n_dr, n_dc = 2 * NAT_WIN_ROWS - 1, 2 * NAT_WIN_COLS - 1
    padded = jnp.pad(rpb.astype(F32), ((0, 0), (0, 0), (0, 2 * NAT_WIN_ROWS + 1 - n_dr), (0, GRID_W - n_dc)))
    even = padded[:, :, 0:2 * NAT_WIN_ROWS].reshape(L, NAT_HEADS, NAT_WIN_ROWS, LANES)
    odd = padded[:, :, 1:2 * NAT_WIN_ROWS + 1].reshape(L, NAT_HEADS, NAT_WIN_ROWS, LANES)
    return jnp.stack([even, odd], axis=1)


def _feature_major(cache):
    B, L, H, S, d = cache.shape
    return jnp.swapaxes(cache, -1, -2).reshape(B, L, H * d, S)


def kernel(x_prompt, x_sample, cache_mla_ckv, cache_mla_krope, cache_diff_k, cache_diff_v, cache_nat_k, cache_nat_v, c, c_ctx, w_ada, b_ada, ffn1_norm, ffn1_w_gate, ffn1_w_up, ffn1_w_down, mix_norm, w_in, mla_q_norm, mla_w_uq, mla_kv_norm, mla_w_ukv, diff_lambda_q1, diff_lambda_k1, diff_lambda_q2, diff_lambda_k2, diff_subln, nat_rpb, w_out, ffn2_norm, ffn2_w_gate, ffn2_w_up, ffn2_w_down, final_norm):
    L = w_ada.shape[0]
    B, S_ctx, _ = x_prompt.shape
    Bd, S_lat, _ = x_sample.shape
    assert S_ctx == CTX_SEQ and S_lat == LAT_SEQ and 1 + Bd <= ADA_ROWS
    assert cache_mla_ckv.shape[2] == CTX_SEQ and (B * CTX_SEQ) % FFN_TILE == 0

    cvec = jnp.concatenate([c_ctx[None, :], c, jnp.zeros((ADA_ROWS - 1 - Bd, D_MODEL), F32)], axis=0)
    b_ada3 = b_ada.reshape(L, 1, N_ADA * D_MODEL)
    ada = _ada_call(cvec, w_ada, b_ada3, 0)

    bf = lambda w: w.astype(BF16)
    win = _win_prep_call(w_in)
    wuq4 = mla_w_uq.reshape(L, MLA_Q_LORA, MLA_HEADS, MLA_NOPE + MLA_ROPE)
    wuq = bf(jnp.concatenate([wuq4[..., :MLA_NOPE].reshape(L, MLA_Q_LORA, -1),
                              wuq4[..., MLA_NOPE:].reshape(L, MLA_Q_LORA, -1)], axis=2))
    wukv4 = mla_w_ukv.reshape(L, MLA_KV_LORA, MLA_HEADS, MLA_NOPE + MLA_V)
    wukv = bf(jnp.concatenate([wukv4[..., :MLA_NOPE].reshape(L, MLA_KV_LORA, -1),
                               wukv4[..., MLA_NOPE:].reshape(L, MLA_KV_LORA, -1)], axis=2))
    wout = bf(w_out)
    lamv = jnp.stack([diff_lambda_q1, diff_lambda_k1, diff_lambda_q2, diff_lambda_k2], axis=1)
    subln = jnp.tile(diff_subln, (1, DIFF_HEADS)).reshape(L, HEAD_BLOCK, 1)
    r3 = lambda a: a.reshape(L, 1, a.shape[-1])
    n1, n2, nm, qn, kvn = r3(ffn1_norm), r3(ffn2_norm), r3(mix_norm), r3(mla_q_norm), r3(mla_kv_norm)
    fnorm = final_norm.reshape(1, D_MODEL)

    caches = (bf(cache_mla_ckv), jnp.swapaxes(cache_mla_krope, -1, -2),
              _feature_major(cache_diff_k), _feature_major(cache_diff_v),
              _feature_major(cache_nat_k), _feature_major(cache_nat_v))
    rope_tabs = (jnp.asarray(_ROPE_COS), jnp.asarray(_ROPE_SNEXT), jnp.asarray(_ROPE_SPREV))
    rpb_rows = _rpb_rows(nat_rpb)

    xi = x_prompt.reshape(B * CTX_SEQ, D_MODEL)
    xs = x_sample.reshape(Bd * LAT_SEQ, D_MODEL)
    nb = MIX_CTX_BATCH if B % MIX_CTX_BATCH == 0 else FFN_TILE // CTX_SEQ
    states = None
    for l in range(L):
        lam_init = 0.8 - 0.6 * math.exp(-0.3 * l)
        last = l == L - 1
        ada4 = ada.reshape(ADA_ROWS, 3, 3, D_MODEL)
        xi, xs, *st = _ffn_call(xi, xs, ada4, l, 0, n1, ffn1_w_gate, ffn1_w_up, ffn1_w_down, None, "ffn1",
                                fill_layers=L if l == 0 else 0)
        states = tuple(st) if l == 0 else states
        xi, states = _mix_i_call(xi, ada4, l, L, nm, win, qn, wuq, kvn, wukv, lamv, subln, wout, states, nb, lam_init)
        xs = _mix_ii_call(xs, ada4, l, nm, win, qn, wuq, kvn, wukv, lamv, subln, wout, rope_tabs,
                          rpb_rows, caches, lam_init)
        xi, xs, *nxt = _ffn_call(xi, xs, ada4, l, 2, n2, ffn2_w_gate, ffn2_w_up, ffn2_w_down,
                                 fnorm if last else None, "ffn2",
                                 next_ada=None if last else (cvec, w_ada, b_ada3))
        ada = None if last else nxt[0]

    states = (states[0],) + tuple(jnp.swapaxes(s, -1, -2) for s in states[1:])
    return (xi.reshape(B, CTX_SEQ, D_MODEL), xs.reshape(Bd, LAT_SEQ, D_MODEL)) + states
```
